```python
import math
import jax, jax.numpy as jnp
from jax import lax
import numpy as np


D_MODEL = 2048
BATCH = 8
SEQ = 8192
DEPTH = 4

F32 = jnp.float32
EPS = 1e-6

HG_HEADS = 6
HG_DK = 128
HG_DV = 128
HG_WIDTH = HG_HEADS * HG_DV
HG_CHUNK = 64

RET_HEADS = 6
RET_DK = 64
RET_DV = 128
RET_WIDTH = RET_HEADS * RET_DV
RET_CHUNK = 128
ROPE_BASE = 10000.0

DIL_SLOTS = 4
DIL_HD = 128
DIL_GROUPS = ((128, 1), (512, 4), (2048, 16))
DIL_WIDTH = DIL_SLOTS * DIL_HD
DIL_HEADS = DIL_SLOTS * len(DIL_GROUPS)

MIX_WIDTH = HG_WIDTH + RET_WIDTH + DIL_WIDTH
D_FF = 4 * D_MODEL
REL_BUCKETS = 32
REL_MAX_DIST = 1024

IN_SPLITS = (HG_HEADS * HG_DK, HG_WIDTH, HG_HEADS * HG_DK, HG_HEADS * HG_DK, HG_WIDTH,
             RET_HEADS * RET_DK, RET_HEADS * RET_DK, RET_WIDTH, RET_WIDTH) + (DIL_WIDTH,) * (3 * len(DIL_GROUPS))
IN_WIDTH = sum(IN_SPLITS)

kernel_name = 'hybrid_hgrn2_retnet_dilated_encoder'


def rms_norm(x, g):
    xf = x.astype(F32)
    y = xf * lax.rsqrt(jnp.mean(xf * xf, axis=-1, keepdims=True) + EPS)
    return (y * g.astype(F32)).astype(x.dtype)


def head_rms(t, gain):
    return t * lax.rsqrt(jnp.mean(t * t, axis=-1, keepdims=True) + EPS) * gain.astype(F32)


def to_heads(t, n_heads):
    B, S, W = t.shape
    return t.astype(F32).reshape(B, S, n_heads, W // n_heads).transpose(0, 2, 1, 3)


def rope(t):
    S, d = t.shape[1], t.shape[-1]
    half = d // 2
    inv = ROPE_BASE ** (-jnp.arange(half, dtype=F32) / half)
    ang = jnp.arange(S, dtype=F32)[:, None] * inv[None, :]
    cos = jnp.cos(ang)[None, :, None, :]
    sin = jnp.sin(ang)[None, :, None, :]
    t1, t2 = t[..., :half], t[..., half:]
    return jnp.concatenate([t1 * cos - t2 * sin, t1 * sin + t2 * cos], axis=-1)


def hgrn2_chunk_scan(q, k, v, log_f):
    B, H, S, DK = q.shape
    DV = v.shape[-1]
    C = HG_CHUNK
    nc = S // C

    def chunks(t):
        return t.reshape(B, H, nc, C, t.shape[-1]).transpose(2, 0, 1, 3, 4)

    mask = jnp.tril(jnp.ones((C, C), dtype=bool))[:, :, None]

    def step(state, inp):
        qc, kc, vc, lc = inp
        b = jnp.cumsum(lc, axis=2)
        diff = b[:, :, :, None, :] - b[:, :, None, :, :]
        decay = jnp.exp(jnp.where(mask, diff, -jnp.inf))
        attn = jnp.einsum('bhik,bhjk,bhijk->bhij', qc, kc, decay)
        out = (jnp.einsum('bhij,bhjv->bhiv', attn, vc)
               + jnp.einsum('bhik,bhkv->bhiv', qc * jnp.exp(b), state))
        b_last = b[:, :, -1:, :]
        state = (jnp.exp(b_last)[:, :, 0, :, None] * state
                 + jnp.einsum('bhjk,bhjv->bhkv', kc * jnp.exp(b_last - b), vc))
        return state, out

    state0 = jnp.zeros((B, H, DK, DV), F32)
    _, out = lax.scan(step, state0, (chunks(q), chunks(k), chunks(v), chunks(log_f)))
    return out.transpose(1, 2, 0, 3, 4).reshape(B, H, S, DV)


def hgrn2_mixer(q, i_in, z_fwd, z_bwd, gate, lb_fwd, lb_bwd, norm_g):
    B, S, _ = q.shape
    qh = to_heads(q, HG_HEADS)
    vh = to_heads(i_in, HG_HEADS)

    def one_direction(z, lb, reverse):
        lbh = lb.astype(F32).reshape(HG_HEADS, 1, HG_DK)
        f = lbh + (1.0 - lbh) * jax.nn.sigmoid(to_heads(z, HG_HEADS))
        k = 1.0 - f
        lf = jnp.log(f)
        if reverse:
            fl = lambda t: jnp.flip(t, axis=2)
            return fl(hgrn2_chunk_scan(fl(qh), fl(k), fl(vh), fl(lf)))
        return hgrn2_chunk_scan(qh, k, vh, lf)

    o = one_direction(z_fwd, lb_fwd, False) + one_direction(z_bwd, lb_bwd, True)
    o = o.transpose(0, 2, 1, 3)
    o = o * lax.rsqrt(jnp.mean(o * o, axis=-1, keepdims=True) + EPS)
    o = o.reshape(B, S, HG_WIDTH) * norm_g.astype(F32)
    return o * jax.nn.silu(gate.astype(F32))


def retention_chunk(q, k, v, log_gamma):
    B, H, S, DK = q.shape
    DV = v.shape[-1]
    C = RET_CHUNK
    nc = S // C
    qc = q.reshape(B, H, nc, C, DK)
    kc = k.reshape(B, H, nc, C, DK)
    vc = v.reshape(B, H, nc, C, DV)
    idx = jnp.arange(C, dtype=F32)
    lg = log_gamma[:, None]
    rel = idx[:, None] - idx[None, :]
    decay = jnp.where(rel >= 0, jnp.exp(lg[:, :, None] * jnp.maximum(rel, 0.0)), 0.0)
    scores = jnp.einsum('bhnid,bhnjd->bhnij', qc, kc) * decay[None, :, None]
    intra = jnp.einsum('bhnij,bhnje->bhnie', scores, vc)
    zeta = jnp.exp(lg * (C - 1 - idx))
    xi = jnp.exp(lg * (idx + 1))
    kv = jnp.einsum('bhnjd,hj,bhnje->nbhde', kc, zeta, vc)
    chunk_decay = jnp.exp(log_gamma * C)[None, :, None, None]

    def step(state, kv_n):
        return chunk_decay * state + kv_n, state

    _, prev = lax.scan(step, jnp.zeros((B, H, DK, DV), F32), kv)
    cross = jnp.einsum('bhnid,hi,nbhde->bhnie', qc, xi, prev)
    return (intra + cross).reshape(B, H, S, DV)


def retention_mixer(q, k, v, gate, norm_g):
    B, S, _ = q.shape
    qh = rope(q.astype(F32).reshape(B, S, RET_HEADS, RET_DK)).transpose(0, 2, 1, 3)
    kh = rope(k.astype(F32).reshape(B, S, RET_HEADS, RET_DK)).transpose(0, 2, 1, 3) * RET_DK ** -0.5
    vh = to_heads(v, RET_HEADS)
    hidx = jnp.arange(RET_HEADS, dtype=F32)
    log_g_fwd = jnp.log1p(-jnp.exp2(-5.0 - hidx))
    log_g_bwd = log_g_fwd[::-1]
    fl = lambda t: jnp.flip(t, axis=2)
    o = retention_chunk(qh, kh, vh, log_g_fwd) + fl(retention_chunk(fl(qh), fl(kh), fl(vh), log_g_bwd))
    o = o.transpose(0, 2, 1, 3)
    mu = jnp.mean(o, axis=-1, keepdims=True)
    var = jnp.mean(jnp.square(o - mu), axis=-1, keepdims=True)
    o = ((o - mu) * lax.rsqrt(var + EPS)).reshape(B, S, RET_WIDTH) * norm_g.astype(F32)
    return o * jax.nn.silu(gate.astype(F32))


def t5_bucket(rel):
    nb = REL_BUCKETS // 2
    max_exact = nb // 2
    sign_off = jnp.where(rel > 0, nb, 0)
    n = jnp.abs(rel)
    nf = jnp.maximum(n, 1).astype(F32)
    large = max_exact + (jnp.log(nf / max_exact) / math.log(REL_MAX_DIST / max_exact)
                         * (nb - max_exact)).astype(jnp.int32)
    large = jnp.minimum(large, nb - 1)
    return sign_off + jnp.where(n < max_exact, n, large)


def dilated_local_attention(q, k, v, bias_table, dil, half):
    B, H, S, D = q.shape
    L = S // dil
    nb = -(-L // half)
    Lp = nb * half

    def to_res(t):
        return t.reshape(B, H, L, dil, D).transpose(0, 1, 3, 2, 4)

    qr, kr, vr = to_res(q), to_res(k), to_res(v)
    qb = jnp.pad(qr, ((0, 0), (0, 0), (0, 0), (0, Lp - L), (0, 0))).reshape(B, H, dil, nb, half, D)

    def band(t):
        tp = jnp.pad(t, ((0, 0), (0, 0), (0, 0), (half, Lp - L + half), (0, 0)))
        tp = tp.reshape(B, H, dil, nb + 2, half, D)
        return jnp.concatenate([tp[:, :, :, :-2], tp[:, :, :, 1:-1], tp[:, :, :, 2:]], axis=4)

    kb, vb = band(kr), band(vr)
    ii = jnp.arange(half)[:, None]
    jj = jnp.arange(3 * half)[None, :]
    rel = jj - half - ii
    bias = bias_table.astype(F32)[t5_bucket(rel * dil)].transpose(2, 0, 1)
    key_idx = jnp.arange(nb)[:, None, None] * half + jj[None] - half
    valid = (jnp.abs(rel) <= half)[None] & (key_idx >= 0) & (key_idx < L)
    s = jnp.einsum('bhrnqd,bhrnkd->bhrnqk', qb, kb) + bias[None, :, None, None]
    s = jnp.where(valid, s, -jnp.inf)
    m = jnp.max(s, axis=-1, keepdims=True)
    p = jnp.exp(s - m)
    den = jnp.sum(p, axis=-1)
    o = jnp.einsum('bhrnqk,bhrnkd->bhrnqd', p, vb) / den[..., None]
    lse = m[..., 0] + jnp.log(den)
    o = o.reshape(B, H, dil, Lp, D)[:, :, :, :L].transpose(0, 1, 3, 2, 4).reshape(B, H, S, D)
    lse = lse.reshape(B, H, dil, Lp)[..., :L].transpose(0, 1, 3, 2).reshape(B, H, S)
    return o, lse


def dilated_mixer(parts, rel_bias, q_gain, k_gain):
    B, S, _ = parts[0].shape
    outs, lses = [], []
    for g, (window, dil) in enumerate(DIL_GROUPS):
        half = window // (2 * dil)
        q = head_rms(parts[3 * g].astype(F32).reshape(B, S, DIL_SLOTS, DIL_HD), q_gain) * DIL_HD ** -0.5
        k = head_rms(parts[3 * g + 1].astype(F32).reshape(B, S, DIL_SLOTS, DIL_HD), k_gain)
        v = parts[3 * g + 2].astype(F32).reshape(B, S, DIL_SLOTS, DIL_HD)
        tbl = rel_bias[:, g * DIL_SLOTS:(g + 1) * DIL_SLOTS]
        o, lse = dilated_local_attention(q.transpose(0, 2, 1, 3), k.transpose(0, 2, 1, 3),
                                         v.transpose(0, 2, 1, 3), tbl, dil, half)
        outs.append(o)
        lses.append(lse)
    w = jax.nn.softmax(jnp.stack(lses, axis=0), axis=0)
    o = jnp.sum(w[..., None] * jnp.stack(outs, axis=0), axis=0)
    return o.transpose(0, 2, 1, 3).reshape(B, S, DIL_WIDTH)


def _fwd_setup_inputs(seed: int = 0) -> dict:
    key = jax.random.key(seed)
    ks = jax.random.split(key, 14)

    def nrm(k, shape, scale):
        return jax.random.normal(k, shape, F32) * scale

    return {
        'x': nrm(ks[0], (BATCH, SEQ, D_MODEL), 1.0),
        'w_in': nrm(ks[1], (DEPTH, D_MODEL, IN_WIDTH), D_MODEL ** -0.5),
        'w_out': nrm(ks[2], (DEPTH, MIX_WIDTH, D_MODEL), MIX_WIDTH ** -0.5),
        'w_up': nrm(ks[3], (DEPTH, D_MODEL, D_FF), D_MODEL ** -0.5),
        'w_down': nrm(ks[4], (DEPTH, D_FF, D_MODEL), D_FF ** -0.5),
        'norm_mix': 1.0 + nrm(ks[5], (DEPTH, D_MODEL), 0.02),
        'norm_mlp': 1.0 + nrm(ks[6], (DEPTH, D_MODEL), 0.02),
        'hg_lb_fwd': nrm(ks[7], (DEPTH, HG_HEADS * HG_DK), 0.5),
        'hg_lb_bwd': nrm(ks[8], (DEPTH, HG_HEADS * HG_DK), 0.5),
        'hg_norm': 1.0 + nrm(ks[9], (DEPTH, HG_WIDTH), 0.02),
        'ret_norm': 1.0 + nrm(ks[10], (DEPTH, RET_WIDTH), 0.02),
        'q_norm': 1.0 + nrm(ks[11], (DEPTH, DIL_HD), 0.02),
        'k_norm': 1.0 + nrm(ks[12], (DEPTH, DIL_HD), 0.02),
        'rel_bias': nrm(ks[13], (REL_BUCKETS, DIL_HEADS), 0.1),
    }


def _fwd_reference(x, w_in, w_out, w_up, w_down, norm_mix, norm_mlp, hg_lb_fwd, hg_lb_bwd,
              hg_norm, ret_norm, q_norm, k_norm, rel_bias):
    lb_fwd_all = jnp.cumsum(jax.nn.softmax(hg_lb_fwd.astype(F32), axis=0), axis=0)
    lb_bwd_all = jnp.cumsum(jax.nn.softmax(hg_lb_bwd.astype(F32), axis=0), axis=0)
    offsets = np.cumsum(IN_SPLITS)[:-1].tolist()
    for l in range(DEPTH):
        h = rms_norm(x, norm_mix[l]) @ w_in[l]
        p = jnp.split(h, offsets, axis=-1)
        y_a = hgrn2_mixer(p[0], p[1], p[2], p[3], p[4],
                          lb_fwd_all[l] - lb_fwd_all[0], lb_bwd_all[l] - lb_bwd_all[0], hg_norm[l])
        y_b = retention_mixer(p[5], p[6], p[7], p[8], ret_norm[l])
        y_c = dilated_mixer(p[9:], rel_bias, q_norm[l], k_norm[l])
        y = jnp.concatenate([y_a, y_b, y_c], axis=-1).astype(x.dtype)
        x = x + y @ w_out[l]
        hm = rms_norm(x, norm_mlp[l])
        x = x + jnp.square(jax.nn.relu(hm @ w_up[l])) @ w_down[l]
    return x


import jax as _jax
import jax.numpy as _jnp

TWIN_FORMAT = 'train_step'
FWD_PARAMS = ['x', 'w_in', 'w_out', 'w_up', 'w_down', 'norm_mix', 'norm_mlp', 'hg_lb_fwd', 'hg_lb_bwd', 'hg_norm', 'ret_norm', 'q_norm', 'k_norm', 'rel_bias']
TWIN_WEIGHTS = ['w_in', 'w_out', 'w_up', 'w_down', 'norm_mix', 'norm_mlp', 'hg_lb_fwd', 'hg_lb_bwd', 'hg_norm', 'ret_norm', 'q_norm', 'k_norm', 'rel_bias']
TWIN_DIFF_INPUT = 'x'
TWIN_INPUTS = ['x', 'w_in', 'w_out', 'w_up', 'w_down', 'norm_mix', 'norm_mlp', 'hg_lb_fwd', 'hg_lb_bwd', 'hg_norm', 'ret_norm', 'q_norm', 'k_norm', 'rel_bias', 'loss_target', 'm_w_in', 'm_w_out', 'm_w_up', 'm_w_down', 'm_norm_mix', 'm_norm_mlp', 'm_hg_lb_fwd', 'm_hg_lb_bwd', 'm_hg_norm', 'm_ret_norm', 'm_q_norm', 'm_k_norm', 'm_rel_bias', 'v_w_in', 'v_w_out', 'v_w_up', 'v_w_down', 'v_norm_mix', 'v_norm_mlp', 'v_hg_lb_fwd', 'v_hg_lb_bwd', 'v_hg_norm', 'v_ret_norm', 'v_q_norm', 'v_k_norm', 'v_rel_bias']
TWIN_OUTPUTS = ['loss', 'grad_x', 'grad_w_in', 'grad_w_out', 'grad_w_up', 'grad_w_down', 'grad_norm_mix', 'grad_norm_mlp', 'grad_hg_lb_fwd', 'grad_hg_lb_bwd', 'grad_hg_norm', 'grad_ret_norm', 'grad_q_norm', 'grad_k_norm', 'grad_rel_bias', 'delta_w_in', 'delta_w_out', 'delta_w_up', 'delta_w_down', 'delta_norm_mix', 'delta_norm_mlp', 'delta_hg_lb_fwd', 'delta_hg_lb_bwd', 'delta_hg_norm', 'delta_ret_norm', 'delta_q_norm', 'delta_k_norm', 'delta_rel_bias', 'new_m_w_in', 'new_m_w_out', 'new_m_w_up', 'new_m_w_down', 'new_m_norm_mix', 'new_m_norm_mlp', 'new_m_hg_lb_fwd', 'new_m_hg_lb_bwd', 'new_m_hg_norm', 'new_m_ret_norm', 'new_m_q_norm', 'new_m_k_norm', 'new_m_rel_bias', 'new_v_w_in', 'new_v_w_out', 'new_v_w_up', 'new_v_w_down', 'new_v_norm_mix', 'new_v_norm_mlp', 'new_v_hg_lb_fwd', 'new_v_hg_lb_bwd', 'new_v_hg_norm', 'new_v_ret_norm', 'new_v_q_norm', 'new_v_k_norm', 'new_v_rel_bias']
TWIN_LEAF_KINDS = {'loss': 'loss', 'grad_x': 'grad_x', 'grad_w_in': 'grad_w', 'grad_w_out': 'grad_w', 'grad_w_up': 'grad_w', 'grad_w_down': 'grad_w', 'grad_norm_mix': 'grad_w', 'grad_norm_mlp': 'grad_w', 'grad_hg_lb_fwd': 'grad_w', 'grad_hg_lb_bwd': 'grad_w', 'grad_hg_norm': 'grad_w', 'grad_ret_norm': 'grad_w', 'grad_q_norm': 'grad_w', 'grad_k_norm': 'grad_w', 'grad_rel_bias': 'grad_w', 'delta_w_in': 'delta_w', 'delta_w_out': 'delta_w', 'delta_w_up': 'delta_w', 'delta_w_down': 'delta_w', 'delta_norm_mix': 'delta_w', 'delta_norm_mlp': 'delta_w', 'delta_hg_lb_fwd': 'delta_w', 'delta_hg_lb_bwd': 'delta_w', 'delta_hg_norm': 'delta_w', 'delta_ret_norm': 'delta_w', 'delta_q_norm': 'delta_w', 'delta_k_norm': 'delta_w', 'delta_rel_bias': 'delta_w', 'new_m_w_in': 'new_m', 'new_m_w_out': 'new_m', 'new_m_w_up': 'new_m', 'new_m_w_down': 'new_m', 'new_m_norm_mix': 'new_m', 'new_m_norm_mlp': 'new_m', 'new_m_hg_lb_fwd': 'new_m', 'new_m_hg_lb_bwd': 'new_m', 'new_m_hg_norm': 'new_m', 'new_m_ret_norm': 'new_m', 'new_m_q_norm': 'new_m', 'new_m_k_norm': 'new_m', 'new_m_rel_bias': 'new_m', 'new_v_w_in': 'new_v', 'new_v_w_out': 'new_v', 'new_v_w_up': 'new_v', 'new_v_w_down': 'new_v', 'new_v_norm_mix': 'new_v', 'new_v_norm_mlp': 'new_v', 'new_v_hg_lb_fwd': 'new_v', 'new_v_hg_lb_bwd': 'new_v', 'new_v_hg_norm': 'new_v', 'new_v_ret_norm': 'new_v', 'new_v_q_norm': 'new_v', 'new_v_k_norm': 'new_v', 'new_v_rel_bias': 'new_v'}


def _forward(args):
    return _fwd_reference(*[args[k] for k in FWD_PARAMS])


def _output_shape():
    def fwd():
        inp = _fwd_setup_inputs(0)
        return _fwd_reference(*[inp[k] for k in FWD_PARAMS])
    out = _jax.eval_shape(fwd)
    return out.shape, out.dtype

N_MICROBATCH = 1
ADAM_LR = 0.001
ADAM_B1 = 0.9
ADAM_B2 = 0.999
ADAM_EPS = 1e-08
ADAM_WD = 0.01
ADAM_STEP = 10
PER_EXAMPLE_BATCH_AXIS = {'x': 0, 'loss_target': 0}
SHARED_INPUTS = []
_WEIGHT_DTYPES = {'w_in': _jnp.float32, 'w_out': _jnp.float32, 'w_up': _jnp.float32, 'w_down': _jnp.float32, 'norm_mix': _jnp.float32, 'norm_mlp': _jnp.float32, 'hg_lb_fwd': _jnp.float32, 'hg_lb_bwd': _jnp.float32, 'hg_norm': _jnp.float32, 'ret_norm': _jnp.float32, 'q_norm': _jnp.float32, 'k_norm': _jnp.float32, 'rel_bias': _jnp.float32}
MOMENT_SCALE = {'w_in': 2.397995e+00, 'w_out': 3.634945e+00, 'w_up': 6.550500e+00, 'w_down': 2.492665e+01, 'norm_mix': 1.136602e+01, 'norm_mlp': 9.744431e+01, 'hg_lb_fwd': 1.650532e-01, 'hg_lb_bwd': 1.512707e-01, 'hg_norm': 1.499497e+01, 'ret_norm': 1.479510e+01, 'q_norm': 1.268491e+00, 'k_norm': 1.258792e+00, 'rel_bias': 3.212174e+00}


def _to_microbatches(a, axis):
    t = _jnp.moveaxis(a, axis, 0)
    t = t.reshape((N_MICROBATCH, t.shape[0] // N_MICROBATCH) + t.shape[1:])
    return _jnp.moveaxis(t, 1, axis + 1)


def setup_inputs(seed: int = 0) -> dict:
    inp = _fwd_setup_inputs(seed)
    key = _jax.random.fold_in(_jax.random.key(seed), 7919)
    shape, _ = _output_shape()
    out = dict(inp)
    out["loss_target"] = _jax.random.normal(_jax.random.fold_in(key, 0), shape, _jnp.float32)
    for i, name in enumerate(TWIN_WEIGHTS):
        w = inp[name].astype(_jnp.float32)
        if MOMENT_SCALE is None:
            s = _jnp.sqrt(_jnp.mean(_jnp.square(w)) + 1e-30)
        else:
            s = MOMENT_SCALE[name]
        km, kv = _jax.random.split(_jax.random.fold_in(key, i + 1))
        out[name] = w
        out["m_" + name] = s * _jax.random.normal(km, w.shape, _jnp.float32)
        out["v_" + name] = (s * s) * _jax.random.uniform(kv, w.shape, _jnp.float32, 0.5, 1.5)
    if N_MICROBATCH > 1:
        for name, axis in PER_EXAMPLE_BATCH_AXIS.items():
            out[name] = _to_microbatches(out[name], axis)
    return {'x': out['x'], 'w_in': out['w_in'], 'w_out': out['w_out'], 'w_up': out['w_up'], 'w_down': out['w_down'], 'norm_mix': out['norm_mix'], 'norm_mlp': out['norm_mlp'], 'hg_lb_fwd': out['hg_lb_fwd'], 'hg_lb_bwd': out['hg_lb_bwd'], 'hg_norm': out['hg_norm'], 'ret_norm': out['ret_norm'], 'q_norm': out['q_norm'], 'k_norm': out['k_norm'], 'rel_bias': out['rel_bias'], 'loss_target': out['loss_target'], 'm_w_in': out['m_w_in'], 'm_w_out': out['m_w_out'], 'm_w_up': out['m_w_up'], 'm_w_down': out['m_w_down'], 'm_norm_mix': out['m_norm_mix'], 'm_norm_mlp': out['m_norm_mlp'], 'm_hg_lb_fwd': out['m_hg_lb_fwd'], 'm_hg_lb_bwd': out['m_hg_lb_bwd'], 'm_hg_norm': out['m_hg_norm'], 'm_ret_norm': out['m_ret_norm'], 'm_q_norm': out['m_q_norm'], 'm_k_norm': out['m_k_norm'], 'm_rel_bias': out['m_rel_bias'], 'v_w_in': out['v_w_in'], 'v_w_out': out['v_w_out'], 'v_w_up': out['v_w_up'], 'v_w_down': out['v_w_down'], 'v_norm_mix': out['v_norm_mix'], 'v_norm_mlp': out['v_norm_mlp'], 'v_hg_lb_fwd': out['v_hg_lb_fwd'], 'v_hg_lb_bwd': out['v_hg_lb_bwd'], 'v_hg_norm': out['v_hg_norm'], 'v_ret_norm': out['v_ret_norm'], 'v_q_norm': out['v_q_norm'], 'v_k_norm': out['v_k_norm'], 'v_rel_bias': out['v_rel_bias']}


def _loss(weights, diff, rest, loss_target):
    with _jax.named_scope("forward"):
        args = {**rest, TWIN_DIFF_INPUT: diff, **{k: w.astype(_WEIGHT_DTYPES[k]) for k, w in weights.items()}}
        y = _forward(args)
    with _jax.named_scope("loss_head"):
        err = _jnp.square(y.astype(_jnp.float32) - loss_target)
        return 0.5 * _jnp.sum(_jnp.mean(err, axis=-1)) if err.ndim else 0.5 * err


def _adamw(w, g, m, v):
    m = ADAM_B1 * m + (1.0 - ADAM_B1) * g
    v = ADAM_B2 * v + (1.0 - ADAM_B2) * _jnp.square(g)
    m_hat = m / (1.0 - ADAM_B1 ** ADAM_STEP)
    v_hat = v / (1.0 - ADAM_B2 ** ADAM_STEP)
    delta = -ADAM_LR * (m_hat / (_jnp.sqrt(v_hat) + ADAM_EPS) + ADAM_WD * w)
    return delta, m, v


def reference(x, w_in, w_out, w_up, w_down, norm_mix, norm_mlp, hg_lb_fwd, hg_lb_bwd, hg_norm, ret_norm, q_norm, k_norm, rel_bias, loss_target, m_w_in, m_w_out, m_w_up, m_w_down, m_norm_mix, m_norm_mlp, m_hg_lb_fwd, m_hg_lb_bwd, m_hg_norm, m_ret_norm, m_q_norm, m_k_norm, m_rel_bias, v_w_in, v_w_out, v_w_up, v_w_down, v_norm_mix, v_norm_mlp, v_hg_lb_fwd, v_hg_lb_bwd, v_hg_norm, v_ret_norm, v_q_norm, v_k_norm, v_rel_bias):
    given = dict(x=x, w_in=w_in, w_out=w_out, w_up=w_up, w_down=w_down, norm_mix=norm_mix, norm_mlp=norm_mlp, hg_lb_fwd=hg_lb_fwd, hg_lb_bwd=hg_lb_bwd, hg_norm=hg_norm, ret_norm=ret_norm, q_norm=q_norm, k_norm=k_norm, rel_bias=rel_bias, loss_target=loss_target, m_w_in=m_w_in, m_w_out=m_w_out, m_w_up=m_w_up, m_w_down=m_w_down, m_norm_mix=m_norm_mix, m_norm_mlp=m_norm_mlp, m_hg_lb_fwd=m_hg_lb_fwd, m_hg_lb_bwd=m_hg_lb_bwd, m_hg_norm=m_hg_norm, m_ret_norm=m_ret_norm, m_q_norm=m_q_norm, m_k_norm=m_k_norm, m_rel_bias=m_rel_bias, v_w_in=v_w_in, v_w_out=v_w_out, v_w_up=v_w_up, v_w_down=v_w_down, v_norm_mix=v_norm_mix, v_norm_mlp=v_norm_mlp, v_hg_lb_fwd=v_hg_lb_fwd, v_hg_lb_bwd=v_hg_lb_bwd, v_hg_norm=v_hg_norm, v_ret_norm=v_ret_norm, v_q_norm=v_q_norm, v_k_norm=v_k_norm, v_rel_bias=v_rel_bias)
    weights = {n: given[n] for n in TWIN_WEIGHTS}
    shared = {n: given[n] for n in SHARED_INPUTS}
    per_example = {n: given[n] for n in ['x']}
    grad_fn = _jax.value_and_grad(_loss, argnums=(0, 1))

    def one_microbatch(ex, loss_target):
        ex = dict(ex)
        diff = ex.pop(TWIN_DIFF_INPUT)
        return grad_fn(weights, diff, {**shared, **ex}, loss_target)

    if N_MICROBATCH == 1:
        loss, (grad_w, grad_x) = one_microbatch(per_example, given["loss_target"])
    else:
        def body(carry, xs):
            loss_sum, grad_sum = carry
            l_k, (gw_k, gx_k) = one_microbatch(xs[0], xs[1])
            with _jax.named_scope("update"):
                return (loss_sum + l_k, _jax.tree.map(_jnp.add, grad_sum, gw_k)), gx_k

        init = (_jnp.zeros((), _jnp.float32), _jax.tree.map(_jnp.zeros_like, weights))
        (loss, grad_w), grad_x = _jax.lax.scan(body, init, (per_example, given["loss_target"]))
    with _jax.named_scope("update"):
        delta_w, new_m, new_v = {}, {}, {}
        for n in TWIN_WEIGHTS:
            delta_w[n], new_m[n], new_v[n] = _adamw(weights[n], grad_w[n], given["m_" + n], given["v_" + n])
    return (loss, grad_x, *[grad_w[n] for n in TWIN_WEIGHTS], *[delta_w[n] for n in TWIN_WEIGHTS],
            *[new_m[n] for n in TWIN_WEIGHTS], *[new_v[n] for n in TWIN_WEIGHTS])
```

```python
import functools
import math

import numpy as np
import jax
import jax.numpy as jnp
from jax import lax
from jax.experimental import pallas as pl
from jax.experimental.pallas import tpu as pltpu

F32 = jnp.float32
MXU = jnp.bfloat16
WIRE = jnp.bfloat16
EPS = 1e-6
N_DEV = 8
VMEM_LIMIT = 48 * 1024 * 1024

HG_HEADS, HG_D = 6, 128
RET_HEADS, RET_DK, RET_DV = 6, 64, 128
DIL_SLOTS, DIL_HD = 4, 128
DIL_GROUPS = ((128, 1), (512, 4), (2048, 16))
HGW = HG_HEADS * HG_D
RETW = RET_HEADS * RET_DV
DILW = DIL_SLOTS * DIL_HD
IN_SPLITS = (HGW, HGW, HGW, HGW, HGW, RET_HEADS * RET_DK, RET_HEADS * RET_DK, RETW, RETW) + (DILW,) * 9
REL_BUCKETS, REL_MAX_DIST = 32, 1024
ROPE_BASE = 10000.0
ADAM_LR, ADAM_B1, ADAM_B2, ADAM_EPS, ADAM_WD, ADAM_STEP = 0.001, 0.9, 0.999, 1e-08, 0.01, 10

SCAN_C = 128
HG_SUB = 16
EXP_CLAMP = 60.0
ATT_SB, ATT_HALO = 128, 64
ATT_WIN = ATT_SB + 2 * ATT_HALO
MESH_ID = pl.DeviceIdType.MESH


def _pick(n, prefs):
    for p in prefs:
        if n % p == 0:
            return p
    return n


def _params(*sem):
    return pltpu.CompilerParams(dimension_semantics=sem, vmem_limit_bytes=VMEM_LIMIT)


def _dot(a, b):
    return lax.dot_general(a.astype(MXU), b.astype(MXU), (((1,), (0,)), ((), ())), preferred_element_type=F32)


def _dot_nt(a, b):
    return lax.dot_general(a.astype(MXU), b.astype(MXU), (((1,), (1,)), ((), ())), preferred_element_type=F32)


def _dot01(sel, x):
    if MXU == F32:
        return _dot(sel, x)
    hi = x.astype(MXU)
    r1 = x - hi.astype(F32)
    mid = r1.astype(MXU)
    lo = (r1 - mid.astype(F32)).astype(MXU)
    return _dot(sel, hi) + _dot(sel, mid) + _dot(sel, lo)


def _mm(a, b, *, nt=False, out_dtype=F32, res=None, u_in=None, emit_act=False, name):
    M, K = a.shape
    N = b.shape[0] if nt else b.shape[1]
    tm = _pick(M, (1024, 512, 256, 128))
    tn = _pick(N, (1024, 768, 512, 384, 256, 128))
    tk = _pick(K, (2048, 1536, 1024, 512, 256, 128))
    nk = K // tk

    def body(*refs):
        it = iter(refs)
        a_ref, b_ref = next(it), next(it)
        res_ref = next(it) if res is not None else None
        u_ref = next(it) if u_in is not None else None
        o_ref = next(it)
        act_ref = next(it) if emit_act else None
        acc_ref = next(it)
        k = pl.program_id(2)

        @pl.when(k == 0)
        def _():
            acc_ref[...] = jnp.zeros_like(acc_ref)

        acc_ref[...] += (_dot_nt if nt else _dot)(a_ref[...], b_ref[...])

        @pl.when(k == nk - 1)
        def _():
            r = acc_ref[...]
            if res_ref is not None:
                r = r + res_ref[...]
            if u_ref is not None:
                r = r * (2.0 * jnp.maximum(u_ref[...], 0.0))
            o_ref[...] = r.astype(o_ref.dtype)
            if act_ref is not None:
                t = jnp.maximum(r, 0.0)
                act_ref[...] = (t * t).astype(act_ref.dtype)

    mn = pl.BlockSpec((tm, tn), lambda i, j, k: (i, j))
    in_specs = [pl.BlockSpec((tm, tk), lambda i, j, k: (i, k)),
                pl.BlockSpec((tn, tk), lambda i, j, k: (j, k)) if nt else pl.BlockSpec((tk, tn), lambda i, j, k: (k, j))]
    args = [a, b]
    for extra in (res, u_in):
        if extra is not None:
            in_specs.append(mn)
            args.append(extra)
    out_shape = [jax.ShapeDtypeStruct((M, N), out_dtype)]
    out_specs = [mn]
    if emit_act:
        out_shape.append(jax.ShapeDtypeStruct((M, N), MXU))
        out_specs.append(mn)
    out = pl.pallas_call(
        body, out_shape=out_shape, grid=(M // tm, N // tn, nk), in_specs=in_specs, out_specs=out_specs,
        scratch_shapes=[pltpu.VMEM((tm, tn), F32)], name=name,
        compiler_params=_params("parallel", "parallel", "arbitrary"))(*args)
    return out if emit_act else out[0]


def _gnorm_stats(x, center):
    if center:
        x = x - jnp.mean(x, axis=-1, keepdims=True)
    r = lax.rsqrt(jnp.mean(x * x, axis=-1, keepdims=True) + EPS)
    return x * r, r


def _silu_parts(gt):
    sg = jax.nn.sigmoid(gt)
    return gt * sg, sg * (1.0 + gt * (1.0 - sg))


def _gnorm_fwd(xs, gain, gate, *, group, center, scale, out_dtype, name):
    S, W = xs[0].shape
    ts = _pick(S, (512, 256, 128))
    nx = len(xs)

    def body(*refs):
        x_refs, g_ref = refs[:nx], refs[nx]
        gate_ref = refs[nx + 1] if gate is not None else None
        o_ref = refs[-1]
        for gi in range(W // group):
            sl = slice(gi * group, (gi + 1) * group)
            x = x_refs[0][:, sl]
            for xr in x_refs[1:]:
                x = x + xr[:, sl]
            n, _ = _gnorm_stats(x, center)
            y = n * (g_ref[:, sl] * scale)
            if gate_ref is not None:
                y = y * _silu_parts(gate_ref[:, sl])[0]
            o_ref[:, sl] = y.astype(o_ref.dtype)

    row = pl.BlockSpec((ts, W), lambda i: (i, 0))
    vec = pl.BlockSpec((1, W), lambda i: (0, 0))
    args = list(xs) + [gain] + ([gate] if gate is not None else [])
    in_specs = [row] * nx + [vec] + ([row] if gate is not None else [])
    return pl.pallas_call(body, out_shape=jax.ShapeDtypeStruct((S, W), out_dtype), grid=(S // ts,),
                          in_specs=in_specs, out_specs=row, name=name, compiler_params=_params("parallel"))(*args)


def _gnorm_bwd(dy, xs, gain, gate, *, group, center, scale, name):
    S, W = xs[0].shape
    ts = _pick(S, (512, 256, 128))
    nx = len(xs)

    def body(*refs):
        dy_ref = refs[0]
        x_refs, g_ref = refs[1:1 + nx], refs[1 + nx]
        gate_ref = refs[2 + nx] if gate is not None else None
        outs = refs[(3 + nx if gate is not None else 2 + nx):]
        dx_ref, dg_ref = outs[0], outs[1]
        dgate_ref = outs[2] if gate is not None else None

        @pl.when(pl.program_id(0) == 0)
        def _():
            dg_ref[...] = jnp.zeros_like(dg_ref)

        for gi in range(W // group):
            sl = slice(gi * group, (gi + 1) * group)
            x = x_refs[0][:, sl]
            for xr in x_refs[1:]:
                x = x + xr[:, sl]
            n, r = _gnorm_stats(x, center)
            dyv = dy_ref[:, sl].astype(F32)
            g = g_ref[:, sl] * scale
            if gate_ref is not None:
                act, dact = _silu_parts(gate_ref[:, sl])
                dgate_ref[:, sl] = dyv * n * g * dact
                dyv = dyv * act
            dg_ref[:, sl] += jnp.sum(dyv * n, axis=0, keepdims=True) * scale
            dn = dyv * g
            t = dn - n * jnp.mean(dn * n, axis=-1, keepdims=True)
            if center:
                t = t - jnp.mean(dn, axis=-1, keepdims=True)
            dx_ref[:, sl] = r * t

    row = pl.BlockSpec((ts, W), lambda i: (i, 0))
    vec = pl.BlockSpec((1, W), lambda i: (0, 0))
    args = [dy] + list(xs) + [gain] + ([gate] if gate is not None else [])
    in_specs = [row] * (1 + nx) + [vec] + ([row] if gate is not None else [])
    out_shape = [jax.ShapeDtypeStruct((S, W), F32), jax.ShapeDtypeStruct((1, W), F32)]
    out_specs = [row, vec]
    if gate is not None:
        out_shape.append(jax.ShapeDtypeStruct((S, W), F32))
        out_specs.append(row)
    out = pl.pallas_call(body, out_shape=out_shape, grid=(S // ts,), in_specs=in_specs, out_specs=out_specs,
                         name=name, compiler_params=_params("arbitrary"))(*args)
    return out[0], out[1], (out[2] if gate is not None else None)


def _make_gnorm(nx, has_gate, group, center, scale, tag):
    kw = dict(group=group, center=center, scale=scale)

    @jax.custom_vjp
    def op(*args):
        return fwd(*args)[0]

    def fwd(*args):
        xs, gain = args[:nx], args[nx]
        gate = args[nx + 1] if has_gate else None
        y = _gnorm_fwd(xs, gain[None, :], gate, out_dtype=F32, name=tag + "_fwd", **kw)
        return y, args

    def bwd(args, dy):
        xs, gain = args[:nx], args[nx]
        gate = args[nx + 1] if has_gate else None
        dx, dg, dgate = _gnorm_bwd(dy, xs, gain[None, :], gate, name=tag + "_bwd", **kw)
        return (dx,) * nx + (dg[0],) + ((dgate,) if has_gate else ())

    op.defvjp(fwd, bwd)
    return op


@jax.custom_vjp
def norm_matmul(x, g, w):
    return _norm_matmul_fwd(x, g, w)[0]


def _norm_matmul_fwd(x, g, w):
    xn = _gnorm_fwd([x], g[None, :], None, group=x.shape[1], center=False, scale=1.0, out_dtype=MXU, name="rms_in")
    return _mm(xn, w, name="mm_in"), (x, g, w, xn)


def _norm_matmul_bwd(saved, dh):
    x, g, w, xn = saved
    dhb = dh.astype(MXU)
    dxn = _mm(dhb, w, nt=True, name="mm_in_dx")
    dx, dg, _ = _gnorm_bwd(dxn, [x], g[None, :], None, group=x.shape[1], center=False, scale=1.0, name="rms_in_bwd")
    dw = _mm(xn.T, dhb, out_dtype=w.dtype, name="mm_in_dw")
    return dx, dg[0], dw


norm_matmul.defvjp(_norm_matmul_fwd, _norm_matmul_bwd)


@jax.custom_vjp
def out_proj(y, w, x):
    return _out_proj_fwd(y, w, x)[0]


def _out_proj_fwd(y, w, x):
    yb = y.astype(MXU)
    return _mm(yb, w, res=x, name="mm_out"), (yb, w)


def _out_proj_bwd(saved, dout):
    yb, w = saved
    db = dout.astype(MXU)
    dy = _mm(db, w, nt=True, name="mm_out_dy")
    dw = _mm(yb.T, db, out_dtype=w.dtype, name="mm_out_dw")
    return dy, dw, dout


out_proj.defvjp(_out_proj_fwd, _out_proj_bwd)


@jax.custom_vjp
def mlp(x, g, w_up, w_down):
    return _mlp_fwd(x, g, w_up, w_down)[0]


def _mlp_fwd(x, g, w_up, w_down):
    hm = _gnorm_fwd([x], g[None, :], None, group=x.shape[1], center=False, scale=1.0, out_dtype=MXU, name="rms_mlp")
    u, act = _mm(hm, w_up, emit_act=True, name="mm_up")
    return _mm(act, w_down, res=x, name="mm_down"), (x, g, w_up, w_down, hm, u, act)


def _mlp_bwd(saved, dout):
    x, g, w_up, w_down, hm, u, act = saved
    db = dout.astype(MXU)
    du = _mm(db, w_down, nt=True, u_in=u, out_dtype=MXU, name="mm_down_da")
    dw_down = _mm(act.T, db, out_dtype=w_down.dtype, name="mm_down_dw")
    dhm = _mm(du, w_up, nt=True, name="mm_up_dx")
    dw_up = _mm(hm.T, du, out_dtype=w_up.dtype, name="mm_up_dw")
    dx, dg, _ = _gnorm_bwd(dhm, [x], g[None, :], None, group=x.shape[1], center=False, scale=1.0, name="rms_mlp_bwd")
    return dout + dx, dg[0], dw_up, dw_down


mlp.defvjp(_mlp_fwd, _mlp_bwd)


def _order(C, rev):
    i = np.arange(C)
    return (C - 1 - i) if rev else i


def _hg_constants(C, rev):
    p = _order(C, rev)
    pi, pj = p[:, None], p[None, :]
    tri = (pj <= pi).astype(np.float32)
    masks = [((pi // HG_SUB) == (pj // HG_SUB)) & (pj <= pi)]
    h = HG_SUB
    halves = []
    while h < C:
        masks.append(((pi // (2 * h)) == (pj // (2 * h))) & ((pi // h) % 2 == 1) & ((pj // h) % 2 == 0))
        halves.append(h)
        h *= 2
    return tri, np.stack(masks).astype(np.float32), halves


def _hg_tables(b_scr, C, h, rev):
    nb = C // h
    g_rows, e_rows = [], []
    for rb in range(nb):
        if rev:
            e = b_scr[pl.ds(rb * h, 1), :]
            g = b_scr[pl.ds((rb + 1) * h, 1), :] if rb < nb - 1 else None
        else:
            e = b_scr[pl.ds((rb + 1) * h - 1, 1), :]
            g = b_scr[pl.ds(rb * h - 1, 1), :] if rb >= 1 else None
        e_rows.append(jnp.broadcast_to(e, (h, HG_D)))
        g_rows.append(jnp.zeros((h, HG_D), F32) if g is None else jnp.broadcast_to(g, (h, HG_D)))
    return jnp.concatenate(g_rows, axis=0), jnp.concatenate(e_rows, axis=0)


def _hg_gates(z, lb):
    sg = jax.nn.sigmoid(z)
    f = lb + (1.0 - lb) * sg
    return sg, f, 1.0 - f, jnp.log(f)


def _hg_exponents(b, b_scr, C, halves, rev):
    g0, _ = _hg_tables(b_scr, C, HG_SUB, rev)
    p0 = b - g0
    out = [(p0, jnp.minimum(-p0, EXP_CLAMP))]
    for h in halves:
        g, e = _hg_tables(b_scr, C, h, rev)
        out.append((jnp.minimum(b - g, 0.0), jnp.minimum(e - b, 0.0)))
    return out


def _hg_scan_fwd(q, v, z, lb, rev, name):
    S = q.shape[0]
    C = SCAN_C
    nc = S // C
    tri, masks, halves = _hg_constants(C, rev)
    nlev = masks.shape[0]
    end_row = 0 if rev else C - 1
    vT = v.astype(MXU).T

    def body(q_ref, v_ref, vT_ref, z_ref, lb_ref, tri_ref, mask_ref, o_ref, st_ref, s_scr, b_scr):
        @pl.when(pl.program_id(1) == 0)
        def _():
            s_scr[...] = jnp.zeros_like(s_scr)

        _, f, k, lf = _hg_gates(z_ref[...], lb_ref[...])
        b = _dot01(tri_ref[...], lf)
        b_scr[...] = b
        qv, vv = q_ref[...], v_ref[...]
        st = s_scr[...]
        st_ref[0, 0] = st
        a = jnp.zeros((C, C), F32)
        for lv, (eq, ek) in enumerate(_hg_exponents(b, b_scr, C, halves, rev)):
            a = a + mask_ref[lv] * _dot_nt(qv * jnp.exp(eq), k * jnp.exp(ek))
        bend = b_scr[pl.ds(end_row, 1), :]
        o_ref[...] = _dot(a, vv) + _dot_nt(qv * jnp.exp(b), st)
        s_scr[...] = st * jnp.exp(bend) + _dot(vT_ref[...], k * jnp.exp(bend - b))

    cidx = (lambda c: nc - 1 - c) if rev else (lambda c: c)
    blk = pl.BlockSpec((C, HG_D), lambda h, c: (cidx(c), h))
    o, states = pl.pallas_call(
        body,
        out_shape=[jax.ShapeDtypeStruct((S, HGW), F32), jax.ShapeDtypeStruct((HG_HEADS, nc, HG_D, HG_D), F32)],
        grid=(HG_HEADS, nc),
        in_specs=[blk, blk, pl.BlockSpec((HG_D, C), lambda h, c: (h, cidx(c))), blk,
                  pl.BlockSpec((1, HG_D), lambda h, c: (0, h)),
                  pl.BlockSpec((C, C), lambda h, c: (0, 0)), pl.BlockSpec((nlev, C, C), lambda h, c: (0, 0, 0))],
        out_specs=[blk, pl.BlockSpec((1, 1, HG_D, HG_D), lambda h, c: (h, cidx(c), 0, 0))],
        scratch_shapes=[pltpu.VMEM((HG_D, HG_D), F32), pltpu.VMEM((C, HG_D), F32)],
        name=name, compiler_params=_params("parallel", "arbitrary"),
    )(q, v.astype(MXU), vT, z, lb, jnp.asarray(tri, MXU), jnp.asarray(masks))
    return o, states


def _hg_scan_bwd(q, v, z, lb, states, do, rev, name):
    S = q.shape[0]
    C = SCAN_C
    nc = S // C
    tri, masks, halves = _hg_constants(C, rev)
    nlev = masks.shape[0]
    end_row = 0 if rev else C - 1
    masks_t = np.ascontiguousarray(np.transpose(masks, (0, 2, 1)))
    dob = do.astype(MXU)

    def body(q_ref, v_ref, z_ref, lb_ref, do_ref, doT_ref, st_ref, tri_ref, triT_ref, mask_ref, maskT_ref,
             dq_ref, dv_ref, dz_ref, dlb_ref, dn_scr, b_scr):
        @pl.when(pl.program_id(1) == 0)
        def _():
            dn_scr[...] = jnp.zeros_like(dn_scr)
            dlb_ref[...] = jnp.zeros_like(dlb_ref)

        lb_v = lb_ref[...]
        sg, f, k, lf = _hg_gates(z_ref[...], lb_v)
        b = _dot01(tri_ref[...], lf)
        b_scr[...] = b
        qv, vv, dov = q_ref[...], v_ref[...], do_ref[...]
        st, dn = st_ref[0, 0], dn_scr[...]
        da = _dot_nt(dov, vv)
        da_t = da.T
        a = jnp.zeros((C, C), F32)
        dq = jnp.zeros((C, HG_D), F32)
        dk = jnp.zeros((C, HG_D), F32)
        for lv, (eq, ek) in enumerate(_hg_exponents(b, b_scr, C, halves, rev)):
            xq, xk = jnp.exp(eq), jnp.exp(ek)
            qs, ks = qv * xq, k * xk
            a = a + mask_ref[lv] * _dot_nt(qs, ks)
            dq = dq + _dot(mask_ref[lv] * da, ks) * xq
            dk = dk + _dot(maskT_ref[lv] * da_t, qs) * xk
        bend = b_scr[pl.ds(end_row, 1), :]
        xb, xe, xend = jnp.exp(b), jnp.exp(bend - b), jnp.exp(bend)
        dq = dq + _dot(dov, st) * xb
        dk_state = _dot(vv, dn) * xe
        dk = dk + dk_state
        dv_ref[...] = _dot(a.T, dov) + _dot_nt(k * xe, dn)
        dn_scr[...] = dn * xend + _dot(doT_ref[...], qv * xb)
        extra = jnp.sum(k * dk_state, axis=0, keepdims=True) + xend * jnp.sum(st * dn, axis=0, keepdims=True)
        rows = lax.broadcasted_iota(jnp.int32, (C, HG_D), 0)
        db = qv * dq - k * dk + jnp.where(rows == end_row, extra, 0.0)
        df = _dot01(triT_ref[...], db) / f - dk
        dq_ref[...] = dq
        dz_ref[...] = df * (1.0 - lb_v) * sg * (1.0 - sg)
        dlb_ref[...] += jnp.sum(df * (1.0 - sg), axis=0, keepdims=True)

    cidx = (lambda c: c) if rev else (lambda c: nc - 1 - c)
    blk = pl.BlockSpec((C, HG_D), lambda h, c: (cidx(c), h))
    vec = pl.BlockSpec((1, HG_D), lambda h, c: (0, h))
    cc = pl.BlockSpec((C, C), lambda h, c: (0, 0))
    lcc = pl.BlockSpec((nlev, C, C), lambda h, c: (0, 0, 0))
    sd = jax.ShapeDtypeStruct((S, HGW), F32)
    return pl.pallas_call(
        body, out_shape=[sd, sd, sd, jax.ShapeDtypeStruct((1, HGW), F32)], grid=(HG_HEADS, nc),
        in_specs=[blk, blk, blk, vec, blk, pl.BlockSpec((HG_D, C), lambda h, c: (h, cidx(c))),
                  pl.BlockSpec((1, 1, HG_D, HG_D), lambda h, c: (h, cidx(c), 0, 0)), cc, cc, lcc, lcc],
        out_specs=[blk, blk, blk, vec],
        scratch_shapes=[pltpu.VMEM((HG_D, HG_D), F32), pltpu.VMEM((C, HG_D), F32)],
        name=name, compiler_params=_params("parallel", "arbitrary"),
    )(q, v.astype(MXU), z, lb, dob, dob.T, states, jnp.asarray(tri, MXU), jnp.asarray(tri.T, MXU),
      jnp.asarray(masks), jnp.asarray(masks_t))


def _make_hg_scan(rev):
    tag = "hg_rev" if rev else "hg_fwd"

    @jax.custom_vjp
    def op(q, v, z, lb):
        return fwd(q, v, z, lb)[0]

    def fwd(q, v, z, lb):
        o, states = _hg_scan_fwd(q, v, z, lb[None, :], rev, tag)
        return o, (q, v, z, lb, states)

    def bwd(saved, do):
        q, v, z, lb, states = saved
        dq, dv, dz, dlb = _hg_scan_bwd(q, v, z, lb[None, :], states, do, rev, tag + "_bwd")
        return dq, dv, dz, dlb[0]

    op.defvjp(fwd, bwd)
    return op


def _ret_constants(C, rev):
    hidx = np.arange(RET_HEADS, dtype=np.float64)
    lg = np.log1p(-np.exp2(-5.0 - hidx))
    if rev:
        lg = lg[::-1]
    p = _order(C, rev).astype(np.float64)
    rel = p[:, None] - p[None, :]
    dmat = np.where(rel >= 0, np.exp(lg[:, None, None] * np.maximum(rel, 0.0)), 0.0)
    xi = np.exp(lg[:, None] * (p[None, :] + 1.0))
    zeta = np.exp(lg[:, None] * (C - 1.0 - p[None, :]))
    gc = np.exp(lg * C)
    bc = lambda t: np.ascontiguousarray(np.broadcast_to(t[:, :, None], (RET_HEADS, C, RET_DK))).astype(np.float32)
    gcb = np.ascontiguousarray(np.broadcast_to(gc[:, None, None], (RET_HEADS, 1, RET_DK))).astype(np.float32)
    return dmat.astype(np.float32), bc(xi), bc(zeta), gcb


def _ret_scan_fwd(qh, kh, v, rev, name):
    S = v.shape[0]
    C = SCAN_C
    nc = S // C
    dmat, xi, zeta, gc = _ret_constants(C, rev)
    vb = v.astype(MXU)

    def body(q_ref, k_ref, v_ref, vT_ref, d_ref, xi_ref, zeta_ref, gc_ref, o_ref, st_ref, s_scr):
        @pl.when(pl.program_id(1) == 0)
        def _():
            s_scr[...] = jnp.zeros_like(s_scr)

        qv, kv = q_ref[0], k_ref[0]
        st = s_scr[...]
        st_ref[0, 0] = st
        sc = _dot_nt(qv, kv) * d_ref[0]
        o_ref[...] = _dot(sc, v_ref[...]) + _dot_nt(qv * xi_ref[0], st)
        s_scr[...] = st * gc_ref[0] + _dot(vT_ref[...], kv * zeta_ref[0])

    cidx = (lambda c: nc - 1 - c) if rev else (lambda c: c)
    hk = pl.BlockSpec((1, C, RET_DK), lambda h, c: (h, cidx(c), 0))
    vblk = pl.BlockSpec((C, RET_DV), lambda h, c: (cidx(c), h))
    tab = pl.BlockSpec((1, C, RET_DK), lambda h, c: (h, 0, 0))
    return pl.pallas_call(
        body,
        out_shape=[jax.ShapeDtypeStruct((S, RETW), F32), jax.ShapeDtypeStruct((RET_HEADS, nc, RET_DV, RET_DK), F32)],
        grid=(RET_HEADS, nc),
        in_specs=[hk, hk, vblk, pl.BlockSpec((RET_DV, C), lambda h, c: (h, cidx(c))),
                  pl.BlockSpec((1, C, C), lambda h, c: (h, 0, 0)), tab, tab,
                  pl.BlockSpec((1, 1, RET_DK), lambda h, c: (h, 0, 0))],
        out_specs=[vblk, pl.BlockSpec((1, 1, RET_DV, RET_DK), lambda h, c: (h, cidx(c), 0, 0))],
        scratch_shapes=[pltpu.VMEM((RET_DV, RET_DK), F32)],
        name=name, compiler_params=_params("parallel", "arbitrary"),
    )(qh, kh, vb, vb.T, jnp.asarray(dmat), jnp.asarray(xi), jnp.asarray(zeta), jnp.asarray(gc))


def _ret_scan_bwd(qh, kh, v, states, do, rev, name):
    S = v.shape[0]
    C = SCAN_C
    nc = S // C
    dmat, xi, zeta, gc = _ret_constants(C, rev)
    vb, dob = v.astype(MXU), do.astype(MXU)

    def body(q_ref, k_ref, v_ref, do_ref, doT_ref, st_ref, d_ref, xi_ref, zeta_ref, gc_ref,
             dq_ref, dk_ref, dv_ref, dn_scr):
        @pl.when(pl.program_id(1) == 0)
        def _():
            dn_scr[...] = jnp.zeros_like(dn_scr)

        qv, kv, vv, dov = q_ref[0], k_ref[0], v_ref[...], do_ref[...]
        st, dn = st_ref[0, 0], dn_scr[...]
        dm = d_ref[0]
        sc = _dot_nt(qv, kv) * dm
        dsc = _dot_nt(dov, vv) * dm
        kz = kv * zeta_ref[0]
        dq_ref[0] = _dot(dsc, kv) + _dot(dov, st) * xi_ref[0]
        dk_ref[0] = _dot(dsc.T, qv) + _dot(vv, dn) * zeta_ref[0]
        dv_ref[...] = _dot(sc.T, dov) + _dot_nt(kz, dn)
        dn_scr[...] = dn * gc_ref[0] + _dot(doT_ref[...], qv * xi_ref[0])

    cidx = (lambda c: c) if rev else (lambda c: nc - 1 - c)
    hk = pl.BlockSpec((1, C, RET_DK), lambda h, c: (h, cidx(c), 0))
    vblk = pl.BlockSpec((C, RET_DV), lambda h, c: (cidx(c), h))
    tab = pl.BlockSpec((1, C, RET_DK), lambda h, c: (h, 0, 0))
    hs = jax.ShapeDtypeStruct(qh.shape, F32)
    return pl.pallas_call(
        body, out_shape=[hs, hs, jax.ShapeDtypeStruct((S, RETW), F32)], grid=(RET_HEADS, nc),
        in_specs=[hk, hk, vblk, vblk, pl.BlockSpec((RET_DV, C), lambda h, c: (h, cidx(c))),
                  pl.BlockSpec((1, 1, RET_DV, RET_DK), lambda h, c: (h, cidx(c), 0, 0)),
                  pl.BlockSpec((1, C, C), lambda h, c: (h, 0, 0)), tab, tab,
                  pl.BlockSpec((1, 1, RET_DK), lambda h, c: (h, 0, 0))],
        out_specs=[hk, hk, vblk],
        scratch_shapes=[pltpu.VMEM((RET_DV, RET_DK), F32)],
        name=name, compiler_params=_params("parallel", "arbitrary"),
    )(qh, kh, vb, dob, dob.T, states, jnp.asarray(dmat), jnp.asarray(xi), jnp.asarray(zeta), jnp.asarray(gc))


def _make_ret_scan(rev):
    tag = "ret_rev" if rev else "ret_fwd"

    @jax.custom_vjp
    def op(qh, kh, v):
        return fwd(qh, kh, v)[0]

    def fwd(qh, kh, v):
        o, states = _ret_scan_fwd(qh, kh, v, rev, tag)
        return o, (qh, kh, v, states)

    def bwd(saved, do):
        qh, kh, v, states = saved
        return tuple(_ret_scan_bwd(qh, kh, v, states, do, rev, tag + "_bwd"))

    op.defvjp(fwd, bwd)
    return op


def _rope_tables(S, scale):
    half = RET_DK // 2
    inv = ROPE_BASE ** (-np.arange(half, dtype=np.float32) / half)
    ang = np.arange(S, dtype=np.float32)[:, None] * inv[None, :]
    cos, sin = np.cos(ang), np.sin(ang)
    cos_t = np.tile(np.concatenate([cos, cos], axis=1), (1, RET_HEADS)) * scale
    sin_t = np.tile(np.concatenate([-sin, sin], axis=1), (1, RET_HEADS)) * scale
    return cos_t.astype(np.float32), sin_t.astype(np.float32)


def _rope_apply(t, cos_t, sin_t, name):
    S, W = t.shape
    ts = _pick(S, (512, 256, 128))
    half = RET_DK // 2

    def body(t_ref, c_ref, s_ref, o_ref):
        tv = t_ref[...]
        lane = lax.broadcasted_iota(jnp.int32, tv.shape, 1)
        partner = jnp.where(lane % RET_DK < half, pltpu.roll(tv, W - half, 1), pltpu.roll(tv, half, 1))
        o_ref[...] = tv * c_ref[...] + partner * s_ref[...]

    row = pl.BlockSpec((ts, W), lambda i: (i, 0))
    return pl.pallas_call(body, out_shape=jax.ShapeDtypeStruct((S, W), F32), grid=(S // ts,),
                          in_specs=[row, row, row], out_specs=row, name=name,
                          compiler_params=_params("parallel"))(t, jnp.asarray(cos_t), jnp.asarray(sin_t))


def _make_rope(scale, tag):
    def apply(t):
        cos_t, sin_t = _rope_tables(t.shape[0], scale)
        return _rope_apply(t, cos_t, sin_t, tag)

    op = jax.custom_vjp(apply)

    def fwd(t):
        return apply(t), None

    def bwd(_, dout):
        cos_t, sin_t = _rope_tables(dout.shape[0], scale)
        return (_rope_apply(dout, cos_t, -sin_t, tag + "_bwd"),)

    op.defvjp(fwd, bwd)
    return op


def _att_geometry(L):
    tq = _pick(L, (512, 256, 128))
    return tq, L // tq, tq // ATT_HALO


def _att_specs(tq, per):
    main = pl.BlockSpec((1, tq, DIL_HD), lambda b, n: (b, n, 0))
    prev = pl.BlockSpec((1, ATT_HALO, DIL_HD), lambda b, n: (b, jnp.maximum(n * per - 1, 0), 0))
    return main, prev


def _att_valid(n, u, tq, L):
    ii = lax.broadcasted_iota(jnp.int32, (ATT_SB, ATT_WIN), 0)
    jj = lax.broadcasted_iota(jnp.int32, (ATT_SB, ATT_WIN), 1)
    key = n * tq + u * ATT_SB - ATT_HALO + jj
    return (jnp.abs(jj - ATT_HALO - ii) <= ATT_HALO) & (key >= 0) & (key < L)


def _att_fill(buf, prev_ref, main_ref, next_ref, tq):
    buf[pl.ds(0, ATT_HALO), :] = prev_ref[0]
    buf[pl.ds(ATT_HALO, tq), :] = main_ref[0]
    buf[pl.ds(ATT_HALO + tq, ATT_HALO), :] = next_ref[0]


def _att_fwd(q, k, v, bias, dil, name):
    B, L, _ = q.shape
    tq, nt, per = _att_geometry(L)
    last = L // ATT_HALO - 1

    def body(q_ref, kp_ref, k_ref, kn_ref, vp_ref, v_ref, vn_ref, bias_ref, o_ref, lse_ref, kbuf, vbuf):
        n = pl.program_id(1)
        _att_fill(kbuf, kp_ref, k_ref, kn_ref, tq)
        _att_fill(vbuf, vp_ref, v_ref, vn_ref, tq)
        for u in range(tq // ATT_SB):
            rows = pl.ds(u * ATT_SB, ATT_SB)
            win = pl.ds(u * ATT_SB, ATT_WIN)
            s = _dot_nt(q_ref[0, rows, :], kbuf[win, :]) + bias_ref[0]
            s = jnp.where(_att_valid(n, u, tq, L), s, -1e30)
            m = jnp.max(s, axis=-1, keepdims=True)
            p = jnp.exp(s - m)
            den = jnp.sum(p, axis=-1, keepdims=True)
            o_ref[0, rows, :] = _dot(p, vbuf[win, :]) / den
            lse_ref[0, rows, :] = jnp.broadcast_to(m + jnp.log(den), (ATT_SB, DIL_HD))

    main, prev = _att_specs(tq, per)
    nxt = pl.BlockSpec((1, ATT_HALO, DIL_HD), lambda b, n: (b, jnp.minimum((n + 1) * per, last), 0))
    sd = jax.ShapeDtypeStruct((B, L, DIL_HD), F32)
    return pl.pallas_call(
        body, out_shape=[sd, sd], grid=(B, nt),
        in_specs=[main, prev, main, nxt, prev, main, nxt,
                  pl.BlockSpec((1, ATT_SB, ATT_WIN), lambda b, n: (b // dil, 0, 0))],
        out_specs=[main, main],
        scratch_shapes=[pltpu.VMEM((tq + 2 * ATT_HALO, DIL_HD), q.dtype), pltpu.VMEM((tq + 2 * ATT_HALO, DIL_HD), q.dtype)],
        name=name, compiler_params=_params("parallel", "arbitrary"),
    )(q, k, k, k, v, v, v, bias)


def _att_bwd(q, k, v, bias, o, lse, do, dlse, dil, name):
    B, L, _ = q.shape
    tq, nt, per = _att_geometry(L)
    last = L // ATT_HALO - 1

    def body(q_ref, kp_ref, k_ref, kn_ref, vp_ref, v_ref, vn_ref, bias_ref, o_ref, lse_ref, do_ref, dlse_ref,
             dq_ref, dk_ref, dkp_ref, dkn_ref, dv_ref, dvp_ref, dvn_ref, dbias_ref, kbuf, vbuf, dkbuf, dvbuf):
        b, n = pl.program_id(0), pl.program_id(1)

        @pl.when((b % dil == 0) & (n == 0))
        def _():
            dbias_ref[...] = jnp.zeros_like(dbias_ref)

        _att_fill(kbuf, kp_ref, k_ref, kn_ref, tq)
        _att_fill(vbuf, vp_ref, v_ref, vn_ref, tq)
        dkbuf[...] = jnp.zeros_like(dkbuf)
        dvbuf[...] = jnp.zeros_like(dvbuf)
        for u in range(tq // ATT_SB):
            rows = pl.ds(u * ATT_SB, ATT_SB)
            win = pl.ds(u * ATT_SB, ATT_WIN)
            qu, kw, vw = q_ref[0, rows, :], kbuf[win, :], vbuf[win, :]
            dou = do_ref[0, rows, :]
            s = _dot_nt(qu, kw) + bias_ref[0]
            lse_u = jnp.max(lse_ref[0, rows, :], axis=-1, keepdims=True)
            p = jnp.where(_att_valid(n, u, tq, L), jnp.exp(s - lse_u), 0.0)
            corr = jnp.sum(dlse_ref[0, rows, :] - dou * o_ref[0, rows, :], axis=-1, keepdims=True)
            ds = p * (_dot_nt(dou, vw) + corr)
            dq_ref[0, rows, :] = _dot(ds, kw)
            dkbuf[win, :] += _dot(ds.T, qu)
            dvbuf[win, :] += _dot(p.T, dou)
            dbias_ref[0] += ds
        for full, lo, hi in ((dkbuf, dkp_ref, dkn_ref), (dvbuf, dvp_ref, dvn_ref)):
            lo[0, 0] = full[pl.ds(0, ATT_HALO), :]
            hi[0, 0] = full[pl.ds(ATT_HALO + tq, ATT_HALO), :]
        dk_ref[0] = dkbuf[pl.ds(ATT_HALO, tq), :]
        dv_ref[0] = dvbuf[pl.ds(ATT_HALO, tq), :]

    main, prev = _att_specs(tq, per)
    nxt = pl.BlockSpec((1, ATT_HALO, DIL_HD), lambda b, n: (b, jnp.minimum((n + 1) * per, last), 0))
    halo = pl.BlockSpec((1, 1, ATT_HALO, DIL_HD), lambda b, n: (b, n, 0, 0))
    bias_spec = pl.BlockSpec((1, ATT_SB, ATT_WIN), lambda b, n: (b // dil, 0, 0))
    sd = jax.ShapeDtypeStruct((B, L, DIL_HD), F32)
    hd = jax.ShapeDtypeStruct((B, nt, ATT_HALO, DIL_HD), F32)
    width = tq + 2 * ATT_HALO
    dq, dk, dkp, dkn, dv, dvp, dvn, dbias = pl.pallas_call(
        body, out_shape=[sd, sd, hd, hd, sd, hd, hd, jax.ShapeDtypeStruct(bias.shape, F32)], grid=(B, nt),
        in_specs=[main, prev, main, nxt, prev, main, nxt, bias_spec, main, main, main, main],
        out_specs=[main, main, halo, halo, main, halo, halo, bias_spec],
        scratch_shapes=[pltpu.VMEM((width, DIL_HD), q.dtype), pltpu.VMEM((width, DIL_HD), q.dtype),
                        pltpu.VMEM((width, DIL_HD), F32), pltpu.VMEM((width, DIL_HD), F32)],
        name=name, compiler_params=_params("arbitrary", "arbitrary"),
    )(q, k, k, k, v, v, v, bias, o, lse, do, dlse)

    def fold(mainv, lo, hi):
        t = mainv.reshape(B, nt, tq, DIL_HD)
        if nt > 1:
            t = t.at[:, :-1, tq - ATT_HALO:, :].add(lo[:, 1:])
            t = t.at[:, 1:, :ATT_HALO, :].add(hi[:, :-1])
        return t.reshape(B, L, DIL_HD)

    return dq, fold(dk, dkp, dkn), fold(dv, dvp, dvn), dbias


def _make_attention(dil):
    tag = "att_d%d" % dil

    @jax.custom_vjp
    def op(q, k, v, bias):
        return fwd(q, k, v, bias)[0]

    def fwd(q, k, v, bias):
        qb, kb, vb = q.astype(MXU), k.astype(MXU), v.astype(MXU)
        o, lse = _att_fwd(qb, kb, vb, bias, dil, tag)
        return (o, lse), (qb, kb, vb, bias, o, lse)

    def bwd(saved, cts):
        qb, kb, vb, bias, o, lse = saved
        do, dlse = cts
        return tuple(_att_bwd(qb, kb, vb, bias, o, lse, do, dlse, dil, tag + "_bwd"))

    op.defvjp(fwd, bwd)
    return op


def _merge_weights(l0, l1, l2):
    m = jnp.maximum(jnp.maximum(l0, l1), l2)
    e0, e1, e2 = jnp.exp(l0 - m), jnp.exp(l1 - m), jnp.exp(l2 - m)
    inv = 1.0 / (e0 + e1 + e2)
    return e0 * inv, e1 * inv, e2 * inv


def _merge_call(body, n_in, n_out, shape, name):
    R, W = shape
    ts = _pick(R, (1024, 512, 256, 128))
    row = pl.BlockSpec((ts, W), lambda i: (i, 0))
    sd = jax.ShapeDtypeStruct(shape, F32)
    return pl.pallas_call(body, out_shape=[sd] * n_out, grid=(R // ts,), in_specs=[row] * n_in,
                          out_specs=[row] * n_out, name=name, compiler_params=_params("parallel"))


@jax.custom_vjp
def dil_merge(o0, o1, o2, l0, l1, l2):
    return _dil_merge_fwd(o0, o1, o2, l0, l1, l2)[0]


def _dil_merge_fwd(*args):
    def body(o0, o1, o2, l0, l1, l2, out):
        w0, w1, w2 = _merge_weights(l0[...], l1[...], l2[...])
        out[...] = w0 * o0[...] + w1 * o1[...] + w2 * o2[...]

    return _merge_call(body, 6, 1, args[0].shape, "dil_merge")(*args)[0], args


def _dil_merge_bwd(args, dout):
    def body(o0, o1, o2, l0, l1, l2, d, do0, do1, do2, dl0, dl1, dl2):
        ws = _merge_weights(l0[...], l1[...], l2[...])
        dv = d[...]
        dws = [dv * o[...] for o in (o0, o1, o2)]
        mean = ws[0] * dws[0] + ws[1] * dws[1] + ws[2] * dws[2]
        for w, dw, do_ref, dl_ref in zip(ws, dws, (do0, do1, do2), (dl0, dl1, dl2)):
            do_ref[...] = w * dv
            dl_ref[...] = w * (dw - mean)

    return tuple(_merge_call(body, 7, 6, args[0].shape, "dil_merge_bwd")(*args, dout))


dil_merge.defvjp(_dil_merge_fwd, _dil_merge_bwd)


def _t5_bucket(rel):
    nb = REL_BUCKETS // 2
    max_exact = nb // 2
    sign_off = np.where(rel > 0, nb, 0)
    n = np.abs(rel)
    nf = np.maximum(n, 1).astype(np.float32)
    large = max_exact + (np.log(nf / np.float32(max_exact)) / np.float32(math.log(REL_MAX_DIST / max_exact))
                         * np.float32(nb - max_exact)).astype(np.int32)
    large = np.minimum(large, nb - 1)
    return sign_off + np.where(n < max_exact, n, large)


def _loss_grad(xf, target):
    S, D = xf.shape
    ts = _pick(S, (512, 256, 128))

    def body(x_ref, t_ref, dy_ref, part_ref):
        @pl.when(pl.program_id(0) == 0)
        def _():
            part_ref[...] = jnp.zeros_like(part_ref)

        err = x_ref[...] - t_ref[...]
        dy_ref[...] = err * (1.0 / D)
        part_ref[...] += jnp.sum(err * err, axis=0, keepdims=True)

    row = pl.BlockSpec((ts, D), lambda i: (i, 0))
    return pl.pallas_call(body, out_shape=[jax.ShapeDtypeStruct((S, D), F32), jax.ShapeDtypeStruct((1, D), F32)],
                          grid=(S // ts,), in_specs=[row, row], out_specs=[row, pl.BlockSpec((1, D), lambda i: (0, 0))],
                          name="loss_grad", compiler_params=_params("arbitrary"))(xf, target)


def _adamw_math(g, w, m, v):
    m = ADAM_B1 * m + (1.0 - ADAM_B1) * g
    v = ADAM_B2 * v + (1.0 - ADAM_B2) * (g * g)
    m_hat = m / (1.0 - ADAM_B1 ** ADAM_STEP)
    v_hat = v / (1.0 - ADAM_B2 ** ADAM_STEP)
    delta = -ADAM_LR * (m_hat / (jnp.sqrt(v_hat) + ADAM_EPS) + ADAM_WD * w)
    return delta, m, v


def _adamw(parts, w, m, v, name):
    Lw, R, C = w.shape
    tr = _pick(R, (128, 64, 32, 16, 8))

    def body(p_ref, w_ref, m_ref, v_ref, g_ref, d_ref, nm_ref, nv_ref):
        g = p_ref[0, 0].astype(F32)
        for s in range(1, N_DEV):
            g = g + p_ref[s, 0].astype(F32)
        d, nm, nv = _adamw_math(g, w_ref[0], m_ref[0], v_ref[0])
        g_ref[0], d_ref[0], nm_ref[0], nv_ref[0] = g, d, nm, nv

    blk = pl.BlockSpec((1, tr, C), lambda l, i: (l, i, 0))
    sd = jax.ShapeDtypeStruct(w.shape, F32)
    return pl.pallas_call(body, out_shape=[sd] * 4, grid=(Lw, R // tr),
                          in_specs=[pl.BlockSpec((N_DEV, 1, tr, C), lambda l, i: (0, l, i, 0)), blk, blk, blk],
                          out_specs=[blk] * 4, name=name, compiler_params=_params("parallel", "parallel"))(parts, w, m, v)


def _mesh_pos():
    return lax.axis_index("x"), lax.axis_index("y"), lax.axis_index("c")


def _slot(x, y, c):
    return 4 * x + 2 * y + c


def _all_gather(blocks, name):
    na = len(blocks)

    def body(*refs):
        ins, outs = refs[:na], refs[na:2 * na]
        send_sems, recv_sems, local_sems = refs[2 * na:]
        x, y, c = _mesh_pos()
        me, sibling = (x, y, c), (x, y, 1 - c)
        chips = [(1 - x, y), (x, 1 - y), (1 - x, 1 - y)]
        waits = []
        for a in range(na):
            out = outs[a]

            def copy(k, block, to, src=None, a=a, out=out):
                dst = out.at[_slot(*block)]
                return pltpu.make_async_remote_copy(
                    src_ref=dst if src is None else src, dst_ref=dst,
                    send_sem=send_sems.at[7 * a + k], recv_sem=recv_sems.at[7 * a + k],
                    device_id=to, device_id_type=MESH_ID)

            mine = pltpu.make_async_copy(ins[a], out.at[_slot(*me)], local_sems.at[a])
            mine.start()
            first = [copy(0, me, sibling, src=ins[a])]
            first += [copy(1 + j, me, (*chip, c), src=ins[a]) for j, chip in enumerate(chips)]
            for cp in first:
                cp.start()
            waits.append((copy, mine, first))
        sends = []
        for a in range(na):
            copy, mine, first = waits[a]
            passed = [copy(4 + j, (*chip, c), sibling) for j, chip in enumerate(chips)]
            for j, chip in enumerate(chips):
                copy(1 + j, (*chip, c), me).wait_recv()
                passed[j].start()
            sends.append(first + passed)
        for a in range(na):
            copy, mine, _ = waits[a]
            copy(0, sibling, me).wait_recv()
            for j, chip in enumerate(chips):
                copy(4 + j, (*chip, 1 - c), me).wait_recv()
            for cp in sends[a]:
                cp.wait_send()
            mine.wait()

    anyspec = pl.BlockSpec(memory_space=pl.ANY)
    return pl.pallas_call(
        body, out_shape=[jax.ShapeDtypeStruct((N_DEV,) + b.shape, b.dtype) for b in blocks],
        in_specs=[anyspec] * na, out_specs=[anyspec] * na,
        scratch_shapes=[pltpu.SemaphoreType.DMA((7 * na,)), pltpu.SemaphoreType.DMA((7 * na,)),
                        pltpu.SemaphoreType.DMA((na,))],
        name=name)(*blocks)


def _exchange(fulls, name):
    na = len(fulls)

    def body(*refs):
        ins, outs = refs[:na], refs[na:2 * na]
        send_sems, recv_sems, local_sems = refs[2 * na:]
        x, y, c = _mesh_pos()
        my_slot = _slot(x, y, c)
        copies = []
        for a in range(na):
            mine = pltpu.make_async_copy(ins[a].at[my_slot], outs[a].at[my_slot], local_sems.at[a])
            mine.start()
            copies.append(mine)
        remote = []
        for a in range(na):
            for k in range(1, N_DEV):
                px = 1 - x if k & 4 else x
                py = 1 - y if k & 2 else y
                pc = 1 - c if k & 1 else c
                cp = pltpu.make_async_remote_copy(
                    src_ref=ins[a].at[_slot(px, py, pc)], dst_ref=outs[a].at[my_slot],
                    send_sem=send_sems.at[7 * a + k - 1], recv_sem=recv_sems.at[7 * a + k - 1],
                    device_id=(px, py, pc), device_id_type=MESH_ID)
                cp.start()
                remote.append(cp)
        for cp in remote:
            cp.wait()
        for cp in copies:
            cp.wait()

    anyspec = pl.BlockSpec(memory_space=pl.ANY)
    return pl.pallas_call(
        body, out_shape=[jax.ShapeDtypeStruct(f.shape, f.dtype) for f in fulls],
        in_specs=[anyspec] * na, out_specs=[anyspec] * na,
        scratch_shapes=[pltpu.SemaphoreType.DMA((7 * na,)), pltpu.SemaphoreType.DMA((7 * na,)),
                        pltpu.SemaphoreType.DMA((na,))],
        name=name)(*fulls)


_hg_fwd_op, _hg_rev_op = _make_hg_scan(False), _make_hg_scan(True)
_ret_fwd_op, _ret_rev_op = _make_ret_scan(False), _make_ret_scan(True)
_rope_q, _rope_k = _make_rope(1.0, "rope_q"), _make_rope(RET_DK ** -0.5, "rope_k")
_hg_post = _make_gnorm(2, True, HG_D, False, 1.0, "hg_post")
_ret_post = _make_gnorm(2, True, RET_DV, True, 1.0, "ret_post")
_q_norm = _make_gnorm(1, False, DIL_HD, False, DIL_HD ** -0.5, "dil_qnorm")
_k_norm = _make_gnorm(1, False, DIL_HD, False, 1.0, "dil_knorm")
_att_ops = {dil: _make_attention(dil) for _, dil in DIL_GROUPS}


def _to_heads(t, d):
    S, W = t.shape
    return t.reshape(S, W // d, d).transpose(1, 0, 2)


def _dilated_mixer(parts, rel_bias, q_gain, k_gain):
    S = parts[0].shape[0]
    qg, kg = jnp.tile(q_gain, DIL_SLOTS), jnp.tile(k_gain, DIL_SLOTS)
    ii = np.arange(ATT_SB)[:, None]
    jj = np.arange(ATT_WIN)[None, :]
    outs, lses = [], []
    for g, (window, dil) in enumerate(DIL_GROUPS):
        assert window // (2 * dil) == ATT_HALO
        L = S // dil

        def to_res(t):
            return t.reshape(L, dil, DIL_SLOTS, DIL_HD).transpose(2, 1, 0, 3).reshape(DIL_SLOTS * dil, L, DIL_HD)

        def from_res(t):
            return t.reshape(DIL_SLOTS, dil, L, DIL_HD).transpose(0, 2, 1, 3).reshape(DIL_SLOTS * S, DIL_HD)

        bucket = _t5_bucket((jj - ATT_HALO - ii) * dil)
        bias = rel_bias[:, g * DIL_SLOTS:(g + 1) * DIL_SLOTS][bucket].transpose(2, 0, 1)
        q = _q_norm(parts[3 * g], qg)
        k = _k_norm(parts[3 * g + 1], kg)
        o, lse = _att_ops[dil](to_res(q), to_res(k), to_res(parts[3 * g + 2]), bias)
        outs.append(from_res(o))
        lses.append(from_res(lse))
    merged = dil_merge(*outs, *lses)
    return merged.reshape(DIL_SLOTS, S, DIL_HD).transpose(1, 0, 2).reshape(S, DILW)


def _layer(x, w_in, w_out, w_up, w_down, p):
    h = norm_matmul(x, p["norm_mix"], w_in)
    offs = np.cumsum(IN_SPLITS)[:-1].tolist()
    parts = jnp.split(h, offs, axis=-1)
    q, v = parts[0], parts[1]
    y_a = _hg_post(_hg_fwd_op(q, v, parts[2], p["lb_fwd"]), _hg_rev_op(q, v, parts[3], p["lb_bwd"]), p["hg_norm"], parts[4])
    qh = _to_heads(_rope_q(parts[5]), RET_DK)
    kh = _to_heads(_rope_k(parts[6]), RET_DK)
    y_b = _ret_post(_ret_fwd_op(qh, kh, parts[7]), _ret_rev_op(qh, kh, parts[7]), p["ret_norm"], parts[8])
    y_c = _dilated_mixer(parts[9:], p["rel_bias"], p["q_norm"], p["k_norm"])
    y = jnp.concatenate([y_a, y_b, y_c], axis=-1)
    x = out_proj(y, w_out, x)
    return mlp(x, p["norm_mlp"], w_up, w_down)


def _forward(x, mats, small):
    depth = len(mats)
    lb_f = jnp.cumsum(jax.nn.softmax(small["hg_lb_fwd"], axis=0), axis=0)
    lb_b = jnp.cumsum(jax.nn.softmax(small["hg_lb_bwd"], axis=0), axis=0)
    for l in range(depth):
        p = {k: small[k][l] for k in ("norm_mix", "norm_mlp", "hg_norm", "ret_norm", "q_norm", "k_norm")}
        p["lb_fwd"], p["lb_bwd"] = lb_f[l] - lb_f[0], lb_b[l] - lb_b[0]
        p["rel_bias"] = small["rel_bias"]
        x = _layer(x, *mats[l], p)
    return x


SMALL_NAMES = ("norm_mix", "norm_mlp", "hg_lb_fwd", "hg_lb_bwd", "hg_norm", "ret_norm", "q_norm", "k_norm", "rel_bias")


def _col_full(blocks):
    return blocks.transpose(1, 0, 2).reshape(blocks.shape[1], -1)


def _col_blocks(full):
    K = full.shape[0]
    return full.reshape(K, N_DEV, -1).transpose(1, 0, 2)


def _row_full(blocks):
    return blocks.reshape(-1, blocks.shape[2])


def _row_blocks(full):
    return full.reshape(N_DEV, -1, full.shape[1])


def kernel(x, w_in, w_out, w_up, w_down, norm_mix, norm_mlp, hg_lb_fwd, hg_lb_bwd, hg_norm, ret_norm, q_norm, k_norm, rel_bias, loss_target, m_w_in, m_w_out, m_w_up, m_w_down, m_norm_mix, m_norm_mlp, m_hg_lb_fwd, m_hg_lb_bwd, m_hg_norm, m_ret_norm, m_q_norm, m_k_norm, m_rel_bias, v_w_in, v_w_out, v_w_up, v_w_down, v_norm_mix, v_norm_mlp, v_hg_lb_fwd, v_hg_lb_bwd, v_hg_norm, v_ret_norm, v_q_norm, v_k_norm, v_rel_bias):
    depth = w_in.shape[0]
    big = (w_in, w_out, w_up, w_down)
    big_m = (m_w_in, m_w_out, m_w_up, m_w_down)
    big_v = (v_w_in, v_w_out, v_w_up, v_w_down)
    small = dict(zip(SMALL_NAMES, (norm_mix, norm_mlp, hg_lb_fwd, hg_lb_bwd, hg_norm, ret_norm, q_norm, k_norm, rel_bias)))
    small_m = (m_norm_mix, m_norm_mlp, m_hg_lb_fwd, m_hg_lb_bwd, m_hg_norm, m_ret_norm, m_q_norm, m_k_norm, m_rel_bias)
    small_v = (v_norm_mix, v_norm_mlp, v_hg_lb_fwd, v_hg_lb_bwd, v_hg_norm, v_ret_norm, v_q_norm, v_k_norm, v_rel_bias)

    g_in, g_out, g_up, g_down = _all_gather([w.astype(WIRE) for w in big], "gather_weights")
    full_of = (_col_full, _row_full, _col_full, _row_full)
    blocks_of = (_col_blocks, _row_blocks, _col_blocks, _row_blocks)
    mats = [tuple(f(g[:, l]) for f, g in zip(full_of, (g_in, g_out, g_up, g_down))) for l in range(depth)]

    xf, vjp = jax.vjp(_forward, x[0], mats, small)
    dy, part = _loss_grad(xf, loss_target[0])
    loss = lax.psum(0.5 / xf.shape[1] * jnp.sum(part), ("x", "y", "c"))
    dx, dmats, dsmall = vjp(dy)

    sends = [jnp.stack([blocks_of[i](dmats[l][i]) for l in range(depth)], axis=1) for i in range(4)]
    landed = _exchange(sends, "exchange_grads")
    big_out = [_adamw(landed[i], big[i], big_m[i], big_v[i], "adamw_%d" % i) for i in range(4)]

    flat = jnp.concatenate([dsmall[n].reshape(-1) for n in SMALL_NAMES])
    n_small = flat.shape[0]
    rows = -(-n_small // 1024) * 8
    pad = lambda t: jnp.pad(t, (0, rows * 128 - n_small)).reshape(1, rows, 128)
    (small_parts,) = _all_gather([pad(flat)[0]], "gather_small_grads")
    cat = lambda ts: pad(jnp.concatenate([t.reshape(-1) for t in ts]))
    small_out = _adamw(small_parts[:, None], cat([small[n] for n in SMALL_NAMES]), cat(small_m), cat(small_v), "adamw_small")

    def unpack(t):
        t = t.reshape(-1)
        out, off = [], 0
        for n in SMALL_NAMES:
            size = small[n].size
            out.append(t[off:off + size].reshape(small[n].shape))
            off += size
        return out

    res = [loss, dx[None]]
    for kind in range(4):
        res += [o[kind] for o in big_out] + unpack(small_out[kind])
    return tuple(res)
```

```python
import functools
import math

import numpy as np
import jax
import jax.numpy as jnp
from jax import lax
from jax.experimental import pallas as pl
from jax.experimental.pallas import tpu as pltpu

F32 = jnp.float32
MXU = jnp.bfloat16
WIRE = jnp.bfloat16
EPS = 1e-6
N_DEV = 8
VMEM_LIMIT = 48 * 1024 * 1024

HG_HEADS, HG_D = 6, 128
RET_HEADS, RET_DK, RET_DV = 6, 64, 128
DIL_SLOTS, DIL_HD = 4, 128
DIL_GROUPS = ((128, 1), (512, 4), (2048, 16))
HGW = HG_HEADS * HG_D
RETW = RET_HEADS * RET_DV
DILW = DIL_SLOTS * DIL_HD
IN_SPLITS = (HGW, HGW, HGW, HGW, HGW, RET_HEADS * RET_DK, RET_HEADS * RET_DK, RETW, RETW) + (DILW,) * 9
REL_BUCKETS, REL_MAX_DIST = 32, 1024
ROPE_BASE = 10000.0
ADAM_LR, ADAM_B1, ADAM_B2, ADAM_EPS, ADAM_WD, ADAM_STEP = 0.001, 0.9, 0.999, 1e-08, 0.01, 10

SCAN_C = 128
SCAN_HPS = 6
HG_SUB = 16
EXP_CLAMP = 60.0
ATT_SB, ATT_HALO = 128, 64
ATT_WIN = ATT_SB + 2 * ATT_HALO
MESH_ID = pl.DeviceIdType.MESH


def _pick(n, prefs):
    for p in prefs:
        if n % p == 0:
            return p
    return n


def _params(*sem):
    return pltpu.CompilerParams(dimension_semantics=sem, vmem_limit_bytes=VMEM_LIMIT)


def _dot(a, b):
    return lax.dot_general(a.astype(MXU), b.astype(MXU), (((1,), (0,)), ((), ())), preferred_element_type=F32)


def _dot_nt(a, b):
    return lax.dot_general(a.astype(MXU), b.astype(MXU), (((1,), (1,)), ((), ())), preferred_element_type=F32)


def _dot01(sel, x):
    if MXU == F32:
        return _dot(sel, x)
    hi = x.astype(MXU)
    r1 = x - hi.astype(F32)
    mid = r1.astype(MXU)
    lo = (r1 - mid.astype(F32)).astype(MXU)
    return _dot(sel, hi) + _dot(sel, mid) + _dot(sel, lo)


def _mm(a, b, *, nt=False, out_dtype=F32, res=None, u_in=None, emit_act=False, rider=None, name):
    M, K = a.shape
    N = b.shape[0] if nt else b.shape[1]
    tm = _pick(M, (1024, 512, 256, 128))
    tn = _pick(N, (1024, 768, 512, 384, 256, 128))
    tk = _pick(K, (2048, 1536, 1024, 512, 256, 128))
    ni, nj, nk = M // tm, N // tn, K // tk
    n_ride = len(rider.arrays) if rider is not None else 0

    def body(*refs):
        it = iter(refs)
        a_ref, b_ref = next(it), next(it)
        res_ref = next(it) if res is not None else None
        u_ref = next(it) if u_in is not None else None
        ride_in = [next(it) for _ in range(n_ride)]
        o_ref = next(it)
        act_ref = next(it) if emit_act else None
        ride_out = [next(it) for _ in range(n_ride)]
        acc_ref = next(it)
        sems = list(it)
        i, j, k = pl.program_id(0), pl.program_id(1), pl.program_id(2)

        step = (i * nj + j) * nk + k
        if rider is not None:
            pl.when(step == 0)(functools.partial(rider.start, ride_in, ride_out, sems))
            pl.when(step == (ni // 2) * nj * nk)(functools.partial(rider.mid, ride_in, ride_out, sems))

        @pl.when(k == 0)
        def _():
            acc_ref[...] = jnp.zeros_like(acc_ref)

        acc_ref[...] += (_dot_nt if nt else _dot)(a_ref[...], b_ref[...])

        @pl.when(k == nk - 1)
        def _():
            r = acc_ref[...]
            if res_ref is not None:
                r = r + res_ref[...]
            if u_ref is not None:
                r = r * (2.0 * jnp.maximum(u_ref[...], 0.0))
            o_ref[...] = r.astype(o_ref.dtype)
            if act_ref is not None:
                t = jnp.maximum(r, 0.0)
                act_ref[...] = (t * t).astype(act_ref.dtype)

        if rider is not None:
            pl.when(step == ni * nj * nk - 1)(functools.partial(rider.finish, ride_in, ride_out, sems))

    mn = pl.BlockSpec((tm, tn), lambda i, j, k: (i, j))
    anyspec = pl.BlockSpec(memory_space=pl.ANY)
    in_specs = [pl.BlockSpec((tm, tk), lambda i, j, k: (i, k)),
                pl.BlockSpec((tn, tk), lambda i, j, k: (j, k)) if nt else pl.BlockSpec((tk, tn), lambda i, j, k: (k, j))]
    args = [a, b]
    for extra in (res, u_in):
        if extra is not None:
            in_specs.append(mn)
            args.append(extra)
    out_shape = [jax.ShapeDtypeStruct((M, N), out_dtype)]
    out_specs = [mn]
    if emit_act:
        out_shape.append(jax.ShapeDtypeStruct((M, N), MXU))
        out_specs.append(mn)
    scratch = [pltpu.VMEM((tm, tn), F32)]
    if rider is not None:
        in_specs += [anyspec] * n_ride
        args += list(rider.arrays)
        out_shape += rider.out_shapes()
        out_specs += [anyspec] * n_ride
        scratch += rider.sem_shapes()
    sem = ("arbitrary",) * 3 if rider is not None else ("parallel", "parallel", "arbitrary")
    out = pl.pallas_call(
        body, out_shape=out_shape, grid=(ni, nj, nk), in_specs=in_specs, out_specs=out_specs,
        scratch_shapes=scratch, name=name, compiler_params=_params(*sem))(*args)
    return out if (emit_act or rider is not None) else out[0]


def _gnorm_stats(x, center):
    if center:
        x = x - jnp.mean(x, axis=-1, keepdims=True)
    r = lax.rsqrt(jnp.mean(x * x, axis=-1, keepdims=True) + EPS)
    return x * r, r


def _silu_parts(gt):
    sg = jax.nn.sigmoid(gt)
    return gt * sg, sg * (1.0 + gt * (1.0 - sg))


def _gnorm_fwd(xs, gain, gate, *, group, center, scale, out_dtype, name):
    S, W = xs[0].shape
    ts = _pick(S, (512, 256, 128))
    nx = len(xs)

    def body(*refs):
        x_refs, g_ref = refs[:nx], refs[nx]
        gate_ref = refs[nx + 1] if gate is not None else None
        o_ref = refs[-1]
        for gi in range(W // group):
            sl = slice(gi * group, (gi + 1) * group)
            x = x_refs[0][:, sl]
            for xr in x_refs[1:]:
                x = x + xr[:, sl]
            n, _ = _gnorm_stats(x, center)
            y = n * (g_ref[:, sl] * scale)
            if gate_ref is not None:
                y = y * _silu_parts(gate_ref[:, sl])[0]
            o_ref[:, sl] = y.astype(o_ref.dtype)

    row = pl.BlockSpec((ts, W), lambda i: (i, 0))
    vec = pl.BlockSpec((1, W), lambda i: (0, 0))
    args = list(xs) + [gain] + ([gate] if gate is not None else [])
    in_specs = [row] * nx + [vec] + ([row] if gate is not None else [])
    return pl.pallas_call(body, out_shape=jax.ShapeDtypeStruct((S, W), out_dtype), grid=(S // ts,),
                          in_specs=in_specs, out_specs=row, name=name, compiler_params=_params("parallel"))(*args)


def _gnorm_bwd(dy, xs, gain, gate, *, group, center, scale, name):
    S, W = xs[0].shape
    ts = _pick(S, (512, 256, 128))
    nx = len(xs)

    def body(*refs):
        dy_ref = refs[0]
        x_refs, g_ref = refs[1:1 + nx], refs[1 + nx]
        gate_ref = refs[2 + nx] if gate is not None else None
        outs = refs[(3 + nx if gate is not None else 2 + nx):]
        dx_ref, dg_ref = outs[0], outs[1]
        dgate_ref = outs[2] if gate is not None else None

        @pl.when(pl.program_id(0) == 0)
        def _():
            dg_ref[...] = jnp.zeros_like(dg_ref)

        for gi in range(W // group):
            sl = slice(gi * group, (gi + 1) * group)
            x = x_refs[0][:, sl]
            for xr in x_refs[1:]:
                x = x + xr[:, sl]
            n, r = _gnorm_stats(x, center)
            dyv = dy_ref[:, sl].astype(F32)
            g = g_ref[:, sl] * scale
            if gate_ref is not None:
                act, dact = _silu_parts(gate_ref[:, sl])
                dgate_ref[:, sl] = dyv * n * g * dact
                dyv = dyv * act
            dg_ref[:, sl] += jnp.sum(dyv * n, axis=0, keepdims=True) * scale
            dn = dyv * g
            t = dn - n * jnp.mean(dn * n, axis=-1, keepdims=True)
            if center:
                t = t - jnp.mean(dn, axis=-1, keepdims=True)
            dx_ref[:, sl] = r * t

    row = pl.BlockSpec((ts, W), lambda i: (i, 0))
    vec = pl.BlockSpec((1, W), lambda i: (0, 0))
    args = [dy] + list(xs) + [gain] + ([gate] if gate is not None else [])
    in_specs = [row] * (1 + nx) + [vec] + ([row] if gate is not None else [])
    out_shape = [jax.ShapeDtypeStruct((S, W), F32), jax.ShapeDtypeStruct((1, W), F32)]
    out_specs = [row, vec]
    if gate is not None:
        out_shape.append(jax.ShapeDtypeStruct((S, W), F32))
        out_specs.append(row)
    out = pl.pallas_call(body, out_shape=out_shape, grid=(S // ts,), in_specs=in_specs, out_specs=out_specs,
                         name=name, compiler_params=_params("arbitrary"))(*args)
    return out[0], out[1], (out[2] if gate is not None else None)


def _make_gnorm(nx, has_gate, group, center, scale, tag):
    kw = dict(group=group, center=center, scale=scale)

    @jax.custom_vjp
    def op(*args):
        return fwd(*args)[0]

    def fwd(*args):
        xs, gain = args[:nx], args[nx]
        gate = args[nx + 1] if has_gate else None
        y = _gnorm_fwd(xs, gain[None, :], gate, out_dtype=F32, name=tag + "_fwd", **kw)
        return y, args

    def bwd(args, dy):
        xs, gain = args[:nx], args[nx]
        gate = args[nx + 1] if has_gate else None
        dx, dg, dgate = _gnorm_bwd(dy, xs, gain[None, :], gate, name=tag + "_bwd", **kw)
        return (dx,) * nx + (dg[0],) + ((dgate,) if has_gate else ())

    op.defvjp(fwd, bwd)
    return op


@jax.custom_vjp
def norm_matmul(x, g, w):
    return _norm_matmul_fwd(x, g, w)[0]


def _norm_matmul_fwd(x, g, w):
    xn = _gnorm_fwd([x], g[None, :], None, group=x.shape[1], center=False, scale=1.0, out_dtype=MXU, name="rms_in")
    return _mm(xn, w, name="mm_in"), (x, g, w, xn)


def _norm_matmul_bwd(saved, dh):
    x, g, w, xn = saved
    dhb = dh.astype(MXU)
    dxn = _mm(dhb, w, nt=True, name="mm_in_dx")
    dx, dg, _ = _gnorm_bwd(dxn, [x], g[None, :], None, group=x.shape[1], center=False, scale=1.0, name="rms_in_bwd")
    dw = _mm(xn.T, dhb, out_dtype=w.dtype, name="mm_in_dw")
    return dx, dg[0], dw


norm_matmul.defvjp(_norm_matmul_fwd, _norm_matmul_bwd)


@jax.custom_vjp
def out_proj(y, w, x):
    return _out_proj_fwd(y, w, x)[0]


def _out_proj_fwd(y, w, x):
    yb = y.astype(MXU)
    return _mm(yb, w, res=x, name="mm_out"), (yb, w)


def _out_proj_bwd(saved, dout):
    yb, w = saved
    db = dout.astype(MXU)
    dy = _mm(db, w, nt=True, name="mm_out_dy")
    dw = _mm(yb.T, db, out_dtype=w.dtype, name="mm_out_dw")
    return dy, dw, dout


out_proj.defvjp(_out_proj_fwd, _out_proj_bwd)


@jax.custom_vjp
def mlp(x, g, w_up, w_down):
    return _mlp_fwd(x, g, w_up, w_down)[0]


def _mlp_fwd(x, g, w_up, w_down):
    hm = _gnorm_fwd([x], g[None, :], None, group=x.shape[1], center=False, scale=1.0, out_dtype=MXU, name="rms_mlp")
    u, act = _mm(hm, w_up, emit_act=True, name="mm_up")
    return _mm(act, w_down, res=x, name="mm_down"), (x, g, w_up, w_down, hm, u, act)


def _mlp_bwd(saved, dout):
    x, g, w_up, w_down, hm, u, act = saved
    db = dout.astype(MXU)
    du = _mm(db, w_down, nt=True, u_in=u, out_dtype=MXU, name="mm_down_da")
    dw_down = _mm(act.T, db, out_dtype=w_down.dtype, name="mm_down_dw")
    dhm = _mm(du, w_up, nt=True, name="mm_up_dx")
    dw_up = _mm(hm.T, du, out_dtype=w_up.dtype, name="mm_up_dw")
    dx, dg, _ = _gnorm_bwd(dhm, [x], g[None, :], None, group=x.shape[1], center=False, scale=1.0, name="rms_mlp_bwd")
    return dout + dx, dg[0], dw_up, dw_down


mlp.defvjp(_mlp_fwd, _mlp_bwd)


@jax.custom_vjp
def mlp_carry(x, g, w_up, w_down, nxt_in, nxt_out, nxt_up, nxt_down):
    return _mlp_carry_fwd(x, g, w_up, w_down, nxt_in, nxt_out, nxt_up, nxt_down)[0]


def _mlp_carry_fwd(x, g, w_up, w_down, *nxt):
    wire = [t.astype(WIRE) for t in nxt]
    hm = _gnorm_fwd([x], g[None, :], None, group=x.shape[1], center=False, scale=1.0, out_dtype=MXU, name="rms_mlp")
    u, act, f_in, f_out = _mm(hm, w_up, emit_act=True, rider=_Gather(wire[:2]), name="mm_up_gather")
    out, f_up, f_down = _mm(act, w_down, res=x, rider=_Gather(wire[2:]), name="mm_down_gather")
    return (out, f_in, f_out, f_up, f_down), (x, g, w_up, w_down, hm, u, act)


def _mlp_carry_bwd(saved, cts):
    x, g, w_up, w_down, hm, u, act = saved
    dout, d_in, d_out, d_up, d_down = cts
    db = dout.astype(MXU)
    du, l_in = _mm(db, w_down, nt=True, u_in=u, out_dtype=MXU, rider=_Exchange([d_in]), name="mm_down_da_xchg")
    dw_down, l_up = _mm(act.T, db, out_dtype=w_down.dtype, rider=_Exchange([d_up]), name="mm_down_dw_xchg")
    dhm, l_down = _mm(du, w_up, nt=True, rider=_Exchange([d_down]), name="mm_up_dx_xchg")
    dw_up, l_out = _mm(hm.T, du, out_dtype=w_up.dtype, rider=_Exchange([d_out]), name="mm_up_dw_xchg")
    dx, dg, _ = _gnorm_bwd(dhm, [x], g[None, :], None, group=x.shape[1], center=False, scale=1.0, name="rms_mlp_bwd")
    grads = [_sum_slots(l, "sum_grads_%d" % i) for i, l in enumerate((l_in, l_out, l_up, l_down))]
    return (dout + dx, dg[0], dw_up, dw_down, *grads)


mlp_carry.defvjp(_mlp_carry_fwd, _mlp_carry_bwd)


def _order(C, rev):
    i = np.arange(C)
    return (C - 1 - i) if rev else i


def _hg_constants(C, rev):
    p = _order(C, rev)
    pi, pj = p[:, None], p[None, :]
    tri = (pj <= pi).astype(np.float32)
    masks = [((pi // HG_SUB) == (pj // HG_SUB)) & (pj <= pi)]
    h = HG_SUB
    halves = []
    while h < C:
        masks.append(((pi // (2 * h)) == (pj // (2 * h))) & ((pi // h) % 2 == 1) & ((pj // h) % 2 == 0))
        halves.append(h)
        h *= 2
    return tri, np.stack(masks).astype(np.float32), halves


def _hg_tables(b_scr, C, h, rev):
    nb = C // h
    g_rows, e_rows = [], []
    for rb in range(nb):
        if rev:
            e = b_scr[pl.ds(rb * h, 1), :]
            g = b_scr[pl.ds((rb + 1) * h, 1), :] if rb < nb - 1 else None
        else:
            e = b_scr[pl.ds((rb + 1) * h - 1, 1), :]
            g = b_scr[pl.ds(rb * h - 1, 1), :] if rb >= 1 else None
        e_rows.append(jnp.broadcast_to(e, (h, HG_D)))
        g_rows.append(jnp.zeros((h, HG_D), F32) if g is None else jnp.broadcast_to(g, (h, HG_D)))
    return jnp.concatenate(g_rows, axis=0), jnp.concatenate(e_rows, axis=0)


def _hg_gates(z, lb):
    sg = jax.nn.sigmoid(z)
    f = lb + (1.0 - lb) * sg
    return sg, f, 1.0 - f, jnp.log(f)


def _hg_exponents(b, b_scr, C, halves, rev):
    g0, _ = _hg_tables(b_scr, C, HG_SUB, rev)
    p0 = b - g0
    out = [(p0, jnp.minimum(-p0, EXP_CLAMP))]
    for h in halves:
        g, e = _hg_tables(b_scr, C, h, rev)
        out.append((jnp.minimum(b - g, 0.0), jnp.minimum(e - b, 0.0)))
    return out


def _hg_scan_fwd(q, v, z, lb, rev, name):
    S = q.shape[0]
    C = SCAN_C
    nc = S // C
    tri, masks, halves = _hg_constants(C, rev)
    nlev = masks.shape[0]
    end_row = 0 if rev else C - 1
    vT = v.astype(MXU).T

    def body(q_ref, v_ref, vT_ref, z_ref, lb_ref, tri_ref, mask_ref, o_ref, st_ref, s_scr, b_scr):
        @pl.when(pl.program_id(1) == 0)
        def _():
            s_scr[...] = jnp.zeros_like(s_scr)

        for hh in range(SCAN_HPS):
            sl = slice(hh * HG_D, (hh + 1) * HG_D)
            bh = b_scr.at[hh]
            _, f, k, lf = _hg_gates(z_ref[:, sl], lb_ref[:, sl])
            b = _dot01(tri_ref[...], lf)
            bh[...] = b
            qv, vv = q_ref[:, sl], v_ref[:, sl]
            st = s_scr[hh]
            st_ref[hh, 0] = st
            a = jnp.zeros((C, C), F32)
            for lv, (eq, ek) in enumerate(_hg_exponents(b, bh, C, halves, rev)):
                a = a + mask_ref[lv] * _dot_nt(qv * jnp.exp(eq), k * jnp.exp(ek))
            bend = bh[pl.ds(end_row, 1), :]
            o_ref[:, sl] = _dot(a, vv) + _dot_nt(qv * jnp.exp(b), st)
            s_scr[hh] = st * jnp.exp(bend) + _dot(vT_ref[sl, :], k * jnp.exp(bend - b))

    cidx = (lambda c: nc - 1 - c) if rev else (lambda c: c)
    wid = SCAN_HPS * HG_D
    blk = pl.BlockSpec((C, wid), lambda h, c: (cidx(c), h))
    o, states = pl.pallas_call(
        body,
        out_shape=[jax.ShapeDtypeStruct((S, HGW), F32), jax.ShapeDtypeStruct((HG_HEADS, nc, HG_D, HG_D), F32)],
        grid=(HG_HEADS // SCAN_HPS, nc),
        in_specs=[blk, blk, pl.BlockSpec((wid, C), lambda h, c: (h, cidx(c))), blk,
                  pl.BlockSpec((1, wid), lambda h, c: (0, h)),
                  pl.BlockSpec((C, C), lambda h, c: (0, 0)), pl.BlockSpec((nlev, C, C), lambda h, c: (0, 0, 0))],
        out_specs=[blk, pl.BlockSpec((SCAN_HPS, 1, HG_D, HG_D), lambda h, c: (h, cidx(c), 0, 0))],
        scratch_shapes=[pltpu.VMEM((SCAN_HPS, HG_D, HG_D), F32), pltpu.VMEM((SCAN_HPS, C, HG_D), F32)],
        name=name, compiler_params=_params("parallel", "arbitrary"),
    )(q, v.astype(MXU), vT, z, lb, jnp.asarray(tri, MXU), jnp.asarray(masks))
    return o, states


def _hg_scan_bwd(q, v, z, lb, states, do, rev, name):
    S = q.shape[0]
    C = SCAN_C
    nc = S // C
    tri, masks, halves = _hg_constants(C, rev)
    nlev = masks.shape[0]
    end_row = 0 if rev else C - 1
    masks_t = np.ascontiguousarray(np.transpose(masks, (0, 2, 1)))
    dob = do.astype(MXU)

    def body(q_ref, v_ref, z_ref, lb_ref, do_ref, doT_ref, st_ref, tri_ref, triT_ref, mask_ref, maskT_ref,
             dq_ref, dv_ref, dz_ref, dlb_ref, dn_scr, b_scr):
        @pl.when(pl.program_id(1) == 0)
        def _():
            dn_scr[...] = jnp.zeros_like(dn_scr)
            dlb_ref[...] = jnp.zeros_like(dlb_ref)

        for hh in range(SCAN_HPS):
            sl = slice(hh * HG_D, (hh + 1) * HG_D)
            bh = b_scr.at[hh]
            lb_v = lb_ref[:, sl]
            sg, f, k, lf = _hg_gates(z_ref[:, sl], lb_v)
            b = _dot01(tri_ref[...], lf)
            bh[...] = b
            qv, vv, dov = q_ref[:, sl], v_ref[:, sl], do_ref[:, sl]
            st, dn = st_ref[hh, 0], dn_scr[hh]
            da = _dot_nt(dov, vv)
            da_t = da.T
            a = jnp.zeros((C, C), F32)
            dq = jnp.zeros((C, HG_D), F32)
            dk = jnp.zeros((C, HG_D), F32)
            for lv, (eq, ek) in enumerate(_hg_exponents(b, bh, C, halves, rev)):
                xq, xk = jnp.exp(eq), jnp.exp(ek)
                qs, ks = qv * xq, k * xk
                a = a + mask_ref[lv] * _dot_nt(qs, ks)
                dq = dq + _dot(mask_ref[lv] * da, ks) * xq
                dk = dk + _dot(maskT_ref[lv] * da_t, qs) * xk
            bend = bh[pl.ds(end_row, 1), :]
            xb, xe, xend = jnp.exp(b), jnp.exp(bend - b), jnp.exp(bend)
            dq = dq + _dot(dov, st) * xb
            dk_state = _dot(vv, dn) * xe
            dk = dk + dk_state
            dv_ref[:, sl] = _dot(a.T, dov) + _dot_nt(k * xe, dn)
            dn_scr[hh] = dn * xend + _dot(doT_ref[sl, :], qv * xb)
            extra = jnp.sum(k * dk_state, axis=0, keepdims=True) + xend * jnp.sum(st * dn, axis=0, keepdims=True)
            rows = lax.broadcasted_iota(jnp.int32, (C, HG_D), 0)
            db = qv * dq - k * dk + jnp.where(rows == end_row, extra, 0.0)
            df = _dot01(triT_ref[...], db) / f - dk
            dq_ref[:, sl] = dq
            dz_ref[:, sl] = df * (1.0 - lb_v) * sg * (1.0 - sg)
            dlb_ref[:, sl] += jnp.sum(df * (1.0 - sg), axis=0, keepdims=True)

    cidx = (lambda c: c) if rev else (lambda c: nc - 1 - c)
    wid = SCAN_HPS * HG_D
    blk = pl.BlockSpec((C, wid), lambda h, c: (cidx(c), h))
    vec = pl.BlockSpec((1, wid), lambda h, c: (0, h))
    cc = pl.BlockSpec((C, C), lambda h, c: (0, 0))
    lcc = pl.BlockSpec((nlev, C, C), lambda h, c: (0, 0, 0))
    sd = jax.ShapeDtypeStruct((S, HGW), F32)
    return pl.pallas_call(
        body, out_shape=[sd, sd, sd, jax.ShapeDtypeStruct((1, HGW), F32)], grid=(HG_HEADS // SCAN_HPS, nc),
        in_specs=[blk, blk, blk, vec, blk, pl.BlockSpec((wid, C), lambda h, c: (h, cidx(c))),
                  pl.BlockSpec((SCAN_HPS, 1, HG_D, HG_D), lambda h, c: (h, cidx(c), 0, 0)), cc, cc, lcc, lcc],
        out_specs=[blk, blk, blk, vec],
        scratch_shapes=[pltpu.VMEM((SCAN_HPS, HG_D, HG_D), F32), pltpu.VMEM((SCAN_HPS, C, HG_D), F32)],
        name=name, compiler_params=_params("parallel", "arbitrary"),
    )(q, v.astype(MXU), z, lb, dob, dob.T, states, jnp.asarray(tri, MXU), jnp.asarray(tri.T, MXU),
      jnp.asarray(masks), jnp.asarray(masks_t))


def _make_hg_scan(rev):
    tag = "hg_rev" if rev else "hg_fwd"

    @jax.custom_vjp
    def op(q, v, z, lb):
        return fwd(q, v, z, lb)[0]

    def fwd(q, v, z, lb):
        o, states = _hg_scan_fwd(q, v, z, lb[None, :], rev, tag)
        return o, (q, v, z, lb, states)

    def bwd(saved, do):
        q, v, z, lb, states = saved
        dq, dv, dz, dlb = _hg_scan_bwd(q, v, z, lb[None, :], states, do, rev, tag + "_bwd")
        return dq, dv, dz, dlb[0]

    op.defvjp(fwd, bwd)
    return op


def _ret_constants(C, rev):
    hidx = np.arange(RET_HEADS, dtype=np.float64)
    lg = np.log1p(-np.exp2(-5.0 - hidx))
    if rev:
        lg = lg[::-1]
    p = _order(C, rev).astype(np.float64)
    rel = p[:, None] - p[None, :]
    dmat = np.where(rel >= 0, np.exp(lg[:, None, None] * np.maximum(rel, 0.0)), 0.0)
    xi = np.exp(lg[:, None] * (p[None, :] + 1.0))
    zeta = np.exp(lg[:, None] * (C - 1.0 - p[None, :]))
    gc = np.exp(lg * C)
    bc = lambda t: np.ascontiguousarray(np.broadcast_to(t[:, :, None], (RET_HEADS, C, RET_DK))).astype(np.float32)
    gcb = np.ascontiguousarray(np.broadcast_to(gc[:, None, None], (RET_HEADS, 1, RET_DK))).astype(np.float32)
    return dmat.astype(np.float32), bc(xi), bc(zeta), gcb


def _ret_scan_fwd(qh, kh, v, rev, name):
    S = v.shape[0]
    C = SCAN_C
    nc = S // C
    dmat, xi, zeta, gc = _ret_constants(C, rev)
    vb = v.astype(MXU)

    def body(q_ref, k_ref, v_ref, vT_ref, d_ref, xi_ref, zeta_ref, gc_ref, o_ref, st_ref, s_scr):
        @pl.when(pl.program_id(1) == 0)
        def _():
            s_scr[...] = jnp.zeros_like(s_scr)

        for hh in range(SCAN_HPS):
            sl = slice(hh * RET_DV, (hh + 1) * RET_DV)
            qv, kv = q_ref[hh], k_ref[hh]
            st = s_scr[hh]
            st_ref[hh, 0] = st
            sc = _dot_nt(qv, kv) * d_ref[hh]
            o_ref[:, sl] = _dot(sc, v_ref[:, sl]) + _dot_nt(qv * xi_ref[hh], st)
            s_scr[hh] = st * gc_ref[hh] + _dot(vT_ref[sl, :], kv * zeta_ref[hh])

    cidx = (lambda c: nc - 1 - c) if rev else (lambda c: c)
    hk = pl.BlockSpec((SCAN_HPS, C, RET_DK), lambda h, c: (h, cidx(c), 0))
    vblk = pl.BlockSpec((C, SCAN_HPS * RET_DV), lambda h, c: (cidx(c), h))
    tab = pl.BlockSpec((SCAN_HPS, C, RET_DK), lambda h, c: (h, 0, 0))
    return pl.pallas_call(
        body,
        out_shape=[jax.ShapeDtypeStruct((S, RETW), F32), jax.ShapeDtypeStruct((RET_HEADS, nc, RET_DV, RET_DK), F32)],
        grid=(RET_HEADS // SCAN_HPS, nc),
        in_specs=[hk, hk, vblk, pl.BlockSpec((SCAN_HPS * RET_DV, C), lambda h, c: (h, cidx(c))),
                  pl.BlockSpec((SCAN_HPS, C, C), lambda h, c: (h, 0, 0)), tab, tab,
                  pl.BlockSpec((SCAN_HPS, 1, RET_DK), lambda h, c: (h, 0, 0))],
        out_specs=[vblk, pl.BlockSpec((SCAN_HPS, 1, RET_DV, RET_DK), lambda h, c: (h, cidx(c), 0, 0))],
        scratch_shapes=[pltpu.VMEM((SCAN_HPS, RET_DV, RET_DK), F32)],
        name=name, compiler_params=_params("parallel", "arbitrary"),
    )(qh, kh, vb, vb.T, jnp.asarray(dmat), jnp.asarray(xi), jnp.asarray(zeta), jnp.asarray(gc))


def _ret_scan_bwd(qh, kh, v, states, do, rev, name):
    S = v.shape[0]
    C = SCAN_C
    nc = S // C
    dmat, xi, zeta, gc = _ret_constants(C, rev)
    vb, dob = v.astype(MXU), do.astype(MXU)

    def body(q_ref, k_ref, v_ref, do_ref, doT_ref, st_ref, d_ref, xi_ref, zeta_ref, gc_ref,
             dq_ref, dk_ref, dv_ref, dn_scr):
        @pl.when(pl.program_id(1) == 0)
        def _():
            dn_scr[...] = jnp.zeros_like(dn_scr)

        for hh in range(SCAN_HPS):
            sl = slice(hh * RET_DV, (hh + 1) * RET_DV)
            qv, kv, vv, dov = q_ref[hh], k_ref[hh], v_ref[:, sl], do_ref[:, sl]
            st, dn = st_ref[hh, 0], dn_scr[hh]
            dm = d_ref[hh]
            sc = _dot_nt(qv, kv) * dm
            dsc = _dot_nt(dov, vv) * dm
            kz = kv * zeta_ref[hh]
            dq_ref[hh] = _dot(dsc, kv) + _dot(dov, st) * xi_ref[hh]
            dk_ref[hh] = _dot(dsc.T, qv) + _dot(vv, dn) * zeta_ref[hh]
            dv_ref[:, sl] = _dot(sc.T, dov) + _dot_nt(kz, dn)
            dn_scr[hh] = dn * gc_ref[hh] + _dot(doT_ref[sl, :], qv * xi_ref[hh])

    cidx = (lambda c: c) if rev else (lambda c: nc - 1 - c)
    hk = pl.BlockSpec((SCAN_HPS, C, RET_DK), lambda h, c: (h, cidx(c), 0))
    vblk = pl.BlockSpec((C, SCAN_HPS * RET_DV), lambda h, c: (cidx(c), h))
    tab = pl.BlockSpec((SCAN_HPS, C, RET_DK), lambda h, c: (h, 0, 0))
    hs = jax.ShapeDtypeStruct(qh.shape, F32)
    return pl.pallas_call(
        body, out_shape=[hs, hs, jax.ShapeDtypeStruct((S, RETW), F32)], grid=(RET_HEADS // SCAN_HPS, nc),
        in_specs=[hk, hk, vblk, vblk, pl.BlockSpec((SCAN_HPS * RET_DV, C), lambda h, c: (h, cidx(c))),
                  pl.BlockSpec((SCAN_HPS, 1, RET_DV, RET_DK), lambda h, c: (h, cidx(c), 0, 0)),
                  pl.BlockSpec((SCAN_HPS, C, C), lambda h, c: (h, 0, 0)), tab, tab,
                  pl.BlockSpec((SCAN_HPS, 1, RET_DK), lambda h, c: (h, 0, 0))],
        out_specs=[hk, hk, vblk],
        scratch_shapes=[pltpu.VMEM((SCAN_HPS, RET_DV, RET_DK), F32)],
        name=name, compiler_params=_params("parallel", "arbitrary"),
    )(qh, kh, vb, dob, dob.T, states, jnp.asarray(dmat), jnp.asarray(xi), jnp.asarray(zeta), jnp.asarray(gc))


def _make_ret_scan(rev):
    tag = "ret_rev" if rev else "ret_fwd"

    @jax.custom_vjp
    def op(qh, kh, v):
        return fwd(qh, kh, v)[0]

    def fwd(qh, kh, v):
        o, states = _ret_scan_fwd(qh, kh, v, rev, tag)
        return o, (qh, kh, v, states)

    def bwd(saved, do):
        qh, kh, v, states = saved
        return tuple(_ret_scan_bwd(qh, kh, v, states, do, rev, tag + "_bwd"))

    op.defvjp(fwd, bwd)
    return op


def _rope_tables(S, scale):
    half = RET_DK // 2
    inv = ROPE_BASE ** (-np.arange(half, dtype=np.float32) / half)
    ang = np.arange(S, dtype=np.float32)[:, None] * inv[None, :]
    cos, sin = np.cos(ang), np.sin(ang)
    cos_t = np.tile(np.concatenate([cos, cos], axis=1), (1, RET_HEADS)) * scale
    sin_t = np.tile(np.concatenate([-sin, sin], axis=1), (1, RET_HEADS)) * scale
    return cos_t.astype(np.float32), sin_t.astype(np.float32)


def _rope_apply(t, cos_t, sin_t, name):
    S, W = t.shape
    ts = _pick(S, (512, 256, 128))
    half = RET_DK // 2

    def body(t_ref, c_ref, s_ref, o_ref):
        tv = t_ref[...]
        lane = lax.broadcasted_iota(jnp.int32, tv.shape, 1)
        partner = jnp.where(lane % RET_DK < half, pltpu.roll(tv, W - half, 1), pltpu.roll(tv, half, 1))
        o_ref[...] = tv * c_ref[...] + partner * s_ref[...]

    row = pl.BlockSpec((ts, W), lambda i: (i, 0))
    return pl.pallas_call(body, out_shape=jax.ShapeDtypeStruct((S, W), F32), grid=(S // ts,),
                          in_specs=[row, row, row], out_specs=row, name=name,
                          compiler_params=_params("parallel"))(t, jnp.asarray(cos_t), jnp.asarray(sin_t))


def _make_rope(scale, tag):
    def apply(t):
        cos_t, sin_t = _rope_tables(t.shape[0], scale)
        return _rope_apply(t, cos_t, sin_t, tag)

    op = jax.custom_vjp(apply)

    def fwd(t):
        return apply(t), None

    def bwd(_, dout):
        cos_t, sin_t = _rope_tables(dout.shape[0], scale)
        return (_rope_apply(dout, cos_t, -sin_t, tag + "_bwd"),)

    op.defvjp(fwd, bwd)
    return op


def _att_geometry(L):
    tq = _pick(L, (512, 256, 128))
    return tq, L // tq, tq // ATT_HALO


def _att_specs(tq, per):
    main = pl.BlockSpec((1, tq, DIL_HD), lambda b, n: (b, n, 0))
    prev = pl.BlockSpec((1, ATT_HALO, DIL_HD), lambda b, n: (b, jnp.maximum(n * per - 1, 0), 0))
    return main, prev


def _att_valid(n, u, tq, L):
    ii = lax.broadcasted_iota(jnp.int32, (ATT_SB, ATT_WIN), 0)
    jj = lax.broadcasted_iota(jnp.int32, (ATT_SB, ATT_WIN), 1)
    key = n * tq + u * ATT_SB - ATT_HALO + jj
    return (jnp.abs(jj - ATT_HALO - ii) <= ATT_HALO) & (key >= 0) & (key < L)


def _att_fill(buf, prev_ref, main_ref, next_ref, tq):
    buf[pl.ds(0, ATT_HALO), :] = prev_ref[0]
    buf[pl.ds(ATT_HALO, tq), :] = main_ref[0]
    buf[pl.ds(ATT_HALO + tq, ATT_HALO), :] = next_ref[0]


def _att_fwd(q, k, v, bias, dil, name):
    B, L, _ = q.shape
    tq, nt, per = _att_geometry(L)
    last = L // ATT_HALO - 1

    def body(q_ref, kp_ref, k_ref, kn_ref, vp_ref, v_ref, vn_ref, bias_ref, o_ref, lse_ref, kbuf, vbuf):
        n = pl.program_id(1)
        _att_fill(kbuf, kp_ref, k_ref, kn_ref, tq)
        _att_fill(vbuf, vp_ref, v_ref, vn_ref, tq)
        for u in range(tq // ATT_SB):
            rows = pl.ds(u * ATT_SB, ATT_SB)
            win = pl.ds(u * ATT_SB, ATT_WIN)
            s = _dot_nt(q_ref[0, rows, :], kbuf[win, :]) + bias_ref[0]
            s = jnp.where(_att_valid(n, u, tq, L), s, -1e30)
            m = jnp.max(s, axis=-1, keepdims=True)
            p = jnp.exp(s - m)
            den = jnp.sum(p, axis=-1, keepdims=True)
            o_ref[0, rows, :] = _dot(p, vbuf[win, :]) / den
            lse_ref[0, rows, :] = jnp.broadcast_to(m + jnp.log(den), (ATT_SB, DIL_HD))

    main, prev = _att_specs(tq, per)
    nxt = pl.BlockSpec((1, ATT_HALO, DIL_HD), lambda b, n: (b, jnp.minimum((n + 1) * per, last), 0))
    sd = jax.ShapeDtypeStruct((B, L, DIL_HD), F32)
    return pl.pallas_call(
        body, out_shape=[sd, sd], grid=(B, nt),
        in_specs=[main, prev, main, nxt, prev, main, nxt,
                  pl.BlockSpec((1, ATT_SB, ATT_WIN), lambda b, n: (b // dil, 0, 0))],
        out_specs=[main, main],
        scratch_shapes=[pltpu.VMEM((tq + 2 * ATT_HALO, DIL_HD), q.dtype), pltpu.VMEM((tq + 2 * ATT_HALO, DIL_HD), q.dtype)],
        name=name, compiler_params=_params("parallel", "arbitrary"),
    )(q, k, k, k, v, v, v, bias)


def _att_bwd(q, k, v, bias, o, lse, do, dlse, dil, name):
    B, L, _ = q.shape
    tq, nt, per = _att_geometry(L)
    last = L // ATT_HALO - 1

    def body(q_ref, kp_ref, k_ref, kn_ref, vp_ref, v_ref, vn_ref, bias_ref, o_ref, lse_ref, do_ref, dlse_ref,
             dq_ref, dk_ref, dkp_ref, dkn_ref, dv_ref, dvp_ref, dvn_ref, dbias_ref, kbuf, vbuf, dkbuf, dvbuf):
        b, n = pl.program_id(0), pl.program_id(1)

        @pl.when((b % dil == 0) & (n == 0))
        def _():
            dbias_ref[...] = jnp.zeros_like(dbias_ref)

        _att_fill(kbuf, kp_ref, k_ref, kn_ref, tq)
        _att_fill(vbuf, vp_ref, v_ref, vn_ref, tq)
        dkbuf[...] = jnp.zeros_like(dkbuf)
        dvbuf[...] = jnp.zeros_like(dvbuf)
        for u in range(tq // ATT_SB):
            rows = pl.ds(u * ATT_SB, ATT_SB)
            win = pl.ds(u * ATT_SB, ATT_WIN)
            qu, kw, vw = q_ref[0, rows, :], kbuf[win, :], vbuf[win, :]
            dou = do_ref[0, rows, :]
            s = _dot_nt(qu, kw) + bias_ref[0]
            lse_u = jnp.max(lse_ref[0, rows, :], axis=-1, keepdims=True)
            p = jnp.where(_att_valid(n, u, tq, L), jnp.exp(s - lse_u), 0.0)
            corr = jnp.sum(dlse_ref[0, rows, :] - dou * o_ref[0, rows, :], axis=-1, keepdims=True)
            ds = p * (_dot_nt(dou, vw) + corr)
            dq_ref[0, rows, :] = _dot(ds, kw)
            dkbuf[win, :] += _dot(ds.T, qu)
            dvbuf[win, :] += _dot(p.T, dou)
            dbias_ref[0] += ds
        for full, lo, hi in ((dkbuf, dkp_ref, dkn_ref), (dvbuf, dvp_ref, dvn_ref)):
            lo[0, 0] = full[pl.ds(0, ATT_HALO), :]
            hi[0, 0] = full[pl.ds(ATT_HALO + tq, ATT_HALO), :]
        dk_ref[0] = dkbuf[pl.ds(ATT_HALO, tq), :]
        dv_ref[0] = dvbuf[pl.ds(ATT_HALO, tq), :]

    main, prev = _att_specs(tq, per)
    nxt = pl.BlockSpec((1, ATT_HALO, DIL_HD), lambda b, n: (b, jnp.minimum((n + 1) * per, last), 0))
    halo = pl.BlockSpec((1, 1, ATT_HALO, DIL_HD), lambda b, n: (b, n, 0, 0))
    bias_spec = pl.BlockSpec((1, ATT_SB, ATT_WIN), lambda b, n: (b // dil, 0, 0))
    sd = jax.ShapeDtypeStruct((B, L, DIL_HD), F32)
    hd = jax.ShapeDtypeStruct((B, nt, ATT_HALO, DIL_HD), F32)
    width = tq + 2 * ATT_HALO
    dq, dk, dkp, dkn, dv, dvp, dvn, dbias = pl.pallas_call(
        body, out_shape=[sd, sd, hd, hd, sd, hd, hd, jax.ShapeDtypeStruct(bias.shape, F32)], grid=(B, nt),
        in_specs=[main, prev, main, nxt, prev, main, nxt, bias_spec, main, main, main, main],
        out_specs=[main, main, halo, halo, main, halo, halo, bias_spec],
        scratch_shapes=[pltpu.VMEM((width, DIL_HD), q.dtype), pltpu.VMEM((width, DIL_HD), q.dtype),
                        pltpu.VMEM((width, DIL_HD), F32), pltpu.VMEM((width, DIL_HD), F32)],
        name=name, compiler_params=_params("arbitrary", "arbitrary"),
    )(q, k, k, k, v, v, v, bias, o, lse, do, dlse)

    def fold(mainv, lo, hi):
        t = mainv.reshape(B, nt, tq, DIL_HD)
        if nt > 1:
            t = t.at[:, :-1, tq - ATT_HALO:, :].add(lo[:, 1:])
            t = t.at[:, 1:, :ATT_HALO, :].add(hi[:, :-1])
        return t.reshape(B, L, DIL_HD)

    return dq, fold(dk, dkp, dkn), fold(dv, dvp, dvn), dbias


def _make_attention(dil):
    tag = "att_d%d" % dil

    @jax.custom_vjp
    def op(q, k, v, bias):
        return fwd(q, k, v, bias)[0]

    def fwd(q, k, v, bias):
        qb, kb, vb = q.astype(MXU), k.astype(MXU), v.astype(MXU)
        o, lse = _att_fwd(qb, kb, vb, bias, dil, tag)
        return (o, lse), (qb, kb, vb, bias, o, lse)

    def bwd(saved, cts):
        qb, kb, vb, bias, o, lse = saved
        do, dlse = cts
        return tuple(_att_bwd(qb, kb, vb, bias, o, lse, do, dlse, dil, tag + "_bwd"))

    op.defvjp(fwd, bwd)
    return op


def _merge_weights(l0, l1, l2):
    m = jnp.maximum(jnp.maximum(l0, l1), l2)
    e0, e1, e2 = jnp.exp(l0 - m), jnp.exp(l1 - m), jnp.exp(l2 - m)
    inv = 1.0 / (e0 + e1 + e2)
    return e0 * inv, e1 * inv, e2 * inv


def _merge_call(body, n_in, n_out, shape, name):
    R, W = shape
    ts = _pick(R, (1024, 512, 256, 128))
    row = pl.BlockSpec((ts, W), lambda i: (i, 0))
    sd = jax.ShapeDtypeStruct(shape, F32)
    return pl.pallas_call(body, out_shape=[sd] * n_out, grid=(R // ts,), in_specs=[row] * n_in,
                          out_specs=[row] * n_out, name=name, compiler_params=_params("parallel"))


@jax.custom_vjp
def dil_merge(o0, o1, o2, l0, l1, l2):
    return _dil_merge_fwd(o0, o1, o2, l0, l1, l2)[0]


def _dil_merge_fwd(*args):
    def body(o0, o1, o2, l0, l1, l2, out):
        w0, w1, w2 = _merge_weights(l0[...], l1[...], l2[...])
        out[...] = w0 * o0[...] + w1 * o1[...] + w2 * o2[...]

    return _merge_call(body, 6, 1, args[0].shape, "dil_merge")(*args)[0], args


def _dil_merge_bwd(args, dout):
    def body(o0, o1, o2, l0, l1, l2, d, do0, do1, do2, dl0, dl1, dl2):
        ws = _merge_weights(l0[...], l1[...], l2[...])
        dv = d[...]
        dws = [dv * o[...] for o in (o0, o1, o2)]
        mean = ws[0] * dws[0] + ws[1] * dws[1] + ws[2] * dws[2]
        for w, dw, do_ref, dl_ref in zip(ws, dws, (do0, do1, do2), (dl0, dl1, dl2)):
            do_ref[...] = w * dv
            dl_ref[...] = w * (dw - mean)

    return tuple(_merge_call(body, 7, 6, args[0].shape, "dil_merge_bwd")(*args, dout))


dil_merge.defvjp(_dil_merge_fwd, _dil_merge_bwd)


def _t5_bucket(rel):
    nb = REL_BUCKETS // 2
    max_exact = nb // 2
    sign_off = np.where(rel > 0, nb, 0)
    n = np.abs(rel)
    nf = np.maximum(n, 1).astype(np.float32)
    large = max_exact + (np.log(nf / np.float32(max_exact)) / np.float32(math.log(REL_MAX_DIST / max_exact))
                         * np.float32(nb - max_exact)).astype(np.int32)
    large = np.minimum(large, nb - 1)
    return sign_off + np.where(n < max_exact, n, large)


def _loss_grad(xf, target):
    S, D = xf.shape
    ts = _pick(S, (512, 256, 128))

    def body(x_ref, t_ref, dy_ref, part_ref):
        @pl.when(pl.program_id(0) == 0)
        def _():
            part_ref[...] = jnp.zeros_like(part_ref)

        err = x_ref[...] - t_ref[...]
        dy_ref[...] = err * (1.0 / D)
        part_ref[...] += jnp.sum(err * err, axis=0, keepdims=True)

    row = pl.BlockSpec((ts, D), lambda i: (i, 0))
    return pl.pallas_call(body, out_shape=[jax.ShapeDtypeStruct((S, D), F32), jax.ShapeDtypeStruct((1, D), F32)],
                          grid=(S // ts,), in_specs=[row, row], out_specs=[row, pl.BlockSpec((1, D), lambda i: (0, 0))],
                          name="loss_grad", compiler_params=_params("arbitrary"))(xf, target)


def _adamw_math(g, w, m, v):
    m = ADAM_B1 * m + (1.0 - ADAM_B1) * g
    v = ADAM_B2 * v + (1.0 - ADAM_B2) * (g * g)
    m_hat = m / (1.0 - ADAM_B1 ** ADAM_STEP)
    v_hat = v / (1.0 - ADAM_B2 ** ADAM_STEP)
    delta = -ADAM_LR * (m_hat / (jnp.sqrt(v_hat) + ADAM_EPS) + ADAM_WD * w)
    return delta, m, v


def _adamw(g, w, m, v, name):
    Lw, R, C = w.shape
    tr = _pick(R, (256, 128, 64, 32, 16, 8))

    def body(g_ref, w_ref, m_ref, v_ref, d_ref, nm_ref, nv_ref):
        d_ref[0], nm_ref[0], nv_ref[0] = _adamw_math(g_ref[0], w_ref[0], m_ref[0], v_ref[0])

    blk = pl.BlockSpec((1, tr, C), lambda l, i: (l, i, 0))
    sd = jax.ShapeDtypeStruct(w.shape, F32)
    return pl.pallas_call(body, out_shape=[sd] * 3, grid=(Lw, R // tr), in_specs=[blk] * 4, out_specs=[blk] * 3,
                          name=name, compiler_params=_params("parallel", "parallel"))(g, w, m, v)


def _mesh_pos():
    return lax.axis_index("x"), lax.axis_index("y"), lax.axis_index("c")


def _slot(x, y, c):
    return 4 * x + 2 * y + c


class _Comm:
    def __init__(self, arrays):
        self.arrays = list(arrays)

    def sem_shapes(self):
        na = len(self.arrays)
        return [pltpu.SemaphoreType.DMA((7 * na,)), pltpu.SemaphoreType.DMA((7 * na,)), pltpu.SemaphoreType.DMA((na,))]

    def mid(self, ins, outs, sems):
        pass

    def call(self, name):
        na = len(self.arrays)

        def body(*refs):
            ins, outs, sems = refs[:na], refs[na:2 * na], refs[2 * na:]
            self.start(ins, outs, sems)
            self.mid(ins, outs, sems)
            self.finish(ins, outs, sems)

        anyspec = pl.BlockSpec(memory_space=pl.ANY)
        return pl.pallas_call(body, out_shape=self.out_shapes(), in_specs=[anyspec] * na, out_specs=[anyspec] * na,
                              scratch_shapes=self.sem_shapes(), name=name)(*self.arrays)


class _Gather(_Comm):
    def out_shapes(self):
        return [jax.ShapeDtypeStruct((N_DEV,) + b.shape, b.dtype) for b in self.arrays]

    def _copies(self, ins, outs, sems):
        send_sems, recv_sems, local_sems = sems
        x, y, c = _mesh_pos()
        me, sibling = (x, y, c), (x, y, 1 - c)
        chips = [(1 - x, y), (x, 1 - y), (1 - x, 1 - y)]
        per = []
        for a in range(len(self.arrays)):
            def copy(k, block, to, src=None, a=a):
                dst = outs[a].at[_slot(*block)]
                return pltpu.make_async_remote_copy(
                    src_ref=dst if src is None else src, dst_ref=dst,
                    send_sem=send_sems.at[7 * a + k], recv_sem=recv_sems.at[7 * a + k],
                    device_id=to, device_id_type=MESH_ID)

            per.append(dict(
                mine=pltpu.make_async_copy(ins[a], outs[a].at[_slot(*me)], local_sems.at[a]),
                first=[copy(0, me, sibling, src=ins[a])] + [copy(1 + j, me, (*ch, c), src=ins[a]) for j, ch in enumerate(chips)],
                passed=[copy(4 + j, (*ch, c), sibling) for j, ch in enumerate(chips)],
                over_ici=[copy(1 + j, (*ch, c), me) for j, ch in enumerate(chips)],
                from_sibling=[copy(0, sibling, me)] + [copy(4 + j, (*ch, 1 - c), me) for j, ch in enumerate(chips)]))
        return per

    def start(self, ins, outs, sems):
        for p in self._copies(ins, outs, sems):
            p["mine"].start()
            for cp in p["first"]:
                cp.start()

    def mid(self, ins, outs, sems):
        for p in self._copies(ins, outs, sems):
            for arrived, onward in zip(p["over_ici"], p["passed"]):
                arrived.wait_recv()
                onward.start()

    def finish(self, ins, outs, sems):
        for p in self._copies(ins, outs, sems):
            for cp in p["from_sibling"]:
                cp.wait_recv()
            for cp in p["first"] + p["passed"]:
                cp.wait_send()
            p["mine"].wait()


class _Exchange(_Comm):
    def out_shapes(self):
        return [jax.ShapeDtypeStruct(f.shape, f.dtype) for f in self.arrays]

    def _copies(self, ins, outs, sems):
        send_sems, recv_sems, local_sems = sems
        x, y, c = _mesh_pos()
        my_slot = _slot(x, y, c)
        local, remote = [], []
        for a in range(len(self.arrays)):
            local.append(pltpu.make_async_copy(ins[a].at[my_slot], outs[a].at[my_slot], local_sems.at[a]))
            for k in range(1, N_DEV):
                px = 1 - x if k & 4 else x
                py = 1 - y if k & 2 else y
                pc = 1 - c if k & 1 else c
                remote.append(pltpu.make_async_remote_copy(
                    src_ref=ins[a].at[_slot(px, py, pc)], dst_ref=outs[a].at[my_slot],
                    send_sem=send_sems.at[7 * a + k - 1], recv_sem=recv_sems.at[7 * a + k - 1],
                    device_id=(px, py, pc), device_id_type=MESH_ID))
        return local, remote

    def start(self, ins, outs, sems):
        local, remote = self._copies(ins, outs, sems)
        for cp in local + remote:
            cp.start()

    def finish(self, ins, outs, sems):
        local, remote = self._copies(ins, outs, sems)
        for cp in remote + local:
            cp.wait()


def _sum_slots(parts, name):
    _, R, C = parts.shape
    tr = _pick(R, (256, 128, 64, 32, 16, 8))

    def body(p_ref, o_ref):
        g = p_ref[0].astype(F32)
        for s in range(1, N_DEV):
            g = g + p_ref[s].astype(F32)
        o_ref[...] = g

    return pl.pallas_call(body, out_shape=jax.ShapeDtypeStruct((R, C), F32), grid=(R // tr,),
                          in_specs=[pl.BlockSpec((N_DEV, tr, C), lambda i: (0, i, 0))],
                          out_specs=pl.BlockSpec((tr, C), lambda i: (i, 0)), name=name,
                          compiler_params=_params("parallel"))(parts)


_hg_fwd_op, _hg_rev_op = _make_hg_scan(False), _make_hg_scan(True)
_ret_fwd_op, _ret_rev_op = _make_ret_scan(False), _make_ret_scan(True)
_rope_q, _rope_k = _make_rope(1.0, "rope_q"), _make_rope(RET_DK ** -0.5, "rope_k")
_hg_post = _make_gnorm(2, True, HG_D, False, 1.0, "hg_post")
_ret_post = _make_gnorm(2, True, RET_DV, True, 1.0, "ret_post")
_q_norm = _make_gnorm(1, False, DIL_HD, False, DIL_HD ** -0.5, "dil_qnorm")
_k_norm = _make_gnorm(1, False, DIL_HD, False, 1.0, "dil_knorm")
_att_ops = {dil: _make_attention(dil) for _, dil in DIL_GROUPS}


def _to_heads(t, d):
    S, W = t.shape
    return t.reshape(S, W // d, d).transpose(1, 0, 2)


def _dilated_mixer(parts, rel_bias, q_gain, k_gain):
    S = parts[0].shape[0]
    qg, kg = jnp.tile(q_gain, DIL_SLOTS), jnp.tile(k_gain, DIL_SLOTS)
    ii = np.arange(ATT_SB)[:, None]
    jj = np.arange(ATT_WIN)[None, :]
    outs, lses = [], []
    for g, (window, dil) in enumerate(DIL_GROUPS):
        assert window // (2 * dil) == ATT_HALO
        L = S // dil

        def to_res(t):
            return t.reshape(L, dil, DIL_SLOTS, DIL_HD).transpose(2, 1, 0, 3).reshape(DIL_SLOTS * dil, L, DIL_HD)

        def from_res(t):
            return t.reshape(DIL_SLOTS, dil, L, DIL_HD).transpose(0, 2, 1, 3).reshape(DIL_SLOTS * S, DIL_HD)

        onehot = (_t5_bucket((jj - ATT_HALO - ii) * dil)[:, :, None] == np.arange(REL_BUCKETS)).astype(np.float32)
        bias = jnp.einsum("ijb,bh->hij", onehot, rel_bias[:, g * DIL_SLOTS:(g + 1) * DIL_SLOTS],
                          precision=lax.Precision.HIGHEST)
        q = _q_norm(parts[3 * g], qg)
        k = _k_norm(parts[3 * g + 1], kg)
        o, lse = _att_ops[dil](to_res(q), to_res(k), to_res(parts[3 * g + 2]), bias)
        outs.append(from_res(o))
        lses.append(from_res(lse))
    merged = dil_merge(*outs, *lses)
    return merged.reshape(DIL_SLOTS, S, DIL_HD).transpose(1, 0, 2).reshape(S, DILW)


def _mixers(x, w_in, w_out, p):
    h = norm_matmul(x, p["norm_mix"], w_in)
    offs = np.cumsum(IN_SPLITS)[:-1].tolist()
    parts = jnp.split(h, offs, axis=-1)
    q, v = parts[0], parts[1]
    y_a = _hg_post(_hg_fwd_op(q, v, parts[2], p["lb_fwd"]), _hg_rev_op(q, v, parts[3], p["lb_bwd"]), p["hg_norm"], parts[4])
    qh = _to_heads(_rope_q(parts[5]), RET_DK)
    kh = _to_heads(_rope_k(parts[6]), RET_DK)
    y_b = _ret_post(_ret_fwd_op(qh, kh, parts[7]), _ret_rev_op(qh, kh, parts[7]), p["ret_norm"], parts[8])
    y_c = _dilated_mixer(parts[9:], p["rel_bias"], p["q_norm"], p["k_norm"])
    y = jnp.concatenate([y_a, y_b, y_c], axis=-1)
    return out_proj(y, w_out, x)


def _col_full(blocks):
    return blocks.transpose(1, 0, 2).reshape(blocks.shape[1], -1)


def _col_blocks(full):
    K = full.shape[0]
    return full.reshape(K, N_DEV, -1).transpose(1, 0, 2)


def _row_full(blocks):
    return blocks.reshape(-1, blocks.shape[2])


def _row_blocks(full):
    return full.reshape(N_DEV, -1, full.shape[1])


FULL_OF = (_col_full, _row_full, _col_full, _row_full)
BLOCKS_OF = (_col_blocks, _row_blocks, _col_blocks, _row_blocks)
SMALL_NAMES = ("norm_mix", "norm_mlp", "hg_lb_fwd", "hg_lb_bwd", "hg_norm", "ret_norm", "q_norm", "k_norm", "rel_bias")


def _forward(x, mats0, shards, small):
    depth = len(shards) + 1
    lb_f = jnp.cumsum(jax.nn.softmax(small["hg_lb_fwd"], axis=0), axis=0)
    lb_b = jnp.cumsum(jax.nn.softmax(small["hg_lb_bwd"], axis=0), axis=0)
    mats = mats0
    for l in range(depth):
        p = {k: small[k][l] for k in ("norm_mix", "norm_mlp", "hg_norm", "ret_norm", "q_norm", "k_norm")}
        p["lb_fwd"], p["lb_bwd"] = lb_f[l] - lb_f[0], lb_b[l] - lb_b[0]
        p["rel_bias"] = small["rel_bias"]
        w_in, w_out, w_up, w_down = mats
        x = _mixers(x, w_in, w_out, p)
        if l < depth - 1:
            x, *gathered = mlp_carry(x, p["norm_mlp"], w_up, w_down, *shards[l])
            mats = tuple(f(g) for f, g in zip(FULL_OF, gathered))
        else:
            x = mlp(x, p["norm_mlp"], w_up, w_down)
    return x


def kernel(x, w_in, w_out, w_up, w_down, norm_mix, norm_mlp, hg_lb_fwd, hg_lb_bwd, hg_norm, ret_norm, q_norm, k_norm, rel_bias, loss_target, m_w_in, m_w_out, m_w_up, m_w_down, m_norm_mix, m_norm_mlp, m_hg_lb_fwd, m_hg_lb_bwd, m_hg_norm, m_ret_norm, m_q_norm, m_k_norm, m_rel_bias, v_w_in, v_w_out, v_w_up, v_w_down, v_norm_mix, v_norm_mlp, v_hg_lb_fwd, v_hg_lb_bwd, v_hg_norm, v_ret_norm, v_q_norm, v_k_norm, v_rel_bias):
    depth = w_in.shape[0]
    big = (w_in, w_out, w_up, w_down)
    big_m = (m_w_in, m_w_out, m_w_up, m_w_down)
    big_v = (v_w_in, v_w_out, v_w_up, v_w_down)
    small = dict(zip(SMALL_NAMES, (norm_mix, norm_mlp, hg_lb_fwd, hg_lb_bwd, hg_norm, ret_norm, q_norm, k_norm, rel_bias)))
    small_m = (m_norm_mix, m_norm_mlp, m_hg_lb_fwd, m_hg_lb_bwd, m_hg_norm, m_ret_norm, m_q_norm, m_k_norm, m_rel_bias)
    small_v = (v_norm_mix, v_norm_mlp, v_hg_lb_fwd, v_hg_lb_bwd, v_hg_norm, v_ret_norm, v_q_norm, v_k_norm, v_rel_bias)

    gathered0 = _Gather([w[0].astype(WIRE) for w in big]).call("gather_weights")
    mats0 = tuple(f(g) for f, g in zip(FULL_OF, gathered0))
    shards = [tuple(w[l] for w in big) for l in range(1, depth)]

    xf, vjp = jax.vjp(_forward, x[0], mats0, shards, small)
    dy, part = _loss_grad(xf, loss_target[0])
    loss = lax.psum(0.5 / xf.shape[1] * jnp.sum(part), ("x", "y", "c"))
    dx, dmats0, dshards, dsmall = vjp(dy)

    landed = _Exchange([f(d) for f, d in zip(BLOCKS_OF, dmats0)]).call("exchange_grads")
    big_out = []
    for i in range(4):
        g = jnp.stack([_sum_slots(landed[i], "sum_grads0_%d" % i)] + [dshards[l][i] for l in range(depth - 1)])
        big_out.append((g,) + tuple(_adamw(g, big[i], big_m[i], big_v[i], "adamw_%d" % i)))

    flat = jnp.concatenate([dsmall[n].reshape(-1) for n in SMALL_NAMES])
    n_small = flat.shape[0]
    rows = -(-n_small // 1024) * 8
    pad = lambda t: jnp.pad(t, (0, rows * 128 - n_small)).reshape(1, rows, 128)
    (small_parts,) = _Gather([pad(flat)[0]]).call("gather_small_grads")
    cat = lambda ts: pad(jnp.concatenate([t.reshape(-1) for t in ts]))
    g_small = _sum_slots(small_parts, "sum_small_grads")[None]
    small_out = (g_small,) + tuple(_adamw(g_small, cat([small[n] for n in SMALL_NAMES]), cat(small_m), cat(small_v), "adamw_small"))

    def unpack(t):
        t = t.reshape(-1)
        out, off = [], 0
        for n in SMALL_NAMES:
            size = small[n].size
            out.append(t[off:off + size].reshape(small[n].shape))
            off += size
        return out

    res = [loss, dx[None]]
    for kind in range(4):
        res += [o[kind] for o in big_out] + unpack(small_out[kind])
    return tuple(res)
```

```python
import functools
import math

import numpy as np
import jax
import jax.numpy as jnp
from jax import lax
from jax.experimental import pallas as pl
from jax.experimental.pallas import tpu as pltpu

F32 = jnp.float32
MXU = jnp.bfloat16
WIRE = jnp.bfloat16
EPS = 1e-6
N_DEV = 8
VMEM_LIMIT = 48 * 1024 * 1024

HG_HEADS, HG_D = 6, 128
RET_HEADS, RET_DK, RET_DV = 6, 64, 128
DIL_SLOTS, DIL_HD = 4, 128
DIL_GROUPS = ((128, 1), (512, 4), (2048, 16))
HGW = HG_HEADS * HG_D
RETW = RET_HEADS * RET_DV
DILW = DIL_SLOTS * DIL_HD
IN_SPLITS = (HGW, HGW, HGW, HGW, HGW, RET_HEADS * RET_DK, RET_HEADS * RET_DK, RETW, RETW) + (DILW,) * 9
REL_BUCKETS, REL_MAX_DIST = 32, 1024
ROPE_BASE = 10000.0
ADAM_LR, ADAM_B1, ADAM_B2, ADAM_EPS, ADAM_WD, ADAM_STEP = 0.001, 0.9, 0.999, 1e-08, 0.01, 10

SCAN_C = 128
SCAN_HPS = 6
HG_SUB = 16
EXP_CLAMP = 60.0
ATT_SB, ATT_HALO = 128, 64
ATT_WIN = ATT_SB + 2 * ATT_HALO
MESH_ID = pl.DeviceIdType.MESH


def _pick(n, prefs):
    for p in prefs:
        if n % p == 0:
            return p
    return n


def _params(*sem):
    return pltpu.CompilerParams(dimension_semantics=sem, vmem_limit_bytes=VMEM_LIMIT)


def _dot(a, b):
    return lax.dot_general(a.astype(MXU), b.astype(MXU), (((1,), (0,)), ((), ())), preferred_element_type=F32)


def _dot_nt(a, b):
    return lax.dot_general(a.astype(MXU), b.astype(MXU), (((1,), (1,)), ((), ())), preferred_element_type=F32)


def _dot_tn(a, b):
    return lax.dot_general(a.astype(MXU), b.astype(MXU), (((0,), (0,)), ((), ())), preferred_element_type=F32)


def _dot01(sel, x):
    if MXU == F32:
        return _dot(sel, x)
    hi = x.astype(MXU)
    r1 = x - hi.astype(F32)
    mid = r1.astype(MXU)
    lo = (r1 - mid.astype(F32)).astype(MXU)
    return _dot(sel, hi) + _dot(sel, mid) + _dot(sel, lo)


def _mm(a, b, *, nt=False, ta=False, out_dtype=F32, res=None, u_in=None, emit_act=False, rider=None, name):
    M, K = a.shape[::-1] if ta else a.shape
    N = b.shape[0] if nt else b.shape[1]
    tm = _pick(M, (1024, 512, 256, 128))
    tn = _pick(N, (1024, 768, 512, 384, 256, 128))
    tk = _pick(K, (2048, 1536, 1024, 512, 256, 128))
    ni, nj, nk = M // tm, N // tn, K // tk
    n_ride = len(rider.arrays) if rider is not None else 0

    def body(*refs):
        it = iter(refs)
        a_ref, b_ref = next(it), next(it)
        res_ref = next(it) if res is not None else None
        u_ref = next(it) if u_in is not None else None
        ride_in = [next(it) for _ in range(n_ride)]
        o_ref = next(it)
        act_ref = next(it) if emit_act else None
        ride_out = [next(it) for _ in range(n_ride)]
        acc_ref = next(it)
        sems = list(it)
        i, j, k = pl.program_id(0), pl.program_id(1), pl.program_id(2)

        step = (i * nj + j) * nk + k
        if rider is not None:
            pl.when(step == 0)(functools.partial(rider.start, ride_in, ride_out, sems))
            pl.when(step == (ni // 2) * nj * nk)(functools.partial(rider.mid, ride_in, ride_out, sems))

        @pl.when(k == 0)
        def _():
            acc_ref[...] = jnp.zeros_like(acc_ref)

        acc_ref[...] += (_dot_tn if ta else _dot_nt if nt else _dot)(a_ref[...], b_ref[...])

        @pl.when(k == nk - 1)
        def _():
            r = acc_ref[...]
            if res_ref is not None:
                r = r + res_ref[...]
            if u_ref is not None:
                r = r * (2.0 * jnp.maximum(u_ref[...], 0.0))
            o_ref[...] = r.astype(o_ref.dtype)
            if act_ref is not None:
                t = jnp.maximum(r, 0.0)
                act_ref[...] = (t * t).astype(act_ref.dtype)

        if rider is not None:
            pl.when(step == ni * nj * nk - 1)(functools.partial(rider.finish, ride_in, ride_out, sems))

    mn = pl.BlockSpec((tm, tn), lambda i, j, k: (i, j))
    anyspec = pl.BlockSpec(memory_space=pl.ANY)
    in_specs = [pl.BlockSpec((tk, tm), lambda i, j, k: (k, i)) if ta else pl.BlockSpec((tm, tk), lambda i, j, k: (i, k)),
                pl.BlockSpec((tn, tk), lambda i, j, k: (j, k)) if nt else pl.BlockSpec((tk, tn), lambda i, j, k: (k, j))]
    args = [a, b]
    for extra in (res, u_in):
        if extra is not None:
            in_specs.append(mn)
            args.append(extra)
    out_shape = [jax.ShapeDtypeStruct((M, N), out_dtype)]
    out_specs = [mn]
    if emit_act:
        out_shape.append(jax.ShapeDtypeStruct((M, N), MXU))
        out_specs.append(mn)
    scratch = [pltpu.VMEM((tm, tn), F32)]
    if rider is not None:
        in_specs += [anyspec] * n_ride
        args += list(rider.arrays)
        out_shape += rider.out_shapes()
        out_specs += [anyspec] * n_ride
        scratch += rider.sem_shapes()
    sem = ("arbitrary",) * 3 if rider is not None else ("parallel", "parallel", "arbitrary")
    out = pl.pallas_call(
        body, out_shape=out_shape, grid=(ni, nj, nk), in_specs=in_specs, out_specs=out_specs,
        scratch_shapes=scratch, name=name, compiler_params=_params(*sem))(*args)
    return out if (emit_act or rider is not None) else out[0]


def _gnorm_stats(x, center):
    if center:
        x = x - jnp.mean(x, axis=-1, keepdims=True)
    r = lax.rsqrt(jnp.mean(x * x, axis=-1, keepdims=True) + EPS)
    return x * r, r


def _silu_parts(gt):
    sg = jax.nn.sigmoid(gt)
    return gt * sg, sg * (1.0 + gt * (1.0 - sg))


def _gnorm_fwd(xs, gain, gate, *, group, center, scale, out_dtype, name):
    S, W = xs[0].shape
    ts = _pick(S, (512, 256, 128))
    nx = len(xs)

    def body(*refs):
        x_refs, g_ref = refs[:nx], refs[nx]
        gate_ref = refs[nx + 1] if gate is not None else None
        o_ref = refs[-1]
        for gi in range(W // group):
            sl = slice(gi * group, (gi + 1) * group)
            x = x_refs[0][:, sl]
            for xr in x_refs[1:]:
                x = x + xr[:, sl]
            n, _ = _gnorm_stats(x, center)
            y = n * (g_ref[:, sl] * scale)
            if gate_ref is not None:
                y = y * _silu_parts(gate_ref[:, sl])[0]
            o_ref[:, sl] = y.astype(o_ref.dtype)

    row = pl.BlockSpec((ts, W), lambda i: (i, 0))
    vec = pl.BlockSpec((1, W), lambda i: (0, 0))
    args = list(xs) + [gain] + ([gate] if gate is not None else [])
    in_specs = [row] * nx + [vec] + ([row] if gate is not None else [])
    return pl.pallas_call(body, out_shape=jax.ShapeDtypeStruct((S, W), out_dtype), grid=(S // ts,),
                          in_specs=in_specs, out_specs=row, name=name, compiler_params=_params("parallel"))(*args)


def _gnorm_bwd(dy, xs, gain, gate, *, group, center, scale, name):
    S, W = xs[0].shape
    ts = _pick(S, (512, 256, 128))
    nx = len(xs)

    def body(*refs):
        dy_ref = refs[0]
        x_refs, g_ref = refs[1:1 + nx], refs[1 + nx]
        gate_ref = refs[2 + nx] if gate is not None else None
        outs = refs[(3 + nx if gate is not None else 2 + nx):]
        dx_ref, dg_ref = outs[0], outs[1]
        dgate_ref = outs[2] if gate is not None else None

        @pl.when(pl.program_id(0) == 0)
        def _():
            dg_ref[...] = jnp.zeros_like(dg_ref)

        for gi in range(W // group):
            sl = slice(gi * group, (gi + 1) * group)
            x = x_refs[0][:, sl]
            for xr in x_refs[1:]:
                x = x + xr[:, sl]
            n, r = _gnorm_stats(x, center)
            dyv = dy_ref[:, sl].astype(F32)
            g = g_ref[:, sl] * scale
            if gate_ref is not None:
                act, dact = _silu_parts(gate_ref[:, sl])
                dgate_ref[:, sl] = dyv * n * g * dact
                dyv = dyv * act
            dg_ref[:, sl] += jnp.sum(dyv * n, axis=0, keepdims=True) * scale
            dn = dyv * g
            t = dn - n * jnp.mean(dn * n, axis=-1, keepdims=True)
            if center:
                t = t - jnp.mean(dn, axis=-1, keepdims=True)
            dx_ref[:, sl] = r * t

    row = pl.BlockSpec((ts, W), lambda i: (i, 0))
    vec = pl.BlockSpec((1, W), lambda i: (0, 0))
    args = [dy] + list(xs) + [gain] + ([gate] if gate is not None else [])
    in_specs = [row] * (1 + nx) + [vec] + ([row] if gate is not None else [])
    out_shape = [jax.ShapeDtypeStruct((S, W), F32), jax.ShapeDtypeStruct((1, W), F32)]
    out_specs = [row, vec]
    if gate is not None:
        out_shape.append(jax.ShapeDtypeStruct((S, W), F32))
        out_specs.append(row)
    out = pl.pallas_call(body, out_shape=out_shape, grid=(S // ts,), in_specs=in_specs, out_specs=out_specs,
                         name=name, compiler_params=_params("arbitrary"))(*args)
    return out[0], out[1], (out[2] if gate is not None else None)


def _make_gnorm(nx, has_gate, group, center, scale, tag):
    kw = dict(group=group, center=center, scale=scale)

    @jax.custom_vjp
    def op(*args):
        return fwd(*args)[0]

    def fwd(*args):
        xs, gain = args[:nx], args[nx]
        gate = args[nx + 1] if has_gate else None
        y = _gnorm_fwd(xs, gain[None, :], gate, out_dtype=F32, name=tag + "_fwd", **kw)
        return y, args

    def bwd(args, dy):
        xs, gain = args[:nx], args[nx]
        gate = args[nx + 1] if has_gate else None
        dx, dg, dgate = _gnorm_bwd(dy, xs, gain[None, :], gate, name=tag + "_bwd", **kw)
        return (dx,) * nx + (dg[0],) + ((dgate,) if has_gate else ())

    op.defvjp(fwd, bwd)
    return op


def _ride(cls, arrays):
    return cls(list(arrays)) if len(arrays) else None


def _mm_ride(*args, rider, name, **kw):
    if rider is None:
        return _mm(*args, name=name, **kw), ()
    out = _mm(*args, rider=rider, name=name + "_ride", **kw)
    n = len(rider.arrays)
    main = out[:-n]
    return (main[0] if len(main) == 1 else tuple(main)), tuple(out[-n:])


def _landed_sums(landed, tag):
    return tuple(_sum_slots(l, "%s_sum%d" % (tag, i)) for i, l in enumerate(landed))


def _rms(x, g, name):
    return _gnorm_fwd([x], g[None, :], None, group=x.shape[1], center=False, scale=1.0, out_dtype=MXU, name=name)


def _rms_bwd(dxn, x, g, name):
    dx, dg, _ = _gnorm_bwd(dxn, [x], g[None, :], None, group=x.shape[1], center=False, scale=1.0, name=name)
    return dx, dg[0]


def _wire(shards):
    return [t.astype(WIRE) for t in shards]


@jax.custom_vjp
def norm_matmul(x, g, w, shards):
    return _norm_matmul_fwd(x, g, w, shards)[0]


def _norm_matmul_fwd(x, g, w, shards):
    xn = _rms(x, g, "rms_in")
    h, gathered = _mm_ride(xn, w, rider=_ride(_Gather, _wire(shards)), name="mm_in")
    return (h, gathered), (x, g, w, xn)


def _norm_matmul_bwd(saved, cts):
    x, g, w, xn = saved
    dh, d_gathered = cts
    dhb = dh.astype(MXU)
    dxn, l0 = _mm_ride(dhb, w, nt=True, rider=_ride(_Exchange, d_gathered[:1]), name="mm_in_dx")
    dw, l1 = _mm_ride(xn, dhb, ta=True, out_dtype=w.dtype, rider=_ride(_Exchange, d_gathered[1:]), name="mm_in_dw")
    dx, dg = _rms_bwd(dxn, x, g, "rms_in_bwd")
    return dx, dg, dw, _landed_sums(l0 + l1, "in")


norm_matmul.defvjp(_norm_matmul_fwd, _norm_matmul_bwd)


@jax.custom_vjp
def out_proj(y, w, x, shards):
    return _out_proj_fwd(y, w, x, shards)[0]


def _out_proj_fwd(y, w, x, shards):
    yb = y.astype(MXU)
    out, gathered = _mm_ride(yb, w, res=x, rider=_ride(_Gather, _wire(shards)), name="mm_out")
    return (out, gathered), (yb, w)


def _out_proj_bwd(saved, cts):
    yb, w = saved
    dout, d_gathered = cts
    db = dout.astype(MXU)
    dy, l0 = _mm_ride(db, w, nt=True, rider=_ride(_Exchange, d_gathered[:1]), name="mm_out_dy")
    dw, l1 = _mm_ride(yb, db, ta=True, out_dtype=w.dtype, rider=_ride(_Exchange, d_gathered[1:]), name="mm_out_dw")
    return dy, dw, dout, _landed_sums(l0 + l1, "out")


out_proj.defvjp(_out_proj_fwd, _out_proj_bwd)


@jax.custom_vjp
def mlp(x, g, w_up, w_down, shards):
    return _mlp_fwd(x, g, w_up, w_down, shards)[0]


def _mlp_fwd(x, g, w_up, w_down, shards):
    wire = _wire(shards)
    hm = _rms(x, g, "rms_mlp")
    (u, act), g0 = _mm_ride(hm, w_up, emit_act=True, rider=_ride(_Gather, wire[:1]), name="mm_up")
    out, g1 = _mm_ride(act, w_down, res=x, rider=_ride(_Gather, wire[1:]), name="mm_down")
    return (out, g0 + g1), (x, g, w_up, w_down, hm, u, act)


def _mlp_bwd(saved, cts):
    x, g, w_up, w_down, hm, u, act = saved
    dout, d_gathered = cts
    db = dout.astype(MXU)
    du, l0 = _mm_ride(db, w_down, nt=True, u_in=u, out_dtype=MXU, rider=_ride(_Exchange, d_gathered[:1]), name="mm_down_da")
    dw_down, l1 = _mm_ride(act, db, ta=True, out_dtype=w_down.dtype, rider=_ride(_Exchange, d_gathered[1:]), name="mm_down_dw")
    dhm = _mm(du, w_up, nt=True, name="mm_up_dx")
    dw_up = _mm(hm, du, ta=True, out_dtype=w_up.dtype, name="mm_up_dw")
    dx, dg = _rms_bwd(dhm, x, g, "rms_mlp_bwd")
    return dout + dx, dg, dw_up, dw_down, _landed_sums(l0 + l1, "mlp")


mlp.defvjp(_mlp_fwd, _mlp_bwd)


def _order(C, rev):
    i = np.arange(C)
    return (C - 1 - i) if rev else i


def _hg_constants(C, rev):
    p = _order(C, rev)
    pi, pj = p[:, None], p[None, :]
    tri = (pj <= pi).astype(np.float32)
    masks = [((pi // HG_SUB) == (pj // HG_SUB)) & (pj <= pi)]
    h = HG_SUB
    halves = []
    while h < C:
        masks.append(((pi // (2 * h)) == (pj // (2 * h))) & ((pi // h) % 2 == 1) & ((pj // h) % 2 == 0))
        halves.append(h)
        h *= 2
    return tri, np.stack(masks).astype(np.float32), halves


def _hg_tables(b_scr, C, h, rev):
    nb = C // h
    g_rows, e_rows = [], []
    for rb in range(nb):
        if rev:
            e = b_scr[pl.ds(rb * h, 1), :]
            g = b_scr[pl.ds((rb + 1) * h, 1), :] if rb < nb - 1 else None
        else:
            e = b_scr[pl.ds((rb + 1) * h - 1, 1), :]
            g = b_scr[pl.ds(rb * h - 1, 1), :] if rb >= 1 else None
        e_rows.append(jnp.broadcast_to(e, (h, HG_D)))
        g_rows.append(jnp.zeros((h, HG_D), F32) if g is None else jnp.broadcast_to(g, (h, HG_D)))
    return jnp.concatenate(g_rows, axis=0), jnp.concatenate(e_rows, axis=0)


def _hg_gates(z, lb):
    sg = jax.nn.sigmoid(z)
    f = lb + (1.0 - lb) * sg
    return sg, f, 1.0 - f, jnp.log(f)


def _hg_exponents(b, b_scr, C, halves, rev):
    g0, _ = _hg_tables(b_scr, C, HG_SUB, rev)
    p0 = b - g0
    out = [(p0, jnp.minimum(-p0, EXP_CLAMP))]
    for h in halves:
        g, e = _hg_tables(b_scr, C, h, rev)
        out.append((jnp.minimum(b - g, 0.0), jnp.minimum(e - b, 0.0)))
    return out


def _hg_scan_fwd(q, v, z, lb, rev, name):
    S = q.shape[0]
    C = SCAN_C
    nc = S // C
    tri, masks, halves = _hg_constants(C, rev)
    nlev = masks.shape[0]
    end_row = 0 if rev else C - 1
    vT = v.astype(MXU).T

    def body(q_ref, v_ref, vT_ref, z_ref, lb_ref, tri_ref, mask_ref, o_ref, st_ref, s_scr, b_scr):
        @pl.when(pl.program_id(1) == 0)
        def _():
            s_scr[...] = jnp.zeros_like(s_scr)

        for hh in range(SCAN_HPS):
            sl = slice(hh * HG_D, (hh + 1) * HG_D)
            bh = b_scr.at[hh]
            _, f, k, lf = _hg_gates(z_ref[:, sl], lb_ref[:, sl])
            b = _dot01(tri_ref[...], lf)
            bh[...] = b
            qv, vv = q_ref[:, sl], v_ref[:, sl]
            st = s_scr[hh]
            st_ref[hh, 0] = st
            a = jnp.zeros((C, C), F32)
            for lv, (eq, ek) in enumerate(_hg_exponents(b, bh, C, halves, rev)):
                a = a + mask_ref[lv] * _dot_nt(qv * jnp.exp(eq), k * jnp.exp(ek))
            bend = bh[pl.ds(end_row, 1), :]
            o_ref[:, sl] = _dot(a, vv) + _dot_nt(qv * jnp.exp(b), st)
            s_scr[hh] = st * jnp.exp(bend) + _dot(vT_ref[sl, :], k * jnp.exp(bend - b))

    cidx = (lambda c: nc - 1 - c) if rev else (lambda c: c)
    wid = SCAN_HPS * HG_D
    blk = pl.BlockSpec((C, wid), lambda h, c: (cidx(c), h))
    o, states = pl.pallas_call(
        body,
        out_shape=[jax.ShapeDtypeStruct((S, HGW), F32), jax.ShapeDtypeStruct((HG_HEADS, nc, HG_D, HG_D), F32)],
        grid=(HG_HEADS // SCAN_HPS, nc),
        in_specs=[blk, blk, pl.BlockSpec((wid, C), lambda h, c: (h, cidx(c))), blk,
                  pl.BlockSpec((1, wid), lambda h, c: (0, h)),
                  pl.BlockSpec((C, C), lambda h, c: (0, 0)), pl.BlockSpec((nlev, C, C), lambda h, c: (0, 0, 0))],
        out_specs=[blk, pl.BlockSpec((SCAN_HPS, 1, HG_D, HG_D), lambda h, c: (h, cidx(c), 0, 0))],
        scratch_shapes=[pltpu.VMEM((SCAN_HPS, HG_D, HG_D), F32), pltpu.VMEM((SCAN_HPS, C, HG_D), F32)],
        name=name, compiler_params=_params("parallel", "arbitrary"),
    )(q, v.astype(MXU), vT, z, lb, jnp.asarray(tri, MXU), jnp.asarray(masks))
    return o, states


def _hg_scan_bwd(q, v, z, lb, states, do, rev, name):
    S = q.shape[0]
    C = SCAN_C
    nc = S // C
    tri, masks, halves = _hg_constants(C, rev)
    nlev = masks.shape[0]
    end_row = 0 if rev else C - 1
    masks_t = np.ascontiguousarray(np.transpose(masks, (0, 2, 1)))
    dob = do.astype(MXU)

    def body(q_ref, v_ref, z_ref, lb_ref, do_ref, doT_ref, st_ref, tri_ref, triT_ref, mask_ref, maskT_ref,
             dq_ref, dv_ref, dz_ref, dlb_ref, dn_scr, b_scr):
        @pl.when(pl.program_id(1) == 0)
        def _():
            dn_scr[...] = jnp.zeros_like(dn_scr)
            dlb_ref[...] = jnp.zeros_like(dlb_ref)

        for hh in range(SCAN_HPS):
            sl = slice(hh * HG_D, (hh + 1) * HG_D)
            bh = b_scr.at[hh]
            lb_v = lb_ref[:, sl]
            sg, f, k, lf = _hg_gates(z_ref[:, sl], lb_v)
            b = _dot01(tri_ref[...], lf)
            bh[...] = b
            qv, vv, dov = q_ref[:, sl], v_ref[:, sl], do_ref[:, sl]
            st, dn = st_ref[hh, 0], dn_scr[hh]
            da = _dot_nt(dov, vv)
            da_t = da.T
            a = jnp.zeros((C, C), F32)
            dq = jnp.zeros((C, HG_D), F32)
            dk = jnp.zeros((C, HG_D), F32)
            for lv, (eq, ek) in enumerate(_hg_exponents(b, bh, C, halves, rev)):
                xq, xk = jnp.exp(eq), jnp.exp(ek)
                qs, ks = qv * xq, k * xk
                a = a + mask_ref[lv] * _dot_nt(qs, ks)
                dq = dq + _dot(mask_ref[lv] * da, ks) * xq
                dk = dk + _dot(maskT_ref[lv] * da_t, qs) * xk
            bend = bh[pl.ds(end_row, 1), :]
            xb, xe, xend = jnp.exp(b), jnp.exp(bend - b), jnp.exp(bend)
            dq = dq + _dot(dov, st) * xb
            dk_state = _dot(vv, dn) * xe
            dk = dk + dk_state
            dv_ref[:, sl] = _dot(a.T, dov) + _dot_nt(k * xe, dn)
            dn_scr[hh] = dn * xend + _dot(doT_ref[sl, :], qv * xb)
            extra = jnp.sum(k * dk_state, axis=0, keepdims=True) + xend * jnp.sum(st * dn, axis=0, keepdims=True)
            rows = lax.broadcasted_iota(jnp.int32, (C, HG_D), 0)
            db = qv * dq - k * dk + jnp.where(rows == end_row, extra, 0.0)
            df = _dot01(triT_ref[...], db) / f - dk
            dq_ref[:, sl] = dq
            dz_ref[:, sl] = df * (1.0 - lb_v) * sg * (1.0 - sg)
            dlb_ref[:, sl] += jnp.sum(df * (1.0 - sg), axis=0, keepdims=True)

    cidx = (lambda c: c) if rev else (lambda c: nc - 1 - c)
    wid = SCAN_HPS * HG_D
    blk = pl.BlockSpec((C, wid), lambda h, c: (cidx(c), h))
    vec = pl.BlockSpec((1, wid), lambda h, c: (0, h))
    cc = pl.BlockSpec((C, C), lambda h, c: (0, 0))
    lcc = pl.BlockSpec((nlev, C, C), lambda h, c: (0, 0, 0))
    sd = jax.ShapeDtypeStruct((S, HGW), F32)
    return pl.pallas_call(
        body, out_shape=[sd, sd, sd, jax.ShapeDtypeStruct((1, HGW), F32)], grid=(HG_HEADS // SCAN_HPS, nc),
        in_specs=[blk, blk, blk, vec, blk, pl.BlockSpec((wid, C), lambda h, c: (h, cidx(c))),
                  pl.BlockSpec((SCAN_HPS, 1, HG_D, HG_D), lambda h, c: (h, cidx(c), 0, 0)), cc, cc, lcc, lcc],
        out_specs=[blk, blk, blk, vec],
        scratch_shapes=[pltpu.VMEM((SCAN_HPS, HG_D, HG_D), F32), pltpu.VMEM((SCAN_HPS, C, HG_D), F32)],
        name=name, compiler_params=_params("parallel", "arbitrary"),
    )(q, v.astype(MXU), z, lb, dob, dob.T, states, jnp.asarray(tri, MXU), jnp.asarray(tri.T, MXU),
      jnp.asarray(masks), jnp.asarray(masks_t))


def _make_hg_scan(rev):
    tag = "hg_rev" if rev else "hg_fwd"

    @jax.custom_vjp
    def op(q, v, z, lb):
        return fwd(q, v, z, lb)[0]

    def fwd(q, v, z, lb):
        o, states = _hg_scan_fwd(q, v, z, lb[None, :], rev, tag)
        return o, (q, v, z, lb, states)

    def bwd(saved, do):
        q, v, z, lb, states = saved
        dq, dv, dz, dlb = _hg_scan_bwd(q, v, z, lb[None, :], states, do, rev, tag + "_bwd")
        return dq, dv, dz, dlb[0]

    op.defvjp(fwd, bwd)
    return op


def _ret_constants(C, rev):
    hidx = np.arange(RET_HEADS, dtype=np.float64)
    lg = np.log1p(-np.exp2(-5.0 - hidx))
    if rev:
        lg = lg[::-1]
    p = _order(C, rev).astype(np.float64)
    rel = p[:, None] - p[None, :]
    dmat = np.where(rel >= 0, np.exp(lg[:, None, None] * np.maximum(rel, 0.0)), 0.0)
    xi = np.exp(lg[:, None] * (p[None, :] + 1.0))
    zeta = np.exp(lg[:, None] * (C - 1.0 - p[None, :]))
    gc = np.exp(lg * C)
    bc = lambda t: np.ascontiguousarray(np.broadcast_to(t[:, :, None], (RET_HEADS, C, RET_DK))).astype(np.float32)
    gcb = np.ascontiguousarray(np.broadcast_to(gc[:, None, None], (RET_HEADS, 1, RET_DK))).astype(np.float32)
    return dmat.astype(np.float32), bc(xi), bc(zeta), gcb


def _ret_scan_fwd(qh, kh, v, rev, name):
    S = v.shape[0]
    C = SCAN_C
    nc = S // C
    dmat, xi, zeta, gc = _ret_constants(C, rev)
    vb = v.astype(MXU)

    def body(q_ref, k_ref, v_ref, vT_ref, d_ref, xi_ref, zeta_ref, gc_ref, o_ref, st_ref, s_scr):
        @pl.when(pl.program_id(1) == 0)
        def _():
            s_scr[...] = jnp.zeros_like(s_scr)

        for hh in range(SCAN_HPS):
            sl = slice(hh * RET_DV, (hh + 1) * RET_DV)
            qv, kv = q_ref[hh], k_ref[hh]
            st = s_scr[hh]
            st_ref[hh, 0] = st
            sc = _dot_nt(qv, kv) * d_ref[hh]
            o_ref[:, sl] = _dot(sc, v_ref[:, sl]) + _dot_nt(qv * xi_ref[hh], st)
            s_scr[hh] = st * gc_ref[hh] + _dot(vT_ref[sl, :], kv * zeta_ref[hh])

    cidx = (lambda c: nc - 1 - c) if rev else (lambda c: c)
    hk = pl.BlockSpec((SCAN_HPS, C, RET_DK), lambda h, c: (h, cidx(c), 0))
    vblk = pl.BlockSpec((C, SCAN_HPS * RET_DV), lambda h, c: (cidx(c), h))
    tab = pl.BlockSpec((SCAN_HPS, C, RET_DK), lambda h, c: (h, 0, 0))
    return pl.pallas_call(
        body,
        out_shape=[jax.ShapeDtypeStruct((S, RETW), F32), jax.ShapeDtypeStruct((RET_HEADS, nc, RET_DV, RET_DK), F32)],
        grid=(RET_HEADS // SCAN_HPS, nc),
        in_specs=[hk, hk, vblk, pl.BlockSpec((SCAN_HPS * RET_DV, C), lambda h, c: (h, cidx(c))),
                  pl.BlockSpec((SCAN_HPS, C, C), lambda h, c: (h, 0, 0)), tab, tab,
                  pl.BlockSpec((SCAN_HPS, 1, RET_DK), lambda h, c: (h, 0, 0))],
        out_specs=[vblk, pl.BlockSpec((SCAN_HPS, 1, RET_DV, RET_DK), lambda h, c: (h, cidx(c), 0, 0))],
        scratch_shapes=[pltpu.VMEM((SCAN_HPS, RET_DV, RET_DK), F32)],
        name=name, compiler_params=_params("parallel", "arbitrary"),
    )(qh, kh, vb, vb.T, jnp.asarray(dmat), jnp.asarray(xi), jnp.asarray(zeta), jnp.asarray(gc))


def _ret_scan_bwd(qh, kh, v, states, do, rev, name):
    S = v.shape[0]
    C = SCAN_C
    nc = S // C
    dmat, xi, zeta, gc = _ret_constants(C, rev)
    vb, dob = v.astype(MXU), do.astype(MXU)

    def body(q_ref, k_ref, v_ref, do_ref, doT_ref, st_ref, d_ref, xi_ref, zeta_ref, gc_ref,
             dq_ref, dk_ref, dv_ref, dn_scr):
        @pl.when(pl.program_id(1) == 0)
        def _():
            dn_scr[...] = jnp.zeros_like(dn_scr)

        for hh in range(SCAN_HPS):
            sl = slice(hh * RET_DV, (hh + 1) * RET_DV)
            qv, kv, vv, dov = q_ref[hh], k_ref[hh], v_ref[:, sl], do_ref[:, sl]
            st, dn = st_ref[hh, 0], dn_scr[hh]
            dm = d_ref[hh]
            sc = _dot_nt(qv, kv) * dm
            dsc = _dot_nt(dov, vv) * dm
            kz = kv * zeta_ref[hh]
            dq_ref[hh] = _dot(dsc, kv) + _dot(dov, st) * xi_ref[hh]
            dk_ref[hh] = _dot(dsc.T, qv) + _dot(vv, dn) * zeta_ref[hh]
            dv_ref[:, sl] = _dot(sc.T, dov) + _dot_nt(kz, dn)
            dn_scr[hh] = dn * gc_ref[hh] + _dot(doT_ref[sl, :], qv * xi_ref[hh])

    cidx = (lambda c: c) if rev else (lambda c: nc - 1 - c)
    hk = pl.BlockSpec((SCAN_HPS, C, RET_DK), lambda h, c: (h, cidx(c), 0))
    vblk = pl.BlockSpec((C, SCAN_HPS * RET_DV), lambda h, c: (cidx(c), h))
    tab = pl.BlockSpec((SCAN_HPS, C, RET_DK), lambda h, c: (h, 0, 0))
    hs = jax.ShapeDtypeStruct(qh.shape, F32)
    return pl.pallas_call(
        body, out_shape=[hs, hs, jax.ShapeDtypeStruct((S, RETW), F32)], grid=(RET_HEADS // SCAN_HPS, nc),
        in_specs=[hk, hk, vblk, vblk, pl.BlockSpec((SCAN_HPS * RET_DV, C), lambda h, c: (h, cidx(c))),
                  pl.BlockSpec((SCAN_HPS, 1, RET_DV, RET_DK), lambda h, c: (h, cidx(c), 0, 0)),
                  pl.BlockSpec((SCAN_HPS, C, C), lambda h, c: (h, 0, 0)), tab, tab,
                  pl.BlockSpec((SCAN_HPS, 1, RET_DK), lambda h, c: (h, 0, 0))],
        out_specs=[hk, hk, vblk],
        scratch_shapes=[pltpu.VMEM((SCAN_HPS, RET_DV, RET_DK), F32)],
        name=name, compiler_params=_params("parallel", "arbitrary"),
    )(qh, kh, vb, dob, dob.T, states, jnp.asarray(dmat), jnp.asarray(xi), jnp.asarray(zeta), jnp.asarray(gc))


def _make_ret_scan(rev):
    tag = "ret_rev" if rev else "ret_fwd"

    @jax.custom_vjp
    def op(qh, kh, v):
        return fwd(qh, kh, v)[0]

    def fwd(qh, kh, v):
        o, states = _ret_scan_fwd(qh, kh, v, rev, tag)
        return o, (qh, kh, v, states)

    def bwd(saved, do):
        qh, kh, v, states = saved
        return tuple(_ret_scan_bwd(qh, kh, v, states, do, rev, tag + "_bwd"))

    op.defvjp(fwd, bwd)
    return op


def _rope_tables(S, scale):
    half = RET_DK // 2
    inv = ROPE_BASE ** (-np.arange(half, dtype=np.float32) / half)
    ang = np.arange(S, dtype=np.float32)[:, None] * inv[None, :]
    cos, sin = np.cos(ang), np.sin(ang)
    cos_t = np.tile(np.concatenate([cos, cos], axis=1), (1, RET_HEADS)) * scale
    sin_t = np.tile(np.concatenate([-sin, sin], axis=1), (1, RET_HEADS)) * scale
    return cos_t.astype(np.float32), sin_t.astype(np.float32)


def _rope_apply(t, cos_t, sin_t, name):
    S, W = t.shape
    ts = _pick(S, (512, 256, 128))
    half = RET_DK // 2

    def body(t_ref, c_ref, s_ref, o_ref):
        tv = t_ref[...]
        lane = lax.broadcasted_iota(jnp.int32, tv.shape, 1)
        partner = jnp.where(lane % RET_DK < half, pltpu.roll(tv, W - half, 1), pltpu.roll(tv, half, 1))
        o_ref[...] = tv * c_ref[...] + partner * s_ref[...]

    row = pl.BlockSpec((ts, W), lambda i: (i, 0))
    return pl.pallas_call(body, out_shape=jax.ShapeDtypeStruct((S, W), F32), grid=(S // ts,),
                          in_specs=[row, row, row], out_specs=row, name=name,
                          compiler_params=_params("parallel"))(t, jnp.asarray(cos_t), jnp.asarray(sin_t))


def _make_rope(scale, tag):
    def apply(t):
        cos_t, sin_t = _rope_tables(t.shape[0], scale)
        return _rope_apply(t, cos_t, sin_t, tag)

    op = jax.custom_vjp(apply)

    def fwd(t):
        return apply(t), None

    def bwd(_, dout):
        cos_t, sin_t = _rope_tables(dout.shape[0], scale)
        return (_rope_apply(dout, cos_t, -sin_t, tag + "_bwd"),)

    op.defvjp(fwd, bwd)
    return op


def _att_geometry(L):
    tq = _pick(L, (512, 256, 128))
    return tq, L // tq, tq // ATT_HALO


def _att_specs(tq, per):
    main = pl.BlockSpec((1, tq, DIL_HD), lambda b, n: (b, n, 0))
    prev = pl.BlockSpec((1, ATT_HALO, DIL_HD), lambda b, n: (b, jnp.maximum(n * per - 1, 0), 0))
    return main, prev


def _att_valid(n, u, tq, L):
    ii = lax.broadcasted_iota(jnp.int32, (ATT_SB, ATT_WIN), 0)
    jj = lax.broadcasted_iota(jnp.int32, (ATT_SB, ATT_WIN), 1)
    key = n * tq + u * ATT_SB - ATT_HALO + jj
    return (jnp.abs(jj - ATT_HALO - ii) <= ATT_HALO) & (key >= 0) & (key < L)


def _att_fill(buf, prev_ref, main_ref, next_ref, tq):
    buf[pl.ds(0, ATT_HALO), :] = prev_ref[0]
    buf[pl.ds(ATT_HALO, tq), :] = main_ref[0]
    buf[pl.ds(ATT_HALO + tq, ATT_HALO), :] = next_ref[0]


def _att_fwd(q, k, v, bias, dil, name):
    B, L, _ = q.shape
    tq, nt, per = _att_geometry(L)
    last = L // ATT_HALO - 1

    def body(q_ref, kp_ref, k_ref, kn_ref, vp_ref, v_ref, vn_ref, bias_ref, o_ref, lse_ref, kbuf, vbuf):
        n = pl.program_id(1)
        _att_fill(kbuf, kp_ref, k_ref, kn_ref, tq)
        _att_fill(vbuf, vp_ref, v_ref, vn_ref, tq)
        for u in range(tq // ATT_SB):
            rows = pl.ds(u * ATT_SB, ATT_SB)
            win = pl.ds(u * ATT_SB, ATT_WIN)
            s = _dot_nt(q_ref[0, rows, :], kbuf[win, :]) + bias_ref[0]
            s = jnp.where(_att_valid(n, u, tq, L), s, -1e30)
            m = jnp.max(s, axis=-1, keepdims=True)
            p = jnp.exp(s - m)
            den = jnp.sum(p, axis=-1, keepdims=True)
            o_ref[0, rows, :] = _dot(p, vbuf[win, :]) / den
            lse_ref[0, rows, :] = jnp.broadcast_to(m + jnp.log(den), (ATT_SB, DIL_HD))

    main, prev = _att_specs(tq, per)
    nxt = pl.BlockSpec((1, ATT_HALO, DIL_HD), lambda b, n: (b, jnp.minimum((n + 1) * per, last), 0))
    sd = jax.ShapeDtypeStruct((B, L, DIL_HD), F32)
    return pl.pallas_call(
        body, out_shape=[sd, sd], grid=(B, nt),
        in_specs=[main, prev, main, nxt, prev, main, nxt,
                  pl.BlockSpec((1, ATT_SB, ATT_WIN), lambda b, n: (b // dil, 0, 0))],
        out_specs=[main, main],
        scratch_shapes=[pltpu.VMEM((tq + 2 * ATT_HALO, DIL_HD), q.dtype), pltpu.VMEM((tq + 2 * ATT_HALO, DIL_HD), q.dtype)],
        name=name, compiler_params=_params("parallel", "arbitrary"),
    )(q, k, k, k, v, v, v, bias)


def _att_bwd(q, k, v, bias, o, lse, do, dlse, dil, name):
    B, L, _ = q.shape
    tq, nt, per = _att_geometry(L)
    last = L // ATT_HALO - 1

    def body(q_ref, kp_ref, k_ref, kn_ref, vp_ref, v_ref, vn_ref, bias_ref, o_ref, lse_ref, do_ref, dlse_ref,
             dq_ref, dk_ref, dkp_ref, dkn_ref, dv_ref, dvp_ref, dvn_ref, dbias_ref, kbuf, vbuf, dkbuf, dvbuf):
        b, n = pl.program_id(0), pl.program_id(1)

        @pl.when((b % dil == 0) & (n == 0))
        def _():
            dbias_ref[...] = jnp.zeros_like(dbias_ref)

        _att_fill(kbuf, kp_ref, k_ref, kn_ref, tq)
        _att_fill(vbuf, vp_ref, v_ref, vn_ref, tq)
        dkbuf[...] = jnp.zeros_like(dkbuf)
        dvbuf[...] = jnp.zeros_like(dvbuf)
        for u in range(tq // ATT_SB):
            rows = pl.ds(u * ATT_SB, ATT_SB)
            win = pl.ds(u * ATT_SB, ATT_WIN)
            qu, kw, vw = q_ref[0, rows, :], kbuf[win, :], vbuf[win, :]
            dou = do_ref[0, rows, :]
            s = _dot_nt(qu, kw) + bias_ref[0]
            lse_u = jnp.max(lse_ref[0, rows, :], axis=-1, keepdims=True)
            p = jnp.where(_att_valid(n, u, tq, L), jnp.exp(s - lse_u), 0.0)
            corr = jnp.sum(dlse_ref[0, rows, :] - dou * o_ref[0, rows, :], axis=-1, keepdims=True)
            ds = p * (_dot_nt(dou, vw) + corr)
            dq_ref[0, rows, :] = _dot(ds, kw)
            dkbuf[win, :] += _dot(ds.T, qu)
            dvbuf[win, :] += _dot(p.T, dou)
            dbias_ref[0] += ds
        for full, lo, hi in ((dkbuf, dkp_ref, dkn_ref), (dvbuf, dvp_ref, dvn_ref)):
            lo[0, 0] = full[pl.ds(0, ATT_HALO), :]
            hi[0, 0] = full[pl.ds(ATT_HALO + tq, ATT_HALO), :]
        dk_ref[0] = dkbuf[pl.ds(ATT_HALO, tq), :]
        dv_ref[0] = dvbuf[pl.ds(ATT_HALO, tq), :]

    main, prev = _att_specs(tq, per)
    nxt = pl.BlockSpec((1, ATT_HALO, DIL_HD), lambda b, n: (b, jnp.minimum((n + 1) * per, last), 0))
    halo = pl.BlockSpec((1, 1, ATT_HALO, DIL_HD), lambda b, n: (b, n, 0, 0))
    bias_spec = pl.BlockSpec((1, ATT_SB, ATT_WIN), lambda b, n: (b // dil, 0, 0))
    sd = jax.ShapeDtypeStruct((B, L, DIL_HD), F32)
    hd = jax.ShapeDtypeStruct((B, nt, ATT_HALO, DIL_HD), F32)
    width = tq + 2 * ATT_HALO
    dq, dk, dkp, dkn, dv, dvp, dvn, dbias = pl.pallas_call(
        body, out_shape=[sd, sd, hd, hd, sd, hd, hd, jax.ShapeDtypeStruct(bias.shape, F32)], grid=(B, nt),
        in_specs=[main, prev, main, nxt, prev, main, nxt, bias_spec, main, main, main, main],
        out_specs=[main, main, halo, halo, main, halo, halo, bias_spec],
        scratch_shapes=[pltpu.VMEM((width, DIL_HD), q.dtype), pltpu.VMEM((width, DIL_HD), q.dtype),
                        pltpu.VMEM((width, DIL_HD), F32), pltpu.VMEM((width, DIL_HD), F32)],
        name=name, compiler_params=_params("arbitrary", "arbitrary"),
    )(q, k, k, k, v, v, v, bias, o, lse, do, dlse)

    def fold(mainv, lo, hi):
        t = mainv.reshape(B, nt, tq, DIL_HD)
        if nt > 1:
            t = t.at[:, :-1, tq - ATT_HALO:, :].add(lo[:, 1:])
            t = t.at[:, 1:, :ATT_HALO, :].add(hi[:, :-1])
        return t.reshape(B, L, DIL_HD)

    return dq, fold(dk, dkp, dkn), fold(dv, dvp, dvn), dbias


def _make_attention(dil):
    tag = "att_d%d" % dil

    @jax.custom_vjp
    def op(q, k, v, bias):
        return fwd(q, k, v, bias)[0]

    def fwd(q, k, v, bias):
        qb, kb, vb = q.astype(MXU), k.astype(MXU), v.astype(MXU)
        o, lse = _att_fwd(qb, kb, vb, bias, dil, tag)
        return (o, lse), (qb, kb, vb, bias, o, lse)

    def bwd(saved, cts):
        qb, kb, vb, bias, o, lse = saved
        do, dlse = cts
        return tuple(_att_bwd(qb, kb, vb, bias, o, lse, do, dlse, dil, tag + "_bwd"))

    op.defvjp(fwd, bwd)
    return op


def _merge_weights(l0, l1, l2):
    m = jnp.maximum(jnp.maximum(l0, l1), l2)
    e0, e1, e2 = jnp.exp(l0 - m), jnp.exp(l1 - m), jnp.exp(l2 - m)
    inv = 1.0 / (e0 + e1 + e2)
    return e0 * inv, e1 * inv, e2 * inv


def _merge_call(body, n_in, n_out, shape, name):
    R, W = shape
    ts = _pick(R, (1024, 512, 256, 128))
    row = pl.BlockSpec((ts, W), lambda i: (i, 0))
    sd = jax.ShapeDtypeStruct(shape, F32)
    return pl.pallas_call(body, out_shape=[sd] * n_out, grid=(R // ts,), in_specs=[row] * n_in,
                          out_specs=[row] * n_out, name=name, compiler_params=_params("parallel"))


@jax.custom_vjp
def dil_merge(o0, o1, o2, l0, l1, l2):
    return _dil_merge_fwd(o0, o1, o2, l0, l1, l2)[0]


def _dil_merge_fwd(*args):
    def body(o0, o1, o2, l0, l1, l2, out):
        w0, w1, w2 = _merge_weights(l0[...], l1[...], l2[...])
        out[...] = w0 * o0[...] + w1 * o1[...] + w2 * o2[...]

    return _merge_call(body, 6, 1, args[0].shape, "dil_merge")(*args)[0], args


def _dil_merge_bwd(args, dout):
    def body(o0, o1, o2, l0, l1, l2, d, do0, do1, do2, dl0, dl1, dl2):
        ws = _merge_weights(l0[...], l1[...], l2[...])
        dv = d[...]
        dws = [dv * o[...] for o in (o0, o1, o2)]
        mean = ws[0] * dws[0] + ws[1] * dws[1] + ws[2] * dws[2]
        for w, dw, do_ref, dl_ref in zip(ws, dws, (do0, do1, do2), (dl0, dl1, dl2)):
            do_ref[...] = w * dv
            dl_ref[...] = w * (dw - mean)

    return tuple(_merge_call(body, 7, 6, args[0].shape, "dil_merge_bwd")(*args, dout))


dil_merge.defvjp(_dil_merge_fwd, _dil_merge_bwd)


def _t5_bucket(rel):
    nb = REL_BUCKETS // 2
    max_exact = nb // 2
    sign_off = np.where(rel > 0, nb, 0)
    n = np.abs(rel)
    nf = np.maximum(n, 1).astype(np.float32)
    large = max_exact + (np.log(nf / np.float32(max_exact)) / np.float32(math.log(REL_MAX_DIST / max_exact))
                         * np.float32(nb - max_exact)).astype(np.int32)
    large = np.minimum(large, nb - 1)
    return sign_off + np.where(n < max_exact, n, large)


def _loss_grad(xf, target):
    S, D = xf.shape
    ts = _pick(S, (512, 256, 128))

    def body(x_ref, t_ref, dy_ref, part_ref):
        @pl.when(pl.program_id(0) == 0)
        def _():
            part_ref[...] = jnp.zeros_like(part_ref)

        err = x_ref[...] - t_ref[...]
        dy_ref[...] = err * (1.0 / D)
        part_ref[...] += jnp.sum(err * err, axis=0, keepdims=True)

    row = pl.BlockSpec((ts, D), lambda i: (i, 0))
    return pl.pallas_call(body, out_shape=[jax.ShapeDtypeStruct((S, D), F32), jax.ShapeDtypeStruct((1, D), F32)],
                          grid=(S // ts,), in_specs=[row, row], out_specs=[row, pl.BlockSpec((1, D), lambda i: (0, 0))],
                          name="loss_grad", compiler_params=_params("arbitrary"))(xf, target)


def _adamw_math(g, w, m, v):
    m = ADAM_B1 * m + (1.0 - ADAM_B1) * g
    v = ADAM_B2 * v + (1.0 - ADAM_B2) * (g * g)
    m_hat = m / (1.0 - ADAM_B1 ** ADAM_STEP)
    v_hat = v / (1.0 - ADAM_B2 ** ADAM_STEP)
    delta = -ADAM_LR * (m_hat / (jnp.sqrt(v_hat) + ADAM_EPS) + ADAM_WD * w)
    return delta, m, v


def _adamw(g, w, m, v, name):
    Lw, R, C = w.shape
    tr = _pick(R, (256, 128, 64, 32, 16, 8))

    def body(g_ref, w_ref, m_ref, v_ref, d_ref, nm_ref, nv_ref):
        d_ref[0], nm_ref[0], nv_ref[0] = _adamw_math(g_ref[0], w_ref[0], m_ref[0], v_ref[0])

    blk = pl.BlockSpec((1, tr, C), lambda l, i: (l, i, 0))
    sd = jax.ShapeDtypeStruct(w.shape, F32)
    return pl.pallas_call(body, out_shape=[sd] * 3, grid=(Lw, R // tr), in_specs=[blk] * 4, out_specs=[blk] * 3,
                          name=name, compiler_params=_params("parallel", "parallel"))(g, w, m, v)


def _mesh_pos():
    return lax.axis_index("x"), lax.axis_index("y"), lax.axis_index("c")


def _slot(x, y, c):
    return 4 * x + 2 * y + c


class _Comm:
    def __init__(self, arrays):
        self.arrays = list(arrays)

    def sem_shapes(self):
        na = len(self.arrays)
        return [pltpu.SemaphoreType.DMA((7 * na,)), pltpu.SemaphoreType.DMA((7 * na,)), pltpu.SemaphoreType.DMA((na,))]

    def mid(self, ins, outs, sems):
        pass

    def call(self, name):
        na = len(self.arrays)

        def body(*refs):
            ins, outs, sems = refs[:na], refs[na:2 * na], refs[2 * na:]
            self.start(ins, outs, sems)
            self.mid(ins, outs, sems)
            self.finish(ins, outs, sems)

        anyspec = pl.BlockSpec(memory_space=pl.ANY)
        return pl.pallas_call(body, out_shape=self.out_shapes(), in_specs=[anyspec] * na, out_specs=[anyspec] * na,
                              scratch_shapes=self.sem_shapes(), name=name)(*self.arrays)


class _Gather(_Comm):
    def out_shapes(self):
        return [jax.ShapeDtypeStruct((N_DEV,) + b.shape, b.dtype) for b in self.arrays]

    def _copies(self, ins, outs, sems):
        send_sems, recv_sems, local_sems = sems
        x, y, c = _mesh_pos()
        me, sibling = (x, y, c), (x, y, 1 - c)
        chips = [(1 - x, y), (x, 1 - y), (1 - x, 1 - y)]
        per = []
        for a in range(len(self.arrays)):
            def copy(k, block, to, src=None, a=a):
                dst = outs[a].at[_slot(*block)]
                return pltpu.make_async_remote_copy(
                    src_ref=dst if src is None else src, dst_ref=dst,
                    send_sem=send_sems.at[7 * a + k], recv_sem=recv_sems.at[7 * a + k],
                    device_id=to, device_id_type=MESH_ID)

            per.append(dict(
                mine=pltpu.make_async_copy(ins[a], outs[a].at[_slot(*me)], local_sems.at[a]),
                first=[copy(0, me, sibling, src=ins[a])] + [copy(1 + j, me, (*ch, c), src=ins[a]) for j, ch in enumerate(chips)],
                passed=[copy(4 + j, (*ch, c), sibling) for j, ch in enumerate(chips)],
                over_ici=[copy(1 + j, (*ch, c), me) for j, ch in enumerate(chips)],
                from_sibling=[copy(0, sibling, me)] + [copy(4 + j, (*ch, 1 - c), me) for j, ch in enumerate(chips)]))
        return per

    def start(self, ins, outs, sems):
        for p in self._copies(ins, outs, sems):
            p["mine"].start()
            for cp in p["first"]:
                cp.start()

    def mid(self, ins, outs, sems):
        for p in self._copies(ins, outs, sems):
            for arrived, onward in zip(p["over_ici"], p["passed"]):
                arrived.wait_recv()
                onward.start()

    def finish(self, ins, outs, sems):
        for p in self._copies(ins, outs, sems):
            for cp in p["from_sibling"]:
                cp.wait_recv()
            for cp in p["first"] + p["passed"]:
                cp.wait_send()
            p["mine"].wait()


class _Exchange(_Comm):
    def out_shapes(self):
        return [jax.ShapeDtypeStruct(f.shape, f.dtype) for f in self.arrays]

    def _copies(self, ins, outs, sems):
        send_sems, recv_sems, local_sems = sems
        x, y, c = _mesh_pos()
        my_slot = _slot(x, y, c)
        local, remote = [], []
        for a in range(len(self.arrays)):
            local.append(pltpu.make_async_copy(ins[a].at[my_slot], outs[a].at[my_slot], local_sems.at[a]))
            for k in range(1, N_DEV):
                px = 1 - x if k & 4 else x
                py = 1 - y if k & 2 else y
                pc = 1 - c if k & 1 else c
                remote.append(pltpu.make_async_remote_copy(
                    src_ref=ins[a].at[_slot(px, py, pc)], dst_ref=outs[a].at[my_slot],
                    send_sem=send_sems.at[7 * a + k - 1], recv_sem=recv_sems.at[7 * a + k - 1],
                    device_id=(px, py, pc), device_id_type=MESH_ID))
        return local, remote

    def start(self, ins, outs, sems):
        local, remote = self._copies(ins, outs, sems)
        for cp in local + remote:
            cp.start()

    def finish(self, ins, outs, sems):
        local, remote = self._copies(ins, outs, sems)
        for cp in remote + local:
            cp.wait()


def _sum_slots(parts, name):
    _, R, C = parts.shape
    tr = _pick(R, (256, 128, 64, 32, 16, 8))

    def body(p_ref, o_ref):
        g = p_ref[0].astype(F32)
        for s in range(1, N_DEV):
            g = g + p_ref[s].astype(F32)
        o_ref[...] = g

    return pl.pallas_call(body, out_shape=jax.ShapeDtypeStruct((R, C), F32), grid=(R // tr,),
                          in_specs=[pl.BlockSpec((N_DEV, tr, C), lambda i: (0, i, 0))],
                          out_specs=pl.BlockSpec((tr, C), lambda i: (i, 0)), name=name,
                          compiler_params=_params("parallel"))(parts)


_hg_fwd_op, _hg_rev_op = _make_hg_scan(False), _make_hg_scan(True)
_ret_fwd_op, _ret_rev_op = _make_ret_scan(False), _make_ret_scan(True)
_rope_q, _rope_k = _make_rope(1.0, "rope_q"), _make_rope(RET_DK ** -0.5, "rope_k")
_hg_post = _make_gnorm(2, True, HG_D, False, 1.0, "hg_post")
_ret_post = _make_gnorm(2, True, RET_DV, True, 1.0, "ret_post")
_q_norm = _make_gnorm(1, False, DIL_HD, False, DIL_HD ** -0.5, "dil_qnorm")
_k_norm = _make_gnorm(1, False, DIL_HD, False, 1.0, "dil_knorm")
_att_ops = {dil: _make_attention(dil) for _, dil in DIL_GROUPS}


def _to_heads(t, d):
    S, W = t.shape
    return t.reshape(S, W // d, d).transpose(1, 0, 2)


def _dilated_mixer(parts, rel_bias, q_gain, k_gain):
    S = parts[0].shape[0]
    qg, kg = jnp.tile(q_gain, DIL_SLOTS), jnp.tile(k_gain, DIL_SLOTS)
    ii = np.arange(ATT_SB)[:, None]
    jj = np.arange(ATT_WIN)[None, :]
    outs, lses = [], []
    for g, (window, dil) in enumerate(DIL_GROUPS):
        assert window // (2 * dil) == ATT_HALO
        L = S // dil

        def to_res(t):
            return t.reshape(L, dil, DIL_SLOTS, DIL_HD).transpose(2, 1, 0, 3).reshape(DIL_SLOTS * dil, L, DIL_HD)

        def from_res(t):
            return t.reshape(DIL_SLOTS, dil, L, DIL_HD).transpose(0, 2, 1, 3).reshape(DIL_SLOTS * S, DIL_HD)

        onehot = (_t5_bucket((jj - ATT_HALO - ii) * dil)[:, :, None] == np.arange(REL_BUCKETS)).astype(np.float32)
        bias = jnp.einsum("ijb,bh->hij", onehot, rel_bias[:, g * DIL_SLOTS:(g + 1) * DIL_SLOTS],
                          precision=lax.Precision.HIGHEST)
        q = _q_norm(parts[3 * g], qg)
        k = _k_norm(parts[3 * g + 1], kg)
        o, lse = _att_ops[dil](to_res(q), to_res(k), to_res(parts[3 * g + 2]), bias)
        outs.append(from_res(o))
        lses.append(from_res(lse))
    merged = dil_merge(*outs, *lses)
    return merged.reshape(DIL_SLOTS, S, DIL_HD).transpose(1, 0, 2).reshape(S, DILW)


def _mixers(h, p):
    offs = np.cumsum(IN_SPLITS)[:-1].tolist()
    parts = jnp.split(h, offs, axis=-1)
    q, v = parts[0], parts[1]
    y_a = _hg_post(_hg_fwd_op(q, v, parts[2], p["lb_fwd"]), _hg_rev_op(q, v, parts[3], p["lb_bwd"]), p["hg_norm"], parts[4])
    qh = _to_heads(_rope_q(parts[5]), RET_DK)
    kh = _to_heads(_rope_k(parts[6]), RET_DK)
    y_b = _ret_post(_ret_fwd_op(qh, kh, parts[7]), _ret_rev_op(qh, kh, parts[7]), p["ret_norm"], parts[8])
    y_c = _dilated_mixer(parts[9:], p["rel_bias"], p["q_norm"], p["k_norm"])
    return jnp.concatenate([y_a, y_b, y_c], axis=-1)


def _col_full(blocks):
    return blocks.transpose(1, 0, 2).reshape(blocks.shape[1], -1)


def _col_blocks(full):
    K = full.shape[0]
    return full.reshape(K, N_DEV, -1).transpose(1, 0, 2)


def _row_full(blocks):
    return blocks.reshape(-1, blocks.shape[2])


def _row_blocks(full):
    return full.reshape(N_DEV, -1, full.shape[1])


FULL_OF = (_col_full, _row_full, _col_full, _row_full)
BLOCKS_OF = (_col_blocks, _row_blocks, _col_blocks, _row_blocks)
SMALL_NAMES = ("norm_mix", "norm_mlp", "hg_lb_fwd", "hg_lb_bwd", "hg_norm", "ret_norm", "q_norm", "k_norm", "rel_bias")


def _forward(x, w_in, shards, small):
    depth = len(shards)
    lb_f = jnp.cumsum(jax.nn.softmax(small["hg_lb_fwd"], axis=0), axis=0)
    lb_b = jnp.cumsum(jax.nn.softmax(small["hg_lb_bwd"], axis=0), axis=0)
    for l in range(depth):
        p = {k: small[k][l] for k in ("norm_mix", "norm_mlp", "hg_norm", "ret_norm", "q_norm", "k_norm")}
        p["lb_fwd"], p["lb_bwd"] = lb_f[l] - lb_f[0], lb_b[l] - lb_b[0]
        p["rel_bias"] = small["rel_bias"]
        sh = shards[l]
        h, (g_up, g_out) = norm_matmul(x, p["norm_mix"], w_in, (sh["up"], sh["out"]))
        x, (g_down,) = out_proj(_mixers(h, p), _row_full(g_out), x, (sh["down"],))
        halves = ()
        if l < depth - 1:
            nxt = shards[l + 1]["in"]
            halves = (nxt[:nxt.shape[0] // 2], nxt[nxt.shape[0] // 2:])
        x, g_in = mlp(x, p["norm_mlp"], _col_full(g_up), _row_full(g_down), halves)
        if halves:
            w_in = jnp.concatenate([_col_full(t) for t in g_in], axis=0)
    return x


def kernel(x, w_in, w_out, w_up, w_down, norm_mix, norm_mlp, hg_lb_fwd, hg_lb_bwd, hg_norm, ret_norm, q_norm, k_norm, rel_bias, loss_target, m_w_in, m_w_out, m_w_up, m_w_down, m_norm_mix, m_norm_mlp, m_hg_lb_fwd, m_hg_lb_bwd, m_hg_norm, m_ret_norm, m_q_norm, m_k_norm, m_rel_bias, v_w_in, v_w_out, v_w_up, v_w_down, v_norm_mix, v_norm_mlp, v_hg_lb_fwd, v_hg_lb_bwd, v_hg_norm, v_ret_norm, v_q_norm, v_k_norm, v_rel_bias):
    depth = w_in.shape[0]
    big = (w_in, w_out, w_up, w_down)
    big_m = (m_w_in, m_w_out, m_w_up, m_w_down)
    big_v = (v_w_in, v_w_out, v_w_up, v_w_down)
    small = dict(zip(SMALL_NAMES, (norm_mix, norm_mlp, hg_lb_fwd, hg_lb_bwd, hg_norm, ret_norm, q_norm, k_norm, rel_bias)))
    small_m = (m_norm_mix, m_norm_mlp, m_hg_lb_fwd, m_hg_lb_bwd, m_hg_norm, m_ret_norm, m_q_norm, m_k_norm, m_rel_bias)
    small_v = (v_norm_mix, v_norm_mlp, v_hg_lb_fwd, v_hg_lb_bwd, v_hg_norm, v_ret_norm, v_q_norm, v_k_norm, v_rel_bias)

    (gathered0,) = _Gather([w_in[0].astype(WIRE)]).call("gather_w_in0")
    names = ("in", "out", "up", "down")
    shards = [{n: w[l] for n, w in zip(names, big) if l or n != "in"} for l in range(depth)]

    xf, vjp = jax.vjp(_forward, x[0], _col_full(gathered0), shards, small)
    dy, part = _loss_grad(xf, loss_target[0])
    loss = lax.psum(0.5 / xf.shape[1] * jnp.sum(part), ("x", "y", "c"))
    dx, dw_in0, dshards, dsmall = vjp(dy)

    (landed0,) = _Exchange([_col_blocks(dw_in0)]).call("exchange_w_in0")
    dshards[0]["in"] = _sum_slots(landed0, "sum_w_in0")
    big_out = []
    for i, n in enumerate(names):
        g = jnp.stack([dshards[l][n] for l in range(depth)])
        big_out.append((g,) + tuple(_adamw(g, big[i], big_m[i], big_v[i], "adamw_" + n)))

    flat = jnp.concatenate([dsmall[n].reshape(-1) for n in SMALL_NAMES])
    n_small = flat.shape[0]
    rows = -(-n_small // 1024) * 8
    pad = lambda t: jnp.pad(t, (0, rows * 128 - n_small)).reshape(1, rows, 128)
    (small_parts,) = _Gather([pad(flat)[0]]).call("gather_small_grads")
    cat = lambda ts: pad(jnp.concatenate([t.reshape(-1) for t in ts]))
    g_small = _sum_slots(small_parts, "sum_small_grads")[None]
    small_out = (g_small,) + tuple(_adamw(g_small, cat([small[n] for n in SMALL_NAMES]), cat(small_m), cat(small_v), "adamw_small"))

    def unpack(t):
        t = t.reshape(-1)
        out, off = [], 0
        for n in SMALL_NAMES:
            size = small[n].size
            out.append(t[off:off + size].reshape(small[n].shape))
            off += size
        return out

    res = [loss, dx[None]]
    for kind in range(4):
        res += [o[kind] for o in big_out] + unpack(small_out[kind])
    return tuple(res)
```

```python
import functools
import math

import numpy as np
import jax
import jax.numpy as jnp
from jax import lax
from jax.experimental import pallas as pl
from jax.experimental.pallas import tpu as pltpu

F32 = jnp.float32
MXU = jnp.bfloat16
WIRE = jnp.bfloat16
EPS = 1e-6
N_DEV = 8
VMEM_LIMIT = 48 * 1024 * 1024

HG_HEADS, HG_D = 6, 128
RET_HEADS, RET_DK, RET_DV = 6, 64, 128
DIL_SLOTS, DIL_HD = 4, 128
DIL_GROUPS = ((128, 1), (512, 4), (2048, 16))
HGW = HG_HEADS * HG_D
RETW = RET_HEADS * RET_DV
DILW = DIL_SLOTS * DIL_HD
IN_SPLITS = (HGW, HGW, HGW, HGW, HGW, RET_HEADS * RET_DK, RET_HEADS * RET_DK, RETW, RETW) + (DILW,) * 9
REL_BUCKETS, REL_MAX_DIST = 32, 1024
ROPE_BASE = 10000.0
ADAM_LR, ADAM_B1, ADAM_B2, ADAM_EPS, ADAM_WD, ADAM_STEP = 0.001, 0.9, 0.999, 1e-08, 0.01, 10

SCAN_C = 128
SCAN_HPS = 6
HG_SUB = 16
EXP_CLAMP = 60.0
ATT_SB, ATT_HALO = 128, 64
ATT_WIN = ATT_SB + 2 * ATT_HALO
MESH_ID = pl.DeviceIdType.MESH


def _pick(n, prefs):
    for p in prefs:
        if n % p == 0:
            return p
    return n


def _params(*sem):
    return pltpu.CompilerParams(dimension_semantics=sem, vmem_limit_bytes=VMEM_LIMIT)


def _dot(a, b):
    return lax.dot_general(a.astype(MXU), b.astype(MXU), (((1,), (0,)), ((), ())), preferred_element_type=F32)


def _dot_nt(a, b):
    return lax.dot_general(a.astype(MXU), b.astype(MXU), (((1,), (1,)), ((), ())), preferred_element_type=F32)


def _dot_tn(a, b):
    return lax.dot_general(a.astype(MXU), b.astype(MXU), (((0,), (0,)), ((), ())), preferred_element_type=F32)


def _dot01(sel, x):
    if MXU == F32:
        return _dot(sel, x)
    hi = x.astype(MXU)
    r1 = x - hi.astype(F32)
    mid = r1.astype(MXU)
    lo = (r1 - mid.astype(F32)).astype(MXU)
    return _dot(sel, hi) + _dot(sel, mid) + _dot(sel, lo)


def _mm(a, b, *, nt=False, ta=False, out_dtype=F32, res=None, u_in=None, emit_act=False, rider=None, name):
    M, K = a.shape[::-1] if ta else a.shape
    N = b.shape[0] if nt else b.shape[1]
    tm = _pick(M, (1024, 512, 256, 128))
    tn = _pick(N, (1024, 768, 512, 384, 256, 128))
    tk = _pick(K, (2048, 1536, 1024, 512, 256, 128))
    ni, nj, nk = M // tm, N // tn, K // tk
    n_ride = len(rider.arrays) if rider is not None else 0

    def body(*refs):
        it = iter(refs)
        a_ref, b_ref = next(it), next(it)
        res_ref = next(it) if res is not None else None
        u_ref = next(it) if u_in is not None else None
        ride_in = [next(it) for _ in range(n_ride)]
        o_ref = next(it)
        act_ref = next(it) if emit_act else None
        ride_out = [next(it) for _ in range(n_ride)]
        acc_ref = next(it)
        sems = list(it)
        i, j, k = pl.program_id(0), pl.program_id(1), pl.program_id(2)

        step = (i * nj + j) * nk + k
        if rider is not None:
            pl.when(step == 0)(functools.partial(rider.start, ride_in, ride_out, sems))
            pl.when(step == (ni // 2) * nj * nk)(functools.partial(rider.mid, ride_in, ride_out, sems))

        @pl.when(k == 0)
        def _():
            acc_ref[...] = jnp.zeros_like(acc_ref)

        acc_ref[...] += (_dot_tn if ta else _dot_nt if nt else _dot)(a_ref[...], b_ref[...])

        @pl.when(k == nk - 1)
        def _():
            r = acc_ref[...]
            if res_ref is not None:
                r = r + res_ref[...]
            if u_ref is not None:
                r = r * (2.0 * jnp.maximum(u_ref[...], 0.0))
            o_ref[...] = r.astype(o_ref.dtype)
            if act_ref is not None:
                t = jnp.maximum(r, 0.0)
                act_ref[...] = (t * t).astype(act_ref.dtype)

        if rider is not None:
            pl.when(step == ni * nj * nk - 1)(functools.partial(rider.finish, ride_in, ride_out, sems))

    mn = pl.BlockSpec((tm, tn), lambda i, j, k: (i, j))
    anyspec = pl.BlockSpec(memory_space=pl.ANY)
    in_specs = [pl.BlockSpec((tk, tm), lambda i, j, k: (k, i)) if ta else pl.BlockSpec((tm, tk), lambda i, j, k: (i, k)),
                pl.BlockSpec((tn, tk), lambda i, j, k: (j, k)) if nt else pl.BlockSpec((tk, tn), lambda i, j, k: (k, j))]
    args = [a, b]
    for extra in (res, u_in):
        if extra is not None:
            in_specs.append(mn)
            args.append(extra)
    out_shape = [jax.ShapeDtypeStruct((M, N), out_dtype)]
    out_specs = [mn]
    if emit_act:
        out_shape.append(jax.ShapeDtypeStruct((M, N), MXU))
        out_specs.append(mn)
    scratch = [pltpu.VMEM((tm, tn), F32)]
    if rider is not None:
        in_specs += [anyspec] * n_ride
        args += list(rider.arrays)
        out_shape += rider.out_shapes()
        out_specs += [anyspec] * n_ride
        scratch += rider.sem_shapes()
    sem = ("arbitrary",) * 3 if rider is not None else ("parallel", "parallel", "arbitrary")
    out = pl.pallas_call(
        body, out_shape=out_shape, grid=(ni, nj, nk), in_specs=in_specs, out_specs=out_specs,
        scratch_shapes=scratch, name=name, compiler_params=_params(*sem))(*args)
    return out if (emit_act or rider is not None) else out[0]


def _gnorm_stats(x, center):
    if center:
        x = x - jnp.mean(x, axis=-1, keepdims=True)
    r = lax.rsqrt(jnp.mean(x * x, axis=-1, keepdims=True) + EPS)
    return x * r, r


def _silu_parts(gt):
    sg = jax.nn.sigmoid(gt)
    return gt * sg, sg * (1.0 + gt * (1.0 - sg))


def _gnorm_fwd(xs, gain, gate, *, group, center, scale, out_dtype, name):
    S, W = xs[0].shape
    ts = _pick(S, (512, 256, 128))
    nx = len(xs)

    def body(*refs):
        x_refs, g_ref = refs[:nx], refs[nx]
        gate_ref = refs[nx + 1] if gate is not None else None
        o_ref = refs[-1]
        for gi in range(W // group):
            sl = slice(gi * group, (gi + 1) * group)
            x = x_refs[0][:, sl]
            for xr in x_refs[1:]:
                x = x + xr[:, sl]
            n, _ = _gnorm_stats(x, center)
            y = n * (g_ref[:, sl] * scale)
            if gate_ref is not None:
                y = y * _silu_parts(gate_ref[:, sl])[0]
            o_ref[:, sl] = y.astype(o_ref.dtype)

    row = pl.BlockSpec((ts, W), lambda i: (i, 0))
    vec = pl.BlockSpec((1, W), lambda i: (0, 0))
    args = list(xs) + [gain] + ([gate] if gate is not None else [])
    in_specs = [row] * nx + [vec] + ([row] if gate is not None else [])
    return pl.pallas_call(body, out_shape=jax.ShapeDtypeStruct((S, W), out_dtype), grid=(S // ts,),
                          in_specs=in_specs, out_specs=row, name=name, compiler_params=_params("parallel"))(*args)


def _gnorm_bwd(dy, xs, gain, gate, *, group, center, scale, name):
    S, W = xs[0].shape
    ts = _pick(S, (512, 256, 128))
    nx = len(xs)

    def body(*refs):
        dy_ref = refs[0]
        x_refs, g_ref = refs[1:1 + nx], refs[1 + nx]
        gate_ref = refs[2 + nx] if gate is not None else None
        outs = refs[(3 + nx if gate is not None else 2 + nx):]
        dx_ref, dg_ref = outs[0], outs[1]
        dgate_ref = outs[2] if gate is not None else None

        @pl.when(pl.program_id(0) == 0)
        def _():
            dg_ref[...] = jnp.zeros_like(dg_ref)

        for gi in range(W // group):
            sl = slice(gi * group, (gi + 1) * group)
            x = x_refs[0][:, sl]
            for xr in x_refs[1:]:
                x = x + xr[:, sl]
            n, r = _gnorm_stats(x, center)
            dyv = dy_ref[:, sl].astype(F32)
            g = g_ref[:, sl] * scale
            if gate_ref is not None:
                act, dact = _silu_parts(gate_ref[:, sl])
                dgate_ref[:, sl] = dyv * n * g * dact
                dyv = dyv * act
            dg_ref[:, sl] += jnp.sum(dyv * n, axis=0, keepdims=True) * scale
            dn = dyv * g
            t = dn - n * jnp.mean(dn * n, axis=-1, keepdims=True)
            if center:
                t = t - jnp.mean(dn, axis=-1, keepdims=True)
            dx_ref[:, sl] = r * t

    row = pl.BlockSpec((ts, W), lambda i: (i, 0))
    vec = pl.BlockSpec((1, W), lambda i: (0, 0))
    args = [dy] + list(xs) + [gain] + ([gate] if gate is not None else [])
    in_specs = [row] * (1 + nx) + [vec] + ([row] if gate is not None else [])
    out_shape = [jax.ShapeDtypeStruct((S, W), F32), jax.ShapeDtypeStruct((1, W), F32)]
    out_specs = [row, vec]
    if gate is not None:
        out_shape.append(jax.ShapeDtypeStruct((S, W), F32))
        out_specs.append(row)
    out = pl.pallas_call(body, out_shape=out_shape, grid=(S // ts,), in_specs=in_specs, out_specs=out_specs,
                         name=name, compiler_params=_params("arbitrary"))(*args)
    return out[0], out[1], (out[2] if gate is not None else None)


def _make_gnorm(nx, has_gate, group, center, scale, tag):
    kw = dict(group=group, center=center, scale=scale)

    @jax.custom_vjp
    def op(*args):
        return fwd(*args)[0]

    def fwd(*args):
        xs, gain = args[:nx], args[nx]
        gate = args[nx + 1] if has_gate else None
        y = _gnorm_fwd(xs, gain[None, :], gate, out_dtype=F32, name=tag + "_fwd", **kw)
        return y, args

    def bwd(args, dy):
        xs, gain = args[:nx], args[nx]
        gate = args[nx + 1] if has_gate else None
        dx, dg, dgate = _gnorm_bwd(dy, xs, gain[None, :], gate, name=tag + "_bwd", **kw)
        return (dx,) * nx + (dg[0],) + ((dgate,) if has_gate else ())

    op.defvjp(fwd, bwd)
    return op


def _ride(cls, arrays):
    return cls(list(arrays)) if len(arrays) else None


def _mm_ride(*args, rider, name, **kw):
    if rider is None:
        return _mm(*args, name=name, **kw), ()
    out = _mm(*args, rider=rider, name=name + "_ride", **kw)
    n = len(rider.arrays)
    main = out[:-n]
    return (main[0] if len(main) == 1 else tuple(main)), tuple(out[-n:])


def _landed_sums(landed, tag):
    return tuple(_sum_slots(l, "%s_sum%d" % (tag, i)) for i, l in enumerate(landed))


def _rms(x, g, name):
    return _gnorm_fwd([x], g[None, :], None, group=x.shape[1], center=False, scale=1.0, out_dtype=MXU, name=name)


def _rms_bwd(dxn, x, g, name):
    dx, dg, _ = _gnorm_bwd(dxn, [x], g[None, :], None, group=x.shape[1], center=False, scale=1.0, name=name)
    return dx, dg[0]


def _wire(shards):
    return [t.astype(WIRE) for t in shards]


@jax.custom_vjp
def norm_matmul(x, g, w, shards):
    return _norm_matmul_fwd(x, g, w, shards)[0]


def _norm_matmul_fwd(x, g, w, shards):
    xn = _rms(x, g, "rms_in")
    h, gathered = _mm_ride(xn, w, rider=_ride(_Gather, _wire(shards)), name="mm_in")
    return (h, gathered), (x, g, w, xn)


def _norm_matmul_bwd(saved, cts):
    x, g, w, xn = saved
    dh, d_gathered = cts
    dxn, l0 = _mm_ride(dh, w, nt=True, rider=_ride(_Exchange, d_gathered[:1]), name="mm_in_dx")
    dw, l1 = _mm_ride(xn, dh, ta=True, out_dtype=w.dtype, rider=_ride(_Exchange, d_gathered[1:]), name="mm_in_dw")
    dx, dg = _rms_bwd(dxn, x, g, "rms_in_bwd")
    return dx, dg, dw, _landed_sums(l0 + l1, "in")


norm_matmul.defvjp(_norm_matmul_fwd, _norm_matmul_bwd)


@jax.custom_vjp
def out_proj(y, w, x, shards):
    return _out_proj_fwd(y, w, x, shards)[0]


def _out_proj_fwd(y, w, x, shards):
    yb = y.astype(MXU)
    out, gathered = _mm_ride(yb, w, res=x, rider=_ride(_Gather, _wire(shards)), name="mm_out")
    return (out, gathered), (yb, w)


def _out_proj_bwd(saved, cts):
    yb, w = saved
    dout, d_gathered = cts
    db = dout.astype(MXU)
    dy, l0 = _mm_ride(db, w, nt=True, rider=_ride(_Exchange, d_gathered[:1]), name="mm_out_dy")
    dw, l1 = _mm_ride(yb, db, ta=True, out_dtype=w.dtype, rider=_ride(_Exchange, d_gathered[1:]), name="mm_out_dw")
    return dy, dw, dout, _landed_sums(l0 + l1, "out")


out_proj.defvjp(_out_proj_fwd, _out_proj_bwd)


@jax.custom_vjp
def mlp(x, g, w_up, s_down, shards):
    return _mlp_fwd(x, g, w_up, s_down, shards)[0]


def _mlp_fwd(x, g, w_up, s_down, shards):
    hm = _rms(x, g, "rms_mlp")
    (u, act), (g_down,) = _mm_ride(hm, w_up, emit_act=True, rider=_Gather(_wire([s_down])), name="mm_up")
    w_down = _row_full(g_down)
    out, gathered = _mm_ride(act, w_down, res=x, rider=_ride(_Gather, _wire(shards)), name="mm_down")
    return (out, gathered), (x, g, w_up, w_down, hm, u, act)


def _mlp_bwd(saved, cts):
    x, g, w_up, w_down, hm, u, act = saved
    dout, d_gathered = cts
    db = dout.astype(MXU)
    du, l0 = _mm_ride(db, w_down, nt=True, u_in=u, out_dtype=MXU, rider=_ride(_Exchange, d_gathered[:1]), name="mm_down_da")
    dw_down, l1 = _mm_ride(act, db, ta=True, out_dtype=w_down.dtype, rider=_ride(_Exchange, d_gathered[1:]), name="mm_down_dw")
    dhm, l_down = _mm_ride(du, w_up, nt=True, rider=_Exchange([_row_blocks(dw_down)]), name="mm_up_dx")
    dw_up = _mm(hm, du, ta=True, out_dtype=w_up.dtype, name="mm_up_dw")
    dx, dg = _rms_bwd(dhm, x, g, "rms_mlp_bwd")
    return dout + dx, dg, dw_up, _landed_sums(l_down, "down")[0], _landed_sums(l0 + l1, "mlp")


mlp.defvjp(_mlp_fwd, _mlp_bwd)


def _order(C, rev):
    i = np.arange(C)
    return (C - 1 - i) if rev else i


def _hg_constants(C, rev):
    p = _order(C, rev)
    pi, pj = p[:, None], p[None, :]
    tri = (pj <= pi).astype(np.float32)
    masks = [((pi // HG_SUB) == (pj // HG_SUB)) & (pj <= pi)]
    h = HG_SUB
    halves = []
    while h < C:
        masks.append(((pi // (2 * h)) == (pj // (2 * h))) & ((pi // h) % 2 == 1) & ((pj // h) % 2 == 0))
        halves.append(h)
        h *= 2
    return tri, np.stack(masks).astype(np.float32), halves


def _hg_tables(b_scr, C, h, rev):
    nb = C // h
    g_rows, e_rows = [], []
    for rb in range(nb):
        if rev:
            e = b_scr[pl.ds(rb * h, 1), :]
            g = b_scr[pl.ds((rb + 1) * h, 1), :] if rb < nb - 1 else None
        else:
            e = b_scr[pl.ds((rb + 1) * h - 1, 1), :]
            g = b_scr[pl.ds(rb * h - 1, 1), :] if rb >= 1 else None
        e_rows.append(jnp.broadcast_to(e, (h, HG_D)))
        g_rows.append(jnp.zeros((h, HG_D), F32) if g is None else jnp.broadcast_to(g, (h, HG_D)))
    return jnp.concatenate(g_rows, axis=0), jnp.concatenate(e_rows, axis=0)


def _hg_gates(z, lb):
    sg = jax.nn.sigmoid(z)
    f = lb + (1.0 - lb) * sg
    return sg, f, 1.0 - f, jnp.log(f)


def _hg_exponents(b, b_scr, C, halves, rev):
    g0, _ = _hg_tables(b_scr, C, HG_SUB, rev)
    p0 = b - g0
    out = [(p0, jnp.minimum(-p0, EXP_CLAMP))]
    for h in halves:
        g, e = _hg_tables(b_scr, C, h, rev)
        out.append((jnp.minimum(b - g, 0.0), jnp.minimum(e - b, 0.0)))
    return out


def _hg_scan_fwd(q, v, z, lb, rev, name):
    S = q.shape[0]
    C = SCAN_C
    nc = S // C
    tri, masks, halves = _hg_constants(C, rev)
    nlev = masks.shape[0]
    end_row = 0 if rev else C - 1
    vT = v.astype(MXU).T

    def body(q_ref, v_ref, vT_ref, z_ref, lb_ref, tri_ref, mask_ref, o_ref, st_ref, s_scr, b_scr):
        @pl.when(pl.program_id(1) == 0)
        def _():
            s_scr[...] = jnp.zeros_like(s_scr)

        for hh in range(SCAN_HPS):
            sl = slice(hh * HG_D, (hh + 1) * HG_D)
            bh = b_scr.at[hh]
            _, f, k, lf = _hg_gates(z_ref[:, sl], lb_ref[:, sl])
            b = _dot01(tri_ref[...], lf)
            bh[...] = b
            qv, vv = q_ref[:, sl], v_ref[:, sl]
            st = s_scr[hh]
            st_ref[hh, 0] = st
            a = jnp.zeros((C, C), F32)
            for lv, (eq, ek) in enumerate(_hg_exponents(b, bh, C, halves, rev)):
                a = a + mask_ref[lv] * _dot_nt(qv * jnp.exp(eq), k * jnp.exp(ek))
            bend = bh[pl.ds(end_row, 1), :]
            o_ref[:, sl] = _dot(a, vv) + _dot_nt(qv * jnp.exp(b), st)
            s_scr[hh] = st * jnp.exp(bend) + _dot(vT_ref[sl, :], k * jnp.exp(bend - b))

    cidx = (lambda c: nc - 1 - c) if rev else (lambda c: c)
    wid = SCAN_HPS * HG_D
    blk = pl.BlockSpec((C, wid), lambda h, c: (cidx(c), h))
    o, states = pl.pallas_call(
        body,
        out_shape=[jax.ShapeDtypeStruct((S, HGW), F32), jax.ShapeDtypeStruct((HG_HEADS, nc, HG_D, HG_D), F32)],
        grid=(HG_HEADS // SCAN_HPS, nc),
        in_specs=[blk, blk, pl.BlockSpec((wid, C), lambda h, c: (h, cidx(c))), blk,
                  pl.BlockSpec((1, wid), lambda h, c: (0, h)),
                  pl.BlockSpec((C, C), lambda h, c: (0, 0)), pl.BlockSpec((nlev, C, C), lambda h, c: (0, 0, 0))],
        out_specs=[blk, pl.BlockSpec((SCAN_HPS, 1, HG_D, HG_D), lambda h, c: (h, cidx(c), 0, 0))],
        scratch_shapes=[pltpu.VMEM((SCAN_HPS, HG_D, HG_D), F32), pltpu.VMEM((SCAN_HPS, C, HG_D), F32)],
        name=name, compiler_params=_params("parallel", "arbitrary"),
    )(q, v.astype(MXU), vT, z, lb, jnp.asarray(tri, MXU), jnp.asarray(masks))
    return o, states


def _hg_scan_bwd(q, v, z, lb, states, do, rev, name):
    S = q.shape[0]
    C = SCAN_C
    nc = S // C
    tri, masks, halves = _hg_constants(C, rev)
    nlev = masks.shape[0]
    end_row = 0 if rev else C - 1
    masks_t = np.ascontiguousarray(np.transpose(masks, (0, 2, 1)))
    dob = do.astype(MXU)

    def body(q_ref, v_ref, z_ref, lb_ref, do_ref, doT_ref, st_ref, tri_ref, triT_ref, mask_ref, maskT_ref,
             dq_ref, dv_ref, dz_ref, dlb_ref, dn_scr, b_scr):
        @pl.when(pl.program_id(1) == 0)
        def _():
            dn_scr[...] = jnp.zeros_like(dn_scr)
            dlb_ref[...] = jnp.zeros_like(dlb_ref)

        for hh in range(SCAN_HPS):
            sl = slice(hh * HG_D, (hh + 1) * HG_D)
            bh = b_scr.at[hh]
            lb_v = lb_ref[:, sl]
            sg, f, k, lf = _hg_gates(z_ref[:, sl], lb_v)
            b = _dot01(tri_ref[...], lf)
            bh[...] = b
            qv, vv, dov = q_ref[:, sl], v_ref[:, sl], do_ref[:, sl]
            st, dn = st_ref[hh, 0], dn_scr[hh]
            da = _dot_nt(dov, vv)
            da_t = da.T
            a = jnp.zeros((C, C), F32)
            dq = jnp.zeros((C, HG_D), F32)
            dk = jnp.zeros((C, HG_D), F32)
            for lv, (eq, ek) in enumerate(_hg_exponents(b, bh, C, halves, rev)):
                xq, xk = jnp.exp(eq), jnp.exp(ek)
                qs, ks = qv * xq, k * xk
                a = a + mask_ref[lv] * _dot_nt(qs, ks)
                dq = dq + _dot(mask_ref[lv] * da, ks) * xq
                dk = dk + _dot(maskT_ref[lv] * da_t, qs) * xk
            bend = bh[pl.ds(end_row, 1), :]
            xb, xe, xend = jnp.exp(b), jnp.exp(bend - b), jnp.exp(bend)
            dq = dq + _dot(dov, st) * xb
            dk_state = _dot(vv, dn) * xe
            dk = dk + dk_state
            dv_ref[:, sl] = _dot(a.T, dov) + _dot_nt(k * xe, dn)
            dn_scr[hh] = dn * xend + _dot(doT_ref[sl, :], qv * xb)
            extra = jnp.sum(k * dk_state, axis=0, keepdims=True) + xend * jnp.sum(st * dn, axis=0, keepdims=True)
            rows = lax.broadcasted_iota(jnp.int32, (C, HG_D), 0)
            db = qv * dq - k * dk + jnp.where(rows == end_row, extra, 0.0)
            df = _dot01(triT_ref[...], db) / f - dk
            dq_ref[:, sl] = dq
            dz_ref[:, sl] = df * (1.0 - lb_v) * sg * (1.0 - sg)
            dlb_ref[:, sl] += jnp.sum(df * (1.0 - sg), axis=0, keepdims=True)

    cidx = (lambda c: c) if rev else (lambda c: nc - 1 - c)
    wid = SCAN_HPS * HG_D
    blk = pl.BlockSpec((C, wid), lambda h, c: (cidx(c), h))
    vec = pl.BlockSpec((1, wid), lambda h, c: (0, h))
    cc = pl.BlockSpec((C, C), lambda h, c: (0, 0))
    lcc = pl.BlockSpec((nlev, C, C), lambda h, c: (0, 0, 0))
    sd = jax.ShapeDtypeStruct((S, HGW), F32)
    return pl.pallas_call(
        body, out_shape=[sd, sd, sd, jax.ShapeDtypeStruct((1, HGW), F32)], grid=(HG_HEADS // SCAN_HPS, nc),
        in_specs=[blk, blk, blk, vec, blk, pl.BlockSpec((wid, C), lambda h, c: (h, cidx(c))),
                  pl.BlockSpec((SCAN_HPS, 1, HG_D, HG_D), lambda h, c: (h, cidx(c), 0, 0)), cc, cc, lcc, lcc],
        out_specs=[blk, blk, blk, vec],
        scratch_shapes=[pltpu.VMEM((SCAN_HPS, HG_D, HG_D), F32), pltpu.VMEM((SCAN_HPS, C, HG_D), F32)],
        name=name, compiler_params=_params("parallel", "arbitrary"),
    )(q, v.astype(MXU), z, lb, dob, dob.T, states, jnp.asarray(tri, MXU), jnp.asarray(tri.T, MXU),
      jnp.asarray(masks), jnp.asarray(masks_t))


def _make_hg_scan(rev):
    tag = "hg_rev" if rev else "hg_fwd"

    @jax.custom_vjp
    def op(q, v, z, lb):
        return fwd(q, v, z, lb)[0]

    def fwd(q, v, z, lb):
        o, states = _hg_scan_fwd(q, v, z, lb[None, :], rev, tag)
        return o, (q, v, z, lb, states)

    def bwd(saved, do):
        q, v, z, lb, states = saved
        dq, dv, dz, dlb = _hg_scan_bwd(q, v, z, lb[None, :], states, do, rev, tag + "_bwd")
        return dq, dv, dz, dlb[0]

    op.defvjp(fwd, bwd)
    return op


def _ret_constants(C, rev):
    hidx = np.arange(RET_HEADS, dtype=np.float64)
    lg = np.log1p(-np.exp2(-5.0 - hidx))
    if rev:
        lg = lg[::-1]
    p = _order(C, rev).astype(np.float64)
    rel = p[:, None] - p[None, :]
    dmat = np.where(rel >= 0, np.exp(lg[:, None, None] * np.maximum(rel, 0.0)), 0.0)
    xi = np.exp(lg[:, None] * (p[None, :] + 1.0))
    zeta = np.exp(lg[:, None] * (C - 1.0 - p[None, :]))
    gc = np.exp(lg * C)
    bc = lambda t: np.ascontiguousarray(np.broadcast_to(t[:, :, None], (RET_HEADS, C, RET_DK))).astype(np.float32)
    gcb = np.ascontiguousarray(np.broadcast_to(gc[:, None, None], (RET_HEADS, 1, RET_DK))).astype(np.float32)
    return dmat.astype(np.float32), bc(xi), bc(zeta), gcb


def _ret_scan_fwd(qh, kh, v, rev, name):
    S = v.shape[0]
    C = SCAN_C
    nc = S // C
    dmat, xi, zeta, gc = _ret_constants(C, rev)
    vb = v.astype(MXU)

    def body(q_ref, k_ref, v_ref, vT_ref, d_ref, xi_ref, zeta_ref, gc_ref, o_ref, st_ref, s_scr):
        @pl.when(pl.program_id(1) == 0)
        def _():
            s_scr[...] = jnp.zeros_like(s_scr)

        for hh in range(SCAN_HPS):
            sl = slice(hh * RET_DV, (hh + 1) * RET_DV)
            qv, kv = q_ref[hh], k_ref[hh]
            st = s_scr[hh]
            st_ref[hh, 0] = st
            sc = _dot_nt(qv, kv) * d_ref[hh]
            o_ref[:, sl] = _dot(sc, v_ref[:, sl]) + _dot_nt(qv * xi_ref[hh], st)
            s_scr[hh] = st * gc_ref[hh] + _dot(vT_ref[sl, :], kv * zeta_ref[hh])

    cidx = (lambda c: nc - 1 - c) if rev else (lambda c: c)
    hk = pl.BlockSpec((SCAN_HPS, C, RET_DK), lambda h, c: (h, cidx(c), 0))
    vblk = pl.BlockSpec((C, SCAN_HPS * RET_DV), lambda h, c: (cidx(c), h))
    tab = pl.BlockSpec((SCAN_HPS, C, RET_DK), lambda h, c: (h, 0, 0))
    return pl.pallas_call(
        body,
        out_shape=[jax.ShapeDtypeStruct((S, RETW), F32), jax.ShapeDtypeStruct((RET_HEADS, nc, RET_DV, RET_DK), F32)],
        grid=(RET_HEADS // SCAN_HPS, nc),
        in_specs=[hk, hk, vblk, pl.BlockSpec((SCAN_HPS * RET_DV, C), lambda h, c: (h, cidx(c))),
                  pl.BlockSpec((SCAN_HPS, C, C), lambda h, c: (h, 0, 0)), tab, tab,
                  pl.BlockSpec((SCAN_HPS, 1, RET_DK), lambda h, c: (h, 0, 0))],
        out_specs=[vblk, pl.BlockSpec((SCAN_HPS, 1, RET_DV, RET_DK), lambda h, c: (h, cidx(c), 0, 0))],
        scratch_shapes=[pltpu.VMEM((SCAN_HPS, RET_DV, RET_DK), F32)],
        name=name, compiler_params=_params("parallel", "arbitrary"),
    )(qh, kh, vb, vb.T, jnp.asarray(dmat), jnp.asarray(xi), jnp.asarray(zeta), jnp.asarray(gc))


def _ret_scan_bwd(qh, kh, v, states, do, rev, name):
    S = v.shape[0]
    C = SCAN_C
    nc = S // C
    dmat, xi, zeta, gc = _ret_constants(C, rev)
    vb, dob = v.astype(MXU), do.astype(MXU)

    def body(q_ref, k_ref, v_ref, do_ref, doT_ref, st_ref, d_ref, xi_ref, zeta_ref, gc_ref,
             dq_ref, dk_ref, dv_ref, dn_scr):
        @pl.when(pl.program_id(1) == 0)
        def _():
            dn_scr[...] = jnp.zeros_like(dn_scr)

        for hh in range(SCAN_HPS):
            sl = slice(hh * RET_DV, (hh + 1) * RET_DV)
            qv, kv, vv, dov = q_ref[hh], k_ref[hh], v_ref[:, sl], do_ref[:, sl]
            st, dn = st_ref[hh, 0], dn_scr[hh]
            dm = d_ref[hh]
            sc = _dot_nt(qv, kv) * dm
            dsc = _dot_nt(dov, vv) * dm
            kz = kv * zeta_ref[hh]
            dq_ref[hh] = _dot(dsc, kv) + _dot(dov, st) * xi_ref[hh]
            dk_ref[hh] = _dot(dsc.T, qv) + _dot(vv, dn) * zeta_ref[hh]
            dv_ref[:, sl] = _dot(sc.T, dov) + _dot_nt(kz, dn)
            dn_scr[hh] = dn * gc_ref[hh] + _dot(doT_ref[sl, :], qv * xi_ref[hh])

    cidx = (lambda c: c) if rev else (lambda c: nc - 1 - c)
    hk = pl.BlockSpec((SCAN_HPS, C, RET_DK), lambda h, c: (h, cidx(c), 0))
    vblk = pl.BlockSpec((C, SCAN_HPS * RET_DV), lambda h, c: (cidx(c), h))
    tab = pl.BlockSpec((SCAN_HPS, C, RET_DK), lambda h, c: (h, 0, 0))
    hs = jax.ShapeDtypeStruct(qh.shape, F32)
    return pl.pallas_call(
        body, out_shape=[hs, hs, jax.ShapeDtypeStruct((S, RETW), F32)], grid=(RET_HEADS // SCAN_HPS, nc),
        in_specs=[hk, hk, vblk, vblk, pl.BlockSpec((SCAN_HPS * RET_DV, C), lambda h, c: (h, cidx(c))),
                  pl.BlockSpec((SCAN_HPS, 1, RET_DV, RET_DK), lambda h, c: (h, cidx(c), 0, 0)),
                  pl.BlockSpec((SCAN_HPS, C, C), lambda h, c: (h, 0, 0)), tab, tab,
                  pl.BlockSpec((SCAN_HPS, 1, RET_DK), lambda h, c: (h, 0, 0))],
        out_specs=[hk, hk, vblk],
        scratch_shapes=[pltpu.VMEM((SCAN_HPS, RET_DV, RET_DK), F32)],
        name=name, compiler_params=_params("parallel", "arbitrary"),
    )(qh, kh, vb, dob, dob.T, states, jnp.asarray(dmat), jnp.asarray(xi), jnp.asarray(zeta), jnp.asarray(gc))


def _make_ret_scan(rev):
    tag = "ret_rev" if rev else "ret_fwd"

    @jax.custom_vjp
    def op(qh, kh, v):
        return fwd(qh, kh, v)[0]

    def fwd(qh, kh, v):
        o, states = _ret_scan_fwd(qh, kh, v, rev, tag)
        return o, (qh, kh, v, states)

    def bwd(saved, do):
        qh, kh, v, states = saved
        return tuple(_ret_scan_bwd(qh, kh, v, states, do, rev, tag + "_bwd"))

    op.defvjp(fwd, bwd)
    return op


def _rope_tables(S, scale):
    half = RET_DK // 2
    inv = ROPE_BASE ** (-np.arange(half, dtype=np.float32) / half)
    ang = np.arange(S, dtype=np.float32)[:, None] * inv[None, :]
    cos, sin = np.cos(ang), np.sin(ang)
    cos_t = np.tile(np.concatenate([cos, cos], axis=1), (1, RET_HEADS)) * scale
    sin_t = np.tile(np.concatenate([-sin, sin], axis=1), (1, RET_HEADS)) * scale
    return cos_t.astype(np.float32), sin_t.astype(np.float32)


def _rope_apply(t, cos_t, sin_t, name):
    S, W = t.shape
    ts = _pick(S, (512, 256, 128))
    half = RET_DK // 2

    def body(t_ref, c_ref, s_ref, o_ref):
        tv = t_ref[...]
        lane = lax.broadcasted_iota(jnp.int32, tv.shape, 1)
        partner = jnp.where(lane % RET_DK < half, pltpu.roll(tv, W - half, 1), pltpu.roll(tv, half, 1))
        o_ref[...] = tv * c_ref[...] + partner * s_ref[...]

    row = pl.BlockSpec((ts, W), lambda i: (i, 0))
    return pl.pallas_call(body, out_shape=jax.ShapeDtypeStruct((S, W), F32), grid=(S // ts,),
                          in_specs=[row, row, row], out_specs=row, name=name,
                          compiler_params=_params("parallel"))(t, jnp.asarray(cos_t), jnp.asarray(sin_t))


def _make_rope(scale, tag):
    def apply(t):
        cos_t, sin_t = _rope_tables(t.shape[0], scale)
        return _rope_apply(t, cos_t, sin_t, tag)

    op = jax.custom_vjp(apply)

    def fwd(t):
        return apply(t), None

    def bwd(_, dout):
        cos_t, sin_t = _rope_tables(dout.shape[0], scale)
        return (_rope_apply(dout, cos_t, -sin_t, tag + "_bwd"),)

    op.defvjp(fwd, bwd)
    return op


def _att_geometry(L):
    tq = _pick(L, (512, 256, 128))
    return tq, L // tq, tq // ATT_HALO


def _att_specs(tq, per):
    main = pl.BlockSpec((1, tq, DIL_HD), lambda b, n: (b, n, 0))
    prev = pl.BlockSpec((1, ATT_HALO, DIL_HD), lambda b, n: (b, jnp.maximum(n * per - 1, 0), 0))
    return main, prev


def _att_valid(n, u, tq, L):
    ii = lax.broadcasted_iota(jnp.int32, (ATT_SB, ATT_WIN), 0)
    jj = lax.broadcasted_iota(jnp.int32, (ATT_SB, ATT_WIN), 1)
    key = n * tq + u * ATT_SB - ATT_HALO + jj
    return (jnp.abs(jj - ATT_HALO - ii) <= ATT_HALO) & (key >= 0) & (key < L)


def _att_fill(buf, prev_ref, main_ref, next_ref, tq):
    buf[pl.ds(0, ATT_HALO), :] = prev_ref[0]
    buf[pl.ds(ATT_HALO, tq), :] = main_ref[0]
    buf[pl.ds(ATT_HALO + tq, ATT_HALO), :] = next_ref[0]


def _att_fwd(q, k, v, bias, dil, name):
    B, L, _ = q.shape
    tq, nt, per = _att_geometry(L)
    last = L // ATT_HALO - 1

    def body(q_ref, kp_ref, k_ref, kn_ref, vp_ref, v_ref, vn_ref, bias_ref, o_ref, lse_ref, kbuf, vbuf):
        n = pl.program_id(1)
        _att_fill(kbuf, kp_ref, k_ref, kn_ref, tq)
        _att_fill(vbuf, vp_ref, v_ref, vn_ref, tq)
        for u in range(tq // ATT_SB):
            rows = pl.ds(u * ATT_SB, ATT_SB)
            win = pl.ds(u * ATT_SB, ATT_WIN)
            s = _dot_nt(q_ref[0, rows, :], kbuf[win, :]) + bias_ref[0]
            s = jnp.where(_att_valid(n, u, tq, L), s, -1e30)
            m = jnp.max(s, axis=-1, keepdims=True)
            p = jnp.exp(s - m)
            den = jnp.sum(p, axis=-1, keepdims=True)
            o_ref[0, rows, :] = _dot(p, vbuf[win, :]) / den
            lse_ref[0, rows, :] = jnp.broadcast_to(m + jnp.log(den), (ATT_SB, DIL_HD))

    main, prev = _att_specs(tq, per)
    nxt = pl.BlockSpec((1, ATT_HALO, DIL_HD), lambda b, n: (b, jnp.minimum((n + 1) * per, last), 0))
    sd = jax.ShapeDtypeStruct((B, L, DIL_HD), F32)
    return pl.pallas_call(
        body, out_shape=[sd, sd], grid=(B, nt),
        in_specs=[main, prev, main, nxt, prev, main, nxt,
                  pl.BlockSpec((1, ATT_SB, ATT_WIN), lambda b, n: (b // dil, 0, 0))],
        out_specs=[main, main],
        scratch_shapes=[pltpu.VMEM((tq + 2 * ATT_HALO, DIL_HD), q.dtype), pltpu.VMEM((tq + 2 * ATT_HALO, DIL_HD), q.dtype)],
        name=name, compiler_params=_params("parallel", "arbitrary"),
    )(q, k, k, k, v, v, v, bias)


def _att_bwd(q, k, v, bias, o, lse, do, dlse, dil, name):
    B, L, _ = q.shape
    tq, nt, per = _att_geometry(L)
    last = L // ATT_HALO - 1

    def body(q_ref, kp_ref, k_ref, kn_ref, vp_ref, v_ref, vn_ref, bias_ref, o_ref, lse_ref, do_ref, dlse_ref,
             dq_ref, dk_ref, dkp_ref, dkn_ref, dv_ref, dvp_ref, dvn_ref, dbias_ref, kbuf, vbuf, dkbuf, dvbuf):
        b, n = pl.program_id(0), pl.program_id(1)

        @pl.when((b % dil == 0) & (n == 0))
        def _():
            dbias_ref[...] = jnp.zeros_like(dbias_ref)

        _att_fill(kbuf, kp_ref, k_ref, kn_ref, tq)
        _att_fill(vbuf, vp_ref, v_ref, vn_ref, tq)
        dkbuf[...] = jnp.zeros_like(dkbuf)
        dvbuf[...] = jnp.zeros_like(dvbuf)
        for u in range(tq // ATT_SB):
            rows = pl.ds(u * ATT_SB, ATT_SB)
            win = pl.ds(u * ATT_SB, ATT_WIN)
            qu, kw, vw = q_ref[0, rows, :], kbuf[win, :], vbuf[win, :]
            dou = do_ref[0, rows, :]
            s = _dot_nt(qu, kw) + bias_ref[0]
            lse_u = jnp.max(lse_ref[0, rows, :], axis=-1, keepdims=True)
            p = jnp.where(_att_valid(n, u, tq, L), jnp.exp(s - lse_u), 0.0)
            corr = jnp.sum(dlse_ref[0, rows, :] - dou * o_ref[0, rows, :], axis=-1, keepdims=True)
            ds = p * (_dot_nt(dou, vw) + corr)
            dq_ref[0, rows, :] = _dot(ds, kw)
            dkbuf[win, :] += _dot(ds.T, qu)
            dvbuf[win, :] += _dot(p.T, dou)
            dbias_ref[0] += ds
        for full, lo, hi in ((dkbuf, dkp_ref, dkn_ref), (dvbuf, dvp_ref, dvn_ref)):
            lo[0, 0] = full[pl.ds(0, ATT_HALO), :]
            hi[0, 0] = full[pl.ds(ATT_HALO + tq, ATT_HALO), :]
        dk_ref[0] = dkbuf[pl.ds(ATT_HALO, tq), :]
        dv_ref[0] = dvbuf[pl.ds(ATT_HALO, tq), :]

    main, prev = _att_specs(tq, per)
    nxt = pl.BlockSpec((1, ATT_HALO, DIL_HD), lambda b, n: (b, jnp.minimum((n + 1) * per, last), 0))
    halo = pl.BlockSpec((1, 1, ATT_HALO, DIL_HD), lambda b, n: (b, n, 0, 0))
    bias_spec = pl.BlockSpec((1, ATT_SB, ATT_WIN), lambda b, n: (b // dil, 0, 0))
    sd = jax.ShapeDtypeStruct((B, L, DIL_HD), F32)
    hd = jax.ShapeDtypeStruct((B, nt, ATT_HALO, DIL_HD), F32)
    width = tq + 2 * ATT_HALO
    dq, dk, dkp, dkn, dv, dvp, dvn, dbias = pl.pallas_call(
        body, out_shape=[sd, sd, hd, hd, sd, hd, hd, jax.ShapeDtypeStruct(bias.shape, F32)], grid=(B, nt),
        in_specs=[main, prev, main, nxt, prev, main, nxt, bias_spec, main, main, main, main],
        out_specs=[main, main, halo, halo, main, halo, halo, bias_spec],
        scratch_shapes=[pltpu.VMEM((width, DIL_HD), q.dtype), pltpu.VMEM((width, DIL_HD), q.dtype),
                        pltpu.VMEM((width, DIL_HD), F32), pltpu.VMEM((width, DIL_HD), F32)],
        name=name, compiler_params=_params("arbitrary", "arbitrary"),
    )(q, k, k, k, v, v, v, bias, o, lse, do, dlse)

    def fold(mainv, lo, hi):
        t = mainv.reshape(B, nt, tq, DIL_HD)
        if nt > 1:
            t = t.at[:, :-1, tq - ATT_HALO:, :].add(lo[:, 1:])
            t = t.at[:, 1:, :ATT_HALO, :].add(hi[:, :-1])
        return t.reshape(B, L, DIL_HD)

    return dq, fold(dk, dkp, dkn), fold(dv, dvp, dvn), dbias


def _make_attention(dil):
    tag = "att_d%d" % dil

    @jax.custom_vjp
    def op(q, k, v, bias):
        return fwd(q, k, v, bias)[0]

    def fwd(q, k, v, bias):
        qb, kb, vb = q.astype(MXU), k.astype(MXU), v.astype(MXU)
        o, lse = _att_fwd(qb, kb, vb, bias, dil, tag)
        return (o, lse), (qb, kb, vb, bias, o, lse)

    def bwd(saved, cts):
        qb, kb, vb, bias, o, lse = saved
        do, dlse = cts
        return tuple(_att_bwd(qb, kb, vb, bias, o, lse, do, dlse, dil, tag + "_bwd"))

    op.defvjp(fwd, bwd)
    return op


def _merge_weights(l0, l1, l2):
    m = jnp.maximum(jnp.maximum(l0, l1), l2)
    e0, e1, e2 = jnp.exp(l0 - m), jnp.exp(l1 - m), jnp.exp(l2 - m)
    inv = 1.0 / (e0 + e1 + e2)
    return e0 * inv, e1 * inv, e2 * inv


def _merge_call(body, n_in, n_out, shape, name):
    R, W = shape
    ts = _pick(R, (1024, 512, 256, 128))
    row = pl.BlockSpec((ts, W), lambda i: (i, 0))
    sd = jax.ShapeDtypeStruct(shape, F32)
    return pl.pallas_call(body, out_shape=[sd] * n_out, grid=(R // ts,), in_specs=[row] * n_in,
                          out_specs=[row] * n_out, name=name, compiler_params=_params("parallel"))


@jax.custom_vjp
def dil_merge(o0, o1, o2, l0, l1, l2):
    return _dil_merge_fwd(o0, o1, o2, l0, l1, l2)[0]


def _dil_merge_fwd(*args):
    def body(o0, o1, o2, l0, l1, l2, out):
        w0, w1, w2 = _merge_weights(l0[...], l1[...], l2[...])
        out[...] = w0 * o0[...] + w1 * o1[...] + w2 * o2[...]

    return _merge_call(body, 6, 1, args[0].shape, "dil_merge")(*args)[0], args


def _dil_merge_bwd(args, dout):
    def body(o0, o1, o2, l0, l1, l2, d, do0, do1, do2, dl0, dl1, dl2):
        ws = _merge_weights(l0[...], l1[...], l2[...])
        dv = d[...]
        dws = [dv * o[...] for o in (o0, o1, o2)]
        mean = ws[0] * dws[0] + ws[1] * dws[1] + ws[2] * dws[2]
        for w, dw, do_ref, dl_ref in zip(ws, dws, (do0, do1, do2), (dl0, dl1, dl2)):
            do_ref[...] = w * dv
            dl_ref[...] = w * (dw - mean)

    return tuple(_merge_call(body, 7, 6, args[0].shape, "dil_merge_bwd")(*args, dout))


dil_merge.defvjp(_dil_merge_fwd, _dil_merge_bwd)


def _t5_bucket(rel):
    nb = REL_BUCKETS // 2
    max_exact = nb // 2
    sign_off = np.where(rel > 0, nb, 0)
    n = np.abs(rel)
    nf = np.maximum(n, 1).astype(np.float32)
    large = max_exact + (np.log(nf / np.float32(max_exact)) / np.float32(math.log(REL_MAX_DIST / max_exact))
                         * np.float32(nb - max_exact)).astype(np.int32)
    large = np.minimum(large, nb - 1)
    return sign_off + np.where(n < max_exact, n, large)


def _loss_grad(xf, target):
    S, D = xf.shape
    ts = _pick(S, (512, 256, 128))

    def body(x_ref, t_ref, dy_ref, part_ref):
        @pl.when(pl.program_id(0) == 0)
        def _():
            part_ref[...] = jnp.zeros_like(part_ref)

        err = x_ref[...] - t_ref[...]
        dy_ref[...] = err * (1.0 / D)
        part_ref[...] += jnp.sum(err * err, axis=0, keepdims=True)

    row = pl.BlockSpec((ts, D), lambda i: (i, 0))
    return pl.pallas_call(body, out_shape=[jax.ShapeDtypeStruct((S, D), F32), jax.ShapeDtypeStruct((1, D), F32)],
                          grid=(S // ts,), in_specs=[row, row], out_specs=[row, pl.BlockSpec((1, D), lambda i: (0, 0))],
                          name="loss_grad", compiler_params=_params("arbitrary"))(xf, target)


def _adamw_math(g, w, m, v):
    m = ADAM_B1 * m + (1.0 - ADAM_B1) * g
    v = ADAM_B2 * v + (1.0 - ADAM_B2) * (g * g)
    m_hat = m / (1.0 - ADAM_B1 ** ADAM_STEP)
    v_hat = v / (1.0 - ADAM_B2 ** ADAM_STEP)
    delta = -ADAM_LR * (m_hat / (jnp.sqrt(v_hat) + ADAM_EPS) + ADAM_WD * w)
    return delta, m, v


def _adamw(g, w, m, v, name):
    Lw, R, C = w.shape
    tr = _pick(R, (256, 128, 64, 32, 16, 8))

    def body(g_ref, w_ref, m_ref, v_ref, d_ref, nm_ref, nv_ref):
        d_ref[0], nm_ref[0], nv_ref[0] = _adamw_math(g_ref[0], w_ref[0], m_ref[0], v_ref[0])

    blk = pl.BlockSpec((1, tr, C), lambda l, i: (l, i, 0))
    sd = jax.ShapeDtypeStruct(w.shape, F32)
    return pl.pallas_call(body, out_shape=[sd] * 3, grid=(Lw, R // tr), in_specs=[blk] * 4, out_specs=[blk] * 3,
                          name=name, compiler_params=_params("parallel", "parallel"))(g, w, m, v)


def _mesh_pos():
    return lax.axis_index("x"), lax.axis_index("y"), lax.axis_index("c")


def _slot(x, y, c):
    return 4 * x + 2 * y + c


class _Comm:
    def __init__(self, arrays):
        self.arrays = list(arrays)

    def sem_shapes(self):
        na = len(self.arrays)
        return [pltpu.SemaphoreType.DMA((7 * na,)), pltpu.SemaphoreType.DMA((7 * na,)), pltpu.SemaphoreType.DMA((na,))]

    def mid(self, ins, outs, sems):
        pass

    def call(self, name):
        na = len(self.arrays)

        def body(*refs):
            ins, outs, sems = refs[:na], refs[na:2 * na], refs[2 * na:]
            self.start(ins, outs, sems)
            self.mid(ins, outs, sems)
            self.finish(ins, outs, sems)

        anyspec = pl.BlockSpec(memory_space=pl.ANY)
        return pl.pallas_call(body, out_shape=self.out_shapes(), in_specs=[anyspec] * na, out_specs=[anyspec] * na,
                              scratch_shapes=self.sem_shapes(), name=name)(*self.arrays)


class _Gather(_Comm):
    def out_shapes(self):
        return [jax.ShapeDtypeStruct((N_DEV,) + b.shape, b.dtype) for b in self.arrays]

    def _copies(self, ins, outs, sems):
        send_sems, recv_sems, local_sems = sems
        x, y, c = _mesh_pos()
        me, sibling = (x, y, c), (x, y, 1 - c)
        chips = [(1 - x, y), (x, 1 - y), (1 - x, 1 - y)]
        per = []
        for a in range(len(self.arrays)):
            def copy(k, block, to, src=None, a=a):
                dst = outs[a].at[_slot(*block)]
                return pltpu.make_async_remote_copy(
                    src_ref=dst if src is None else src, dst_ref=dst,
                    send_sem=send_sems.at[7 * a + k], recv_sem=recv_sems.at[7 * a + k],
                    device_id=to, device_id_type=MESH_ID)

            per.append(dict(
                mine=pltpu.make_async_copy(ins[a], outs[a].at[_slot(*me)], local_sems.at[a]),
                first=[copy(0, me, sibling, src=ins[a])] + [copy(1 + j, me, (*ch, c), src=ins[a]) for j, ch in enumerate(chips)],
                passed=[copy(4 + j, (*ch, c), sibling) for j, ch in enumerate(chips)],
                over_ici=[copy(1 + j, (*ch, c), me) for j, ch in enumerate(chips)],
                from_sibling=[copy(0, sibling, me)] + [copy(4 + j, (*ch, 1 - c), me) for j, ch in enumerate(chips)]))
        return per

    def start(self, ins, outs, sems):
        for p in self._copies(ins, outs, sems):
            p["mine"].start()
            for cp in p["first"]:
                cp.start()

    def mid(self, ins, outs, sems):
        for p in self._copies(ins, outs, sems):
            for arrived, onward in zip(p["over_ici"], p["passed"]):
                arrived.wait_recv()
                onward.start()

    def finish(self, ins, outs, sems):
        for p in self._copies(ins, outs, sems):
            for cp in p["from_sibling"]:
                cp.wait_recv()
            for cp in p["first"] + p["passed"]:
                cp.wait_send()
            p["mine"].wait()


class _Exchange(_Comm):
    def out_shapes(self):
        return [jax.ShapeDtypeStruct(f.shape, f.dtype) for f in self.arrays]

    def _copies(self, ins, outs, sems):
        send_sems, recv_sems, local_sems = sems
        x, y, c = _mesh_pos()
        my_slot = _slot(x, y, c)
        local, remote = [], []
        for a in range(len(self.arrays)):
            local.append(pltpu.make_async_copy(ins[a].at[my_slot], outs[a].at[my_slot], local_sems.at[a]))
            for k in range(1, N_DEV):
                px = 1 - x if k & 4 else x
                py = 1 - y if k & 2 else y
                pc = 1 - c if k & 1 else c
                remote.append(pltpu.make_async_remote_copy(
                    src_ref=ins[a].at[_slot(px, py, pc)], dst_ref=outs[a].at[my_slot],
                    send_sem=send_sems.at[7 * a + k - 1], recv_sem=recv_sems.at[7 * a + k - 1],
                    device_id=(px, py, pc), device_id_type=MESH_ID))
        return local, remote

    def start(self, ins, outs, sems):
        local, remote = self._copies(ins, outs, sems)
        for cp in local + remote:
            cp.start()

    def finish(self, ins, outs, sems):
        local, remote = self._copies(ins, outs, sems)
        for cp in remote + local:
            cp.wait()


def _sum_slots(parts, name):
    _, R, C = parts.shape
    tr = _pick(R, (256, 128, 64, 32, 16, 8))

    def body(p_ref, o_ref):
        g = p_ref[0].astype(F32)
        for s in range(1, N_DEV):
            g = g + p_ref[s].astype(F32)
        o_ref[...] = g

    return pl.pallas_call(body, out_shape=jax.ShapeDtypeStruct((R, C), F32), grid=(R // tr,),
                          in_specs=[pl.BlockSpec((N_DEV, tr, C), lambda i: (0, i, 0))],
                          out_specs=pl.BlockSpec((tr, C), lambda i: (i, 0)), name=name,
                          compiler_params=_params("parallel"))(parts)


_hg_fwd_op, _hg_rev_op = _make_hg_scan(False), _make_hg_scan(True)
_ret_fwd_op, _ret_rev_op = _make_ret_scan(False), _make_ret_scan(True)
_rope_q, _rope_k = _make_rope(1.0, "rope_q"), _make_rope(RET_DK ** -0.5, "rope_k")
_hg_post = _make_gnorm(2, True, HG_D, False, 1.0, "hg_post")
_ret_post = _make_gnorm(2, True, RET_DV, True, 1.0, "ret_post")
_q_norm = _make_gnorm(1, False, DIL_HD, False, DIL_HD ** -0.5, "dil_qnorm")
_k_norm = _make_gnorm(1, False, DIL_HD, False, 1.0, "dil_knorm")
_att_ops = {dil: _make_attention(dil) for _, dil in DIL_GROUPS}


def _to_heads(t, d):
    S, W = t.shape
    return t.reshape(S, W // d, d).transpose(1, 0, 2)


def _dilated_mixer(parts, rel_bias, q_gain, k_gain):
    S = parts[0].shape[0]
    qg, kg = jnp.tile(q_gain, DIL_SLOTS), jnp.tile(k_gain, DIL_SLOTS)
    ii = np.arange(ATT_SB)[:, None]
    jj = np.arange(ATT_WIN)[None, :]
    outs, lses = [], []
    for g, (window, dil) in enumerate(DIL_GROUPS):
        assert window // (2 * dil) == ATT_HALO
        L = S // dil

        def to_res(t):
            return t.reshape(L, dil, DIL_SLOTS, DIL_HD).transpose(2, 1, 0, 3).reshape(DIL_SLOTS * dil, L, DIL_HD)

        def from_res(t):
            return t.reshape(DIL_SLOTS, dil, L, DIL_HD).transpose(0, 2, 1, 3).reshape(DIL_SLOTS * S, DIL_HD)

        onehot = (_t5_bucket((jj - ATT_HALO - ii) * dil)[:, :, None] == np.arange(REL_BUCKETS)).astype(np.float32)
        bias = jnp.einsum("ijb,bh->hij", onehot, rel_bias[:, g * DIL_SLOTS:(g + 1) * DIL_SLOTS],
                          precision=lax.Precision.HIGHEST)
        q = _q_norm(parts[3 * g], qg)
        k = _k_norm(parts[3 * g + 1], kg)
        o, lse = _att_ops[dil](to_res(q), to_res(k), to_res(parts[3 * g + 2]), bias)
        outs.append(from_res(o))
        lses.append(from_res(lse))
    merged = dil_merge(*outs, *lses)
    return merged.reshape(DIL_SLOTS, S, DIL_HD).transpose(1, 0, 2).reshape(S, DILW)


def _mixers(h, p):
    offs = np.cumsum(IN_SPLITS)[:-1].tolist()
    parts = jnp.split(h, offs, axis=-1)
    q, v = parts[0], parts[1]
    y_a = _hg_post(_hg_fwd_op(q, v, parts[2], p["lb_fwd"]), _hg_rev_op(q, v, parts[3], p["lb_bwd"]), p["hg_norm"], parts[4])
    qh = _to_heads(_rope_q(parts[5]), RET_DK)
    kh = _to_heads(_rope_k(parts[6]), RET_DK)
    y_b = _ret_post(_ret_fwd_op(qh, kh, parts[7]), _ret_rev_op(qh, kh, parts[7]), p["ret_norm"], parts[8])
    y_c = _dilated_mixer(parts[9:], p["rel_bias"], p["q_norm"], p["k_norm"])
    return jnp.concatenate([y_a, y_b, y_c], axis=-1)


def _col_full(blocks):
    return blocks.transpose(1, 0, 2).reshape(blocks.shape[1], -1)


def _col_blocks(full):
    K = full.shape[0]
    return full.reshape(K, N_DEV, -1).transpose(1, 0, 2)


def _row_full(blocks):
    return blocks.reshape(-1, blocks.shape[2])


def _row_blocks(full):
    return full.reshape(N_DEV, -1, full.shape[1])


SMALL_NAMES = ("norm_mix", "norm_mlp", "hg_lb_fwd", "hg_lb_bwd", "hg_norm", "ret_norm", "q_norm", "k_norm", "rel_bias")


def _forward(x, w_in, shards, small):
    depth = len(shards)
    lb_f = jnp.cumsum(jax.nn.softmax(small["hg_lb_fwd"], axis=0), axis=0)
    lb_b = jnp.cumsum(jax.nn.softmax(small["hg_lb_bwd"], axis=0), axis=0)
    for l in range(depth):
        p = {k: small[k][l] for k in ("norm_mix", "norm_mlp", "hg_norm", "ret_norm", "q_norm", "k_norm")}
        p["lb_fwd"], p["lb_bwd"] = lb_f[l] - lb_f[0], lb_b[l] - lb_b[0]
        p["rel_bias"] = small["rel_bias"]
        sh = shards[l]
        h, (g_up, g_out) = norm_matmul(x, p["norm_mix"], w_in, (sh["up"], sh["out"]))
        x, _ = out_proj(_mixers(h, p), _row_full(g_out), x, ())
        halves = ()
        if l < depth - 1:
            nxt = shards[l + 1]["in"]
            halves = (nxt[:nxt.shape[0] // 2], nxt[nxt.shape[0] // 2:])
        x, g_in = mlp(x, p["norm_mlp"], _col_full(g_up), sh["down"], halves)
        if halves:
            w_in = jnp.concatenate([_col_full(t) for t in g_in], axis=0)
    return x


def kernel(x, w_in, w_out, w_up, w_down, norm_mix, norm_mlp, hg_lb_fwd, hg_lb_bwd, hg_norm, ret_norm, q_norm, k_norm, rel_bias, loss_target, m_w_in, m_w_out, m_w_up, m_w_down, m_norm_mix, m_norm_mlp, m_hg_lb_fwd, m_hg_lb_bwd, m_hg_norm, m_ret_norm, m_q_norm, m_k_norm, m_rel_bias, v_w_in, v_w_out, v_w_up, v_w_down, v_norm_mix, v_norm_mlp, v_hg_lb_fwd, v_hg_lb_bwd, v_hg_norm, v_ret_norm, v_q_norm, v_k_norm, v_rel_bias):
    depth = w_in.shape[0]
    big = (w_in, w_out, w_up, w_down)
    big_m = (m_w_in, m_w_out, m_w_up, m_w_down)
    big_v = (v_w_in, v_w_out, v_w_up, v_w_down)
    small = dict(zip(SMALL_NAMES, (norm_mix, norm_mlp, hg_lb_fwd, hg_lb_bwd, hg_norm, ret_norm, q_norm, k_norm, rel_bias)))
    small_m = (m_norm_mix, m_norm_mlp, m_hg_lb_fwd, m_hg_lb_bwd, m_hg_norm, m_ret_norm, m_q_norm, m_k_norm, m_rel_bias)
    small_v = (v_norm_mix, v_norm_mlp, v_hg_lb_fwd, v_hg_lb_bwd, v_hg_norm, v_ret_norm, v_q_norm, v_k_norm, v_rel_bias)

    (gathered0,) = _Gather([w_in[0].astype(WIRE)]).call("gather_w_in0")
    names = ("in", "out", "up", "down")
    shards = [{n: w[l] for n, w in zip(names, big) if l or n != "in"} for l in range(depth)]

    xf, vjp = jax.vjp(_forward, x[0], _col_full(gathered0), shards, small)
    dy, part = _loss_grad(xf, loss_target[0])
    loss = lax.psum(0.5 / xf.shape[1] * jnp.sum(part), ("x", "y", "c"))
    dx, dw_in0, dshards, dsmall = vjp(dy)

    (landed0,) = _Exchange([_col_blocks(dw_in0)]).call("exchange_w_in0")
    dshards[0]["in"] = _sum_slots(landed0, "sum_w_in0")
    big_out = []
    for i, n in enumerate(names):
        g = jnp.stack([dshards[l][n] for l in range(depth)])
        big_out.append((g,) + tuple(_adamw(g, big[i], big_m[i], big_v[i], "adamw_" + n)))

    flat = jnp.concatenate([dsmall[n].reshape(-1) for n in SMALL_NAMES])
    n_small = flat.shape[0]
    rows = -(-n_small // 1024) * 8
    pad = lambda t: jnp.pad(t, (0, rows * 128 - n_small)).reshape(1, rows, 128)
    (small_parts,) = _Gather([pad(flat)[0]]).call("gather_small_grads")
    cat = lambda ts: pad(jnp.concatenate([t.reshape(-1) for t in ts]))
    g_small = _sum_slots(small_parts, "sum_small_grads")[None]
    small_out = (g_small,) + tuple(_adamw(g_small, cat([small[n] for n in SMALL_NAMES]), cat(small_m), cat(small_v), "adamw_small"))

    def unpack(t):
        t = t.reshape(-1)
        out, off = [], 0
        for n in SMALL_NAMES:
            size = small[n].size
            out.append(t[off:off + size].reshape(small[n].shape))
            off += size
        return out

    res = [loss, dx[None]]
    for kind in range(4):
        res += [o[kind] for o in big_out] + unpack(small_out[kind])
    return tuple(res)
```

```python
import functools
import math

import numpy as np
import jax
import jax.numpy as jnp
from jax import lax
from jax.experimental import pallas as pl
from jax.experimental.pallas import tpu as pltpu

F32 = jnp.float32
MXU = jnp.bfloat16
WIRE = jnp.bfloat16
EPS = 1e-6
N_DEV = 8
VMEM_LIMIT = 48 * 1024 * 1024

HG_HEADS, HG_D = 6, 128
RET_HEADS, RET_DK, RET_DV = 6, 64, 128
DIL_SLOTS, DIL_HD = 4, 128
DIL_GROUPS = ((128, 1), (512, 4), (2048, 16))
HGW = HG_HEADS * HG_D
RETW = RET_HEADS * RET_DV
DILW = DIL_SLOTS * DIL_HD
IN_SPLITS = (HGW, HGW, HGW, HGW, HGW, RET_HEADS * RET_DK, RET_HEADS * RET_DK, RETW, RETW) + (DILW,) * 9
REL_BUCKETS, REL_MAX_DIST = 32, 1024
ROPE_BASE = 10000.0
ADAM_LR, ADAM_B1, ADAM_B2, ADAM_EPS, ADAM_WD, ADAM_STEP = 0.001, 0.9, 0.999, 1e-08, 0.01, 10

SCAN_C = 128
SCAN_HPS = 6
HG_SUB = 16
EXP_CLAMP = 60.0
ATT_SB, ATT_HALO = 128, 64
ATT_WIN = ATT_SB + 2 * ATT_HALO
MESH_ID = pl.DeviceIdType.MESH


def _pick(n, prefs):
    for p in prefs:
        if n % p == 0:
            return p
    return n


def _params(*sem):
    return pltpu.CompilerParams(dimension_semantics=sem, vmem_limit_bytes=VMEM_LIMIT)


def _dot(a, b):
    return lax.dot_general(a.astype(MXU), b.astype(MXU), (((1,), (0,)), ((), ())), preferred_element_type=F32)


def _dot_nt(a, b):
    return lax.dot_general(a.astype(MXU), b.astype(MXU), (((1,), (1,)), ((), ())), preferred_element_type=F32)


def _dot_tn(a, b):
    return lax.dot_general(a.astype(MXU), b.astype(MXU), (((0,), (0,)), ((), ())), preferred_element_type=F32)


def _dot01(sel, x):
    if MXU == F32:
        return _dot(sel, x)
    hi = x.astype(MXU)
    r1 = x - hi.astype(F32)
    mid = r1.astype(MXU)
    lo = (r1 - mid.astype(F32)).astype(MXU)
    return _dot(sel, hi) + _dot(sel, mid) + _dot(sel, lo)


def _mm(a, b, *, nt=False, ta=False, out_dtype=F32, res=None, u_in=None, emit_act=False, rider=None, name):
    M, K = a.shape[::-1] if ta else a.shape
    N = b.shape[0] if nt else b.shape[1]
    tm = _pick(M, (1024, 512, 256, 128))
    tn = _pick(N, (1024, 768, 512, 384, 256, 128))
    tk = _pick(K, (2048, 1536, 1024, 512, 256, 128))
    ni, nj, nk = M // tm, N // tn, K // tk
    n_ride = len(rider.arrays) if rider is not None else 0

    def body(*refs):
        it = iter(refs)
        a_ref, b_ref = next(it), next(it)
        res_ref = next(it) if res is not None else None
        u_ref = next(it) if u_in is not None else None
        ride_in = [next(it) for _ in range(n_ride)]
        o_ref = next(it)
        act_ref = next(it) if emit_act else None
        ride_out = [next(it) for _ in range(n_ride)]
        acc_ref = next(it)
        sems = list(it)
        i, j, k = pl.program_id(0), pl.program_id(1), pl.program_id(2)

        step = (i * nj + j) * nk + k
        if rider is not None:
            pl.when(step == 0)(functools.partial(rider.start, ride_in, ride_out, sems))
            pl.when(step == (3 * ni // 4) * nj * nk)(functools.partial(rider.mid, ride_in, ride_out, sems))

        @pl.when(k == 0)
        def _():
            acc_ref[...] = jnp.zeros_like(acc_ref)

        acc_ref[...] += (_dot_tn if ta else _dot_nt if nt else _dot)(a_ref[...], b_ref[...])

        @pl.when(k == nk - 1)
        def _():
            r = acc_ref[...]
            if res_ref is not None:
                r = r + res_ref[...]
            if u_ref is not None:
                r = r * (2.0 * jnp.maximum(u_ref[...], 0.0))
            o_ref[...] = r.astype(o_ref.dtype)
            if act_ref is not None:
                t = jnp.maximum(r, 0.0)
                act_ref[...] = (t * t).astype(act_ref.dtype)

        if rider is not None:
            pl.when(step == ni * nj * nk - 1)(functools.partial(rider.finish, ride_in, ride_out, sems))

    mn = pl.BlockSpec((tm, tn), lambda i, j, k: (i, j))
    anyspec = pl.BlockSpec(memory_space=pl.ANY)
    in_specs = [pl.BlockSpec((tk, tm), lambda i, j, k: (k, i)) if ta else pl.BlockSpec((tm, tk), lambda i, j, k: (i, k)),
                pl.BlockSpec((tn, tk), lambda i, j, k: (j, k)) if nt else pl.BlockSpec((tk, tn), lambda i, j, k: (k, j))]
    args = [a, b]
    for extra in (res, u_in):
        if extra is not None:
            in_specs.append(mn)
            args.append(extra)
    out_shape = [jax.ShapeDtypeStruct((M, N), out_dtype)]
    out_specs = [mn]
    if emit_act:
        out_shape.append(jax.ShapeDtypeStruct((M, N), MXU))
        out_specs.append(mn)
    scratch = [pltpu.VMEM((tm, tn), F32)]
    if rider is not None:
        in_specs += [anyspec] * n_ride
        args += list(rider.arrays)
        out_shape += rider.out_shapes()
        out_specs += [anyspec] * n_ride
        scratch += rider.sem_shapes()
    sem = ("arbitrary",) * 3 if rider is not None else ("parallel", "parallel", "arbitrary")
    out = pl.pallas_call(
        body, out_shape=out_shape, grid=(ni, nj, nk), in_specs=in_specs, out_specs=out_specs,
        scratch_shapes=scratch, name=name, compiler_params=_params(*sem))(*args)
    return out if (emit_act or rider is not None) else out[0]


def _gnorm_stats(x, center):
    if center:
        x = x - jnp.mean(x, axis=-1, keepdims=True)
    r = lax.rsqrt(jnp.mean(x * x, axis=-1, keepdims=True) + EPS)
    return x * r, r


def _silu_parts(gt):
    sg = jax.nn.sigmoid(gt)
    return gt * sg, sg * (1.0 + gt * (1.0 - sg))


def _gnorm_fwd(xs, gain, gate, *, group, center, scale, out_dtype, name):
    S, W = xs[0].shape
    ts = _pick(S, (512, 256, 128))
    nx = len(xs)

    def body(*refs):
        x_refs, g_ref = refs[:nx], refs[nx]
        gate_ref = refs[nx + 1] if gate is not None else None
        o_ref = refs[-1]
        for gi in range(W // group):
            sl = slice(gi * group, (gi + 1) * group)
            x = x_refs[0][:, sl]
            for xr in x_refs[1:]:
                x = x + xr[:, sl]
            n, _ = _gnorm_stats(x, center)
            y = n * (g_ref[:, sl] * scale)
            if gate_ref is not None:
                y = y * _silu_parts(gate_ref[:, sl])[0]
            o_ref[:, sl] = y.astype(o_ref.dtype)

    row = pl.BlockSpec((ts, W), lambda i: (i, 0))
    vec = pl.BlockSpec((1, W), lambda i: (0, 0))
    args = list(xs) + [gain] + ([gate] if gate is not None else [])
    in_specs = [row] * nx + [vec] + ([row] if gate is not None else [])
    return pl.pallas_call(body, out_shape=jax.ShapeDtypeStruct((S, W), out_dtype), grid=(S // ts,),
                          in_specs=in_specs, out_specs=row, name=name, compiler_params=_params("parallel"))(*args)


def _gnorm_bwd(dy, xs, gain, gate, *, group, center, scale, name):
    S, W = xs[0].shape
    ts = _pick(S, (512, 256, 128))
    nx = len(xs)

    def body(*refs):
        dy_ref = refs[0]
        x_refs, g_ref = refs[1:1 + nx], refs[1 + nx]
        gate_ref = refs[2 + nx] if gate is not None else None
        outs = refs[(3 + nx if gate is not None else 2 + nx):]
        dx_ref, dg_ref = outs[0], outs[1]
        dgate_ref = outs[2] if gate is not None else None

        @pl.when(pl.program_id(0) == 0)
        def _():
            dg_ref[...] = jnp.zeros_like(dg_ref)

        for gi in range(W // group):
            sl = slice(gi * group, (gi + 1) * group)
            x = x_refs[0][:, sl]
            for xr in x_refs[1:]:
                x = x + xr[:, sl]
            n, r = _gnorm_stats(x, center)
            dyv = dy_ref[:, sl].astype(F32)
            g = g_ref[:, sl] * scale
            if gate_ref is not None:
                act, dact = _silu_parts(gate_ref[:, sl])
                dgate_ref[:, sl] = dyv * n * g * dact
                dyv = dyv * act
            dg_ref[:, sl] += jnp.sum(dyv * n, axis=0, keepdims=True) * scale
            dn = dyv * g
            t = dn - n * jnp.mean(dn * n, axis=-1, keepdims=True)
            if center:
                t = t - jnp.mean(dn, axis=-1, keepdims=True)
            dx_ref[:, sl] = r * t

    row = pl.BlockSpec((ts, W), lambda i: (i, 0))
    vec = pl.BlockSpec((1, W), lambda i: (0, 0))
    args = [dy] + list(xs) + [gain] + ([gate] if gate is not None else [])
    in_specs = [row] * (1 + nx) + [vec] + ([row] if gate is not None else [])
    out_shape = [jax.ShapeDtypeStruct((S, W), F32), jax.ShapeDtypeStruct((1, W), F32)]
    out_specs = [row, vec]
    if gate is not None:
        out_shape.append(jax.ShapeDtypeStruct((S, W), F32))
        out_specs.append(row)
    out = pl.pallas_call(body, out_shape=out_shape, grid=(S // ts,), in_specs=in_specs, out_specs=out_specs,
                         name=name, compiler_params=_params("arbitrary"))(*args)
    return out[0], out[1], (out[2] if gate is not None else None)


def _head_norms(xs, gains, scales, dys, name):
    n = len(xs)
    S, W = xs[0].shape
    ts = _pick(S, (256, 128))
    bwd = dys is not None

    def body(*refs):
        x_refs, g_refs = refs[:n], refs[n:2 * n]
        dy_refs = refs[2 * n:3 * n] if bwd else ()
        outs = refs[(3 * n if bwd else 2 * n):]
        if bwd:
            @pl.when(pl.program_id(0) == 0)
            def _():
                for dg_ref in outs[n:]:
                    dg_ref[...] = jnp.zeros_like(dg_ref)

        for a in range(n):
            for gi in range(W // DIL_HD):
                sl = slice(gi * DIL_HD, (gi + 1) * DIL_HD)
                nrm, r = _gnorm_stats(x_refs[a][:, sl], False)
                g = g_refs[a][:, sl] * scales[a]
                if not bwd:
                    outs[a][:, sl] = nrm * g
                    continue
                dyv = dy_refs[a][:, sl]
                outs[n + a][:, sl] += jnp.sum(dyv * nrm, axis=0, keepdims=True) * scales[a]
                dn = dyv * g
                outs[a][:, sl] = r * (dn - nrm * jnp.mean(dn * nrm, axis=-1, keepdims=True))

    row = pl.BlockSpec((ts, W), lambda i: (i, 0))
    vec = pl.BlockSpec((1, W), lambda i: (0, 0))
    sd = jax.ShapeDtypeStruct((S, W), F32)
    out_shape = [sd] * n + ([jax.ShapeDtypeStruct((1, W), F32)] * n if bwd else [])
    return pl.pallas_call(
        body, out_shape=out_shape, grid=(S // ts,), in_specs=[row] * n + [vec] * n + ([row] * n if bwd else []),
        out_specs=[row] * n + ([vec] * n if bwd else []), name=name,
        compiler_params=_params("arbitrary" if bwd else "parallel"))(*xs, *gains, *(dys or ()))


def _make_head_norms(scales, tag):
    n = len(scales)

    def apply(*args):
        return tuple(_head_norms(args[:n], [g[None, :] for g in args[n:]], scales, None, tag))

    op = jax.custom_vjp(apply)

    def fwd(*args):
        return apply(*args), args

    def bwd(args, dys):
        out = _head_norms(args[:n], [g[None, :] for g in args[n:]], scales, list(dys), tag + "_bwd")
        return tuple(out[:n]) + tuple(d[0] for d in out[n:])

    op.defvjp(fwd, bwd)
    return op


def _make_gnorm(nx, has_gate, group, center, scale, tag):
    kw = dict(group=group, center=center, scale=scale)

    @jax.custom_vjp
    def op(*args):
        return fwd(*args)[0]

    def fwd(*args):
        xs, gain = args[:nx], args[nx]
        gate = args[nx + 1] if has_gate else None
        y = _gnorm_fwd(xs, gain[None, :], gate, out_dtype=F32, name=tag + "_fwd", **kw)
        return y, args

    def bwd(args, dy):
        xs, gain = args[:nx], args[nx]
        gate = args[nx + 1] if has_gate else None
        dx, dg, dgate = _gnorm_bwd(dy, xs, gain[None, :], gate, name=tag + "_bwd", **kw)
        return (dx,) * nx + (dg[0],) + ((dgate,) if has_gate else ())

    op.defvjp(fwd, bwd)
    return op


def _ride(cls, arrays):
    return cls(list(arrays)) if len(arrays) else None


def _mm_ride(*args, rider, name, **kw):
    if rider is None:
        return _mm(*args, name=name, **kw), ()
    out = _mm(*args, rider=rider, name=name + "_ride", **kw)
    n = len(rider.arrays)
    main = out[:-n]
    return (main[0] if len(main) == 1 else tuple(main)), tuple(out[-n:])


def _landed_sums(landed, tag):
    return tuple(_sum_slots(l, "%s_sum%d" % (tag, i)) for i, l in enumerate(landed))


def _rms(x, g, name):
    return _gnorm_fwd([x], g[None, :], None, group=x.shape[1], center=False, scale=1.0, out_dtype=MXU, name=name)


def _rms_bwd(dxn, x, g, name):
    dx, dg, _ = _gnorm_bwd(dxn, [x], g[None, :], None, group=x.shape[1], center=False, scale=1.0, name=name)
    return dx, dg[0]


def _wire(shards):
    return [t.astype(WIRE) for t in shards]


@jax.custom_vjp
def norm_matmul(x, g, w, shards):
    return _norm_matmul_fwd(x, g, w, shards)[0]


def _norm_matmul_fwd(x, g, w, shards):
    xn = _rms(x, g, "rms_in")
    h, gathered = _mm_ride(xn, w, rider=_ride(_Gather, _wire(shards)), name="mm_in")
    return (h, gathered), (x, g, w, xn)


def _norm_matmul_bwd(saved, cts):
    x, g, w, xn = saved
    dh, d_gathered = cts
    dxn, l0 = _mm_ride(dh, w, nt=True, rider=_ride(_Exchange, d_gathered[:1]), name="mm_in_dx")
    dw, l1 = _mm_ride(xn, dh, ta=True, out_dtype=w.dtype, rider=_ride(_Exchange, d_gathered[1:]), name="mm_in_dw")
    dx, dg = _rms_bwd(dxn, x, g, "rms_in_bwd")
    return dx, dg, dw, _landed_sums(l0 + l1, "in")


norm_matmul.defvjp(_norm_matmul_fwd, _norm_matmul_bwd)


@jax.custom_vjp
def out_proj(y, w, x, shards):
    return _out_proj_fwd(y, w, x, shards)[0]


def _out_proj_fwd(y, w, x, shards):
    yb = y.astype(MXU)
    out, gathered = _mm_ride(yb, w, res=x, rider=_ride(_Gather, _wire(shards)), name="mm_out")
    return (out, gathered), (yb, w)


def _out_proj_bwd(saved, cts):
    yb, w = saved
    dout, d_gathered = cts
    db = dout.astype(MXU)
    dy, l0 = _mm_ride(db, w, nt=True, rider=_ride(_Exchange, d_gathered[:1]), name="mm_out_dy")
    dw, l1 = _mm_ride(yb, db, ta=True, out_dtype=w.dtype, rider=_ride(_Exchange, d_gathered[1:]), name="mm_out_dw")
    return dy, dw, dout, _landed_sums(l0 + l1, "out")


out_proj.defvjp(_out_proj_fwd, _out_proj_bwd)


@jax.custom_vjp
def mlp(x, g, w_up, s_down, shards):
    return _mlp_fwd(x, g, w_up, s_down, shards)[0]


def _mlp_fwd(x, g, w_up, s_down, shards):
    hm = _rms(x, g, "rms_mlp")
    (u, act), (g_down,) = _mm_ride(hm, w_up, emit_act=True, rider=_Gather(_wire([s_down])), name="mm_up")
    w_down = _row_full(g_down)
    out, gathered = _mm_ride(act, w_down, res=x, rider=_ride(_Gather, _wire(shards)), name="mm_down")
    return (out, gathered), (x, g, w_up, w_down, hm, u, act)


def _mlp_bwd(saved, cts):
    x, g, w_up, w_down, hm, u, act = saved
    dout, d_gathered = cts
    db = dout.astype(MXU)
    du, l0 = _mm_ride(db, w_down, nt=True, u_in=u, out_dtype=MXU, rider=_ride(_Exchange, d_gathered[:1]), name="mm_down_da")
    dw_down, l1 = _mm_ride(act, db, ta=True, out_dtype=w_down.dtype, rider=_ride(_Exchange, d_gathered[1:]), name="mm_down_dw")
    dhm, l_down = _mm_ride(du, w_up, nt=True, rider=_Exchange([_row_blocks(dw_down)]), name="mm_up_dx")
    dw_up = _mm(hm, du, ta=True, out_dtype=w_up.dtype, name="mm_up_dw")
    dx, dg = _rms_bwd(dhm, x, g, "rms_mlp_bwd")
    return dout + dx, dg, dw_up, _landed_sums(l_down, "down")[0], _landed_sums(l0 + l1, "mlp")


mlp.defvjp(_mlp_fwd, _mlp_bwd)


def _order(C, rev):
    i = np.arange(C)
    return (C - 1 - i) if rev else i


def _hg_constants(C, rev):
    p = _order(C, rev)
    pi, pj = p[:, None], p[None, :]
    tri = (pj <= pi).astype(np.float32)
    masks = [((pi // HG_SUB) == (pj // HG_SUB)) & (pj <= pi)]
    h = HG_SUB
    halves = []
    while h < C:
        masks.append(((pi // (2 * h)) == (pj // (2 * h))) & ((pi // h) % 2 == 1) & ((pj // h) % 2 == 0))
        halves.append(h)
        h *= 2
    return tri, np.stack(masks).astype(np.float32), halves


def _hg_tables(b_scr, C, h, rev):
    nb = C // h
    g_rows, e_rows = [], []
    for rb in range(nb):
        if rev:
            e = b_scr[pl.ds(rb * h, 1), :]
            g = b_scr[pl.ds((rb + 1) * h, 1), :] if rb < nb - 1 else None
        else:
            e = b_scr[pl.ds((rb + 1) * h - 1, 1), :]
            g = b_scr[pl.ds(rb * h - 1, 1), :] if rb >= 1 else None
        e_rows.append(jnp.broadcast_to(e, (h, HG_D)))
        g_rows.append(jnp.zeros((h, HG_D), F32) if g is None else jnp.broadcast_to(g, (h, HG_D)))
    return jnp.concatenate(g_rows, axis=0), jnp.concatenate(e_rows, axis=0)


def _hg_gates(z, lb):
    sg = jax.nn.sigmoid(z)
    f = lb + (1.0 - lb) * sg
    return sg, f, 1.0 - f, jnp.log(f)


def _hg_exponents(b, b_scr, C, halves, rev):
    g0, _ = _hg_tables(b_scr, C, HG_SUB, rev)
    p0 = b - g0
    out = [(p0, jnp.minimum(-p0, EXP_CLAMP))]
    for h in halves:
        g, e = _hg_tables(b_scr, C, h, rev)
        out.append((jnp.minimum(b - g, 0.0), jnp.minimum(e - b, 0.0)))
    return out


def _hg_scan_fwd(q, v, z, lb, rev, name):
    S = q.shape[0]
    C = SCAN_C
    nc = S // C
    tri, masks, halves = _hg_constants(C, rev)
    nlev = masks.shape[0]
    end_row = 0 if rev else C - 1
    vT = v.astype(MXU).T

    def body(q_ref, v_ref, vT_ref, z_ref, lb_ref, tri_ref, mask_ref, o_ref, st_ref, s_scr, b_scr):
        @pl.when(pl.program_id(1) == 0)
        def _():
            s_scr[...] = jnp.zeros_like(s_scr)

        for hh in range(SCAN_HPS):
            sl = slice(hh * HG_D, (hh + 1) * HG_D)
            bh = b_scr.at[hh]
            _, f, k, lf = _hg_gates(z_ref[:, sl], lb_ref[:, sl])
            b = _dot01(tri_ref[...], lf)
            bh[...] = b
            qv, vv = q_ref[:, sl], v_ref[:, sl]
            st = s_scr[hh]
            st_ref[hh, 0] = st
            a = jnp.zeros((C, C), F32)
            for lv, (eq, ek) in enumerate(_hg_exponents(b, bh, C, halves, rev)):
                a = a + mask_ref[lv] * _dot_nt(qv * jnp.exp(eq), k * jnp.exp(ek))
            bend = bh[pl.ds(end_row, 1), :]
            o_ref[:, sl] = _dot(a, vv) + _dot_nt(qv * jnp.exp(b), st)
            s_scr[hh] = st * jnp.exp(bend) + _dot(vT_ref[sl, :], k * jnp.exp(bend - b))

    cidx = (lambda c: nc - 1 - c) if rev else (lambda c: c)
    wid = SCAN_HPS * HG_D
    blk = pl.BlockSpec((C, wid), lambda h, c: (cidx(c), h))
    o, states = pl.pallas_call(
        body,
        out_shape=[jax.ShapeDtypeStruct((S, HGW), F32), jax.ShapeDtypeStruct((HG_HEADS, nc, HG_D, HG_D), F32)],
        grid=(HG_HEADS // SCAN_HPS, nc),
        in_specs=[blk, blk, pl.BlockSpec((wid, C), lambda h, c: (h, cidx(c))), blk,
                  pl.BlockSpec((1, wid), lambda h, c: (0, h)),
                  pl.BlockSpec((C, C), lambda h, c: (0, 0)), pl.BlockSpec((nlev, C, C), lambda h, c: (0, 0, 0))],
        out_specs=[blk, pl.BlockSpec((SCAN_HPS, 1, HG_D, HG_D), lambda h, c: (h, cidx(c), 0, 0))],
        scratch_shapes=[pltpu.VMEM((SCAN_HPS, HG_D, HG_D), F32), pltpu.VMEM((SCAN_HPS, C, HG_D), F32)],
        name=name, compiler_params=_params("parallel", "arbitrary"),
    )(q, v, vT, z, lb, jnp.asarray(tri, MXU), jnp.asarray(masks))
    return o, states


def _hg_scan_bwd(q, v, z, lb, states, do, rev, name):
    S = q.shape[0]
    C = SCAN_C
    nc = S // C
    tri, masks, halves = _hg_constants(C, rev)
    nlev = masks.shape[0]
    end_row = 0 if rev else C - 1
    masks_t = np.ascontiguousarray(np.transpose(masks, (0, 2, 1)))
    dob = do.astype(MXU)

    def body(q_ref, v_ref, z_ref, lb_ref, do_ref, doT_ref, st_ref, tri_ref, triT_ref, mask_ref, maskT_ref,
             dq_ref, dv_ref, dz_ref, dlb_ref, dn_scr, b_scr):
        @pl.when(pl.program_id(1) == 0)
        def _():
            dn_scr[...] = jnp.zeros_like(dn_scr)
            dlb_ref[...] = jnp.zeros_like(dlb_ref)

        for hh in range(SCAN_HPS):
            sl = slice(hh * HG_D, (hh + 1) * HG_D)
            bh = b_scr.at[hh]
            lb_v = lb_ref[:, sl]
            sg, f, k, lf = _hg_gates(z_ref[:, sl], lb_v)
            b = _dot01(tri_ref[...], lf)
            bh[...] = b
            qv, vv, dov = q_ref[:, sl], v_ref[:, sl], do_ref[:, sl]
            st, dn = st_ref[hh, 0], dn_scr[hh]
            da = _dot_nt(dov, vv)
            da_t = da.T
            a = jnp.zeros((C, C), F32)
            dq = jnp.zeros((C, HG_D), F32)
            dk = jnp.zeros((C, HG_D), F32)
            for lv, (eq, ek) in enumerate(_hg_exponents(b, bh, C, halves, rev)):
                xq, xk = jnp.exp(eq), jnp.exp(ek)
                qs, ks = qv * xq, k * xk
                a = a + mask_ref[lv] * _dot_nt(qs, ks)
                dq = dq + _dot(mask_ref[lv] * da, ks) * xq
                dk = dk + _dot(maskT_ref[lv] * da_t, qs) * xk
            bend = bh[pl.ds(end_row, 1), :]
            xb, xe, xend = jnp.exp(b), jnp.exp(bend - b), jnp.exp(bend)
            dq = dq + _dot(dov, st) * xb
            dk_state = _dot(vv, dn) * xe
            dk = dk + dk_state
            dv_ref[:, sl] = _dot(a.T, dov) + _dot_nt(k * xe, dn)
            dn_scr[hh] = dn * xend + _dot(doT_ref[sl, :], qv * xb)
            extra = jnp.sum(k * dk_state, axis=0, keepdims=True) + xend * jnp.sum(st * dn, axis=0, keepdims=True)
            rows = lax.broadcasted_iota(jnp.int32, (C, HG_D), 0)
            db = qv * dq - k * dk + jnp.where(rows == end_row, extra, 0.0)
            df = _dot01(triT_ref[...], db) / f - dk
            dq_ref[:, sl] = dq
            dz_ref[:, sl] = df * (1.0 - lb_v) * sg * (1.0 - sg)
            dlb_ref[:, sl] += jnp.sum(df * (1.0 - sg), axis=0, keepdims=True)

    cidx = (lambda c: c) if rev else (lambda c: nc - 1 - c)
    wid = SCAN_HPS * HG_D
    blk = pl.BlockSpec((C, wid), lambda h, c: (cidx(c), h))
    vec = pl.BlockSpec((1, wid), lambda h, c: (0, h))
    cc = pl.BlockSpec((C, C), lambda h, c: (0, 0))
    lcc = pl.BlockSpec((nlev, C, C), lambda h, c: (0, 0, 0))
    sd = jax.ShapeDtypeStruct((S, HGW), F32)
    return pl.pallas_call(
        body, out_shape=[sd, sd, sd, jax.ShapeDtypeStruct((1, HGW), F32)], grid=(HG_HEADS // SCAN_HPS, nc),
        in_specs=[blk, blk, blk, vec, blk, pl.BlockSpec((wid, C), lambda h, c: (h, cidx(c))),
                  pl.BlockSpec((SCAN_HPS, 1, HG_D, HG_D), lambda h, c: (h, cidx(c), 0, 0)), cc, cc, lcc, lcc],
        out_specs=[blk, blk, blk, vec],
        scratch_shapes=[pltpu.VMEM((SCAN_HPS, HG_D, HG_D), F32), pltpu.VMEM((SCAN_HPS, C, HG_D), F32)],
        name=name, compiler_params=_params("parallel", "arbitrary"),
    )(q, v, z, lb, dob, dob.T, states, jnp.asarray(tri, MXU), jnp.asarray(tri.T, MXU),
      jnp.asarray(masks), jnp.asarray(masks_t))


def _make_hg_scan(rev):
    tag = "hg_rev" if rev else "hg_fwd"

    @jax.custom_vjp
    def op(q, v, z, lb):
        return fwd(q, v, z, lb)[0]

    def fwd(q, v, z, lb):
        o, states = _hg_scan_fwd(q, v, z, lb[None, :], rev, tag)
        return o, (q, v, z, lb, states)

    def bwd(saved, do):
        q, v, z, lb, states = saved
        dq, dv, dz, dlb = _hg_scan_bwd(q, v, z, lb[None, :], states, do, rev, tag + "_bwd")
        return dq, dv, dz, dlb[0]

    op.defvjp(fwd, bwd)
    return op


def _ret_constants(C, rev):
    hidx = np.arange(RET_HEADS, dtype=np.float64)
    lg = np.log1p(-np.exp2(-5.0 - hidx))
    if rev:
        lg = lg[::-1]
    p = _order(C, rev).astype(np.float64)
    rel = p[:, None] - p[None, :]
    dmat = np.where(rel >= 0, np.exp(lg[:, None, None] * np.maximum(rel, 0.0)), 0.0)
    xi = np.exp(lg[:, None] * (p[None, :] + 1.0))
    zeta = np.exp(lg[:, None] * (C - 1.0 - p[None, :]))
    gc = np.exp(lg * C)
    bc = lambda t: np.ascontiguousarray(np.broadcast_to(t[:, :, None], (RET_HEADS, C, RET_DK))).astype(np.float32)
    gcb = np.ascontiguousarray(np.broadcast_to(gc[:, None, None], (RET_HEADS, 1, RET_DK))).astype(np.float32)
    return dmat.astype(np.float32), bc(xi), bc(zeta), gcb


def _ret_scan_fwd(qh, kh, v, rev, name):
    S = v.shape[0]
    C = SCAN_C
    nc = S // C
    dmat, xi, zeta, gc = _ret_constants(C, rev)
    vb = v.astype(MXU)

    def body(q_ref, k_ref, v_ref, vT_ref, d_ref, xi_ref, zeta_ref, gc_ref, o_ref, st_ref, s_scr):
        @pl.when(pl.program_id(1) == 0)
        def _():
            s_scr[...] = jnp.zeros_like(s_scr)

        for hh in range(SCAN_HPS):
            sl = slice(hh * RET_DV, (hh + 1) * RET_DV)
            qv, kv = q_ref[hh], k_ref[hh]
            st = s_scr[hh]
            st_ref[hh, 0] = st
            sc = _dot_nt(qv, kv) * d_ref[hh]
            o_ref[:, sl] = _dot(sc, v_ref[:, sl]) + _dot_nt(qv * xi_ref[hh], st)
            s_scr[hh] = st * gc_ref[hh] + _dot(vT_ref[sl, :], kv * zeta_ref[hh])

    cidx = (lambda c: nc - 1 - c) if rev else (lambda c: c)
    hk = pl.BlockSpec((SCAN_HPS, C, RET_DK), lambda h, c: (h, cidx(c), 0))
    vblk = pl.BlockSpec((C, SCAN_HPS * RET_DV), lambda h, c: (cidx(c), h))
    tab = pl.BlockSpec((SCAN_HPS, C, RET_DK), lambda h, c: (h, 0, 0))
    return pl.pallas_call(
        body,
        out_shape=[jax.ShapeDtypeStruct((S, RETW), F32), jax.ShapeDtypeStruct((RET_HEADS, nc, RET_DV, RET_DK), F32)],
        grid=(RET_HEADS // SCAN_HPS, nc),
        in_specs=[hk, hk, vblk, pl.BlockSpec((SCAN_HPS * RET_DV, C), lambda h, c: (h, cidx(c))),
                  pl.BlockSpec((SCAN_HPS, C, C), lambda h, c: (h, 0, 0)), tab, tab,
                  pl.BlockSpec((SCAN_HPS, 1, RET_DK), lambda h, c: (h, 0, 0))],
        out_specs=[vblk, pl.BlockSpec((SCAN_HPS, 1, RET_DV, RET_DK), lambda h, c: (h, cidx(c), 0, 0))],
        scratch_shapes=[pltpu.VMEM((SCAN_HPS, RET_DV, RET_DK), F32)],
        name=name, compiler_params=_params("parallel", "arbitrary"),
    )(qh, kh, v, vb.T, jnp.asarray(dmat), jnp.asarray(xi), jnp.asarray(zeta), jnp.asarray(gc))


def _ret_scan_bwd(qh, kh, v, states, do, rev, name):
    S = v.shape[0]
    C = SCAN_C
    nc = S // C
    dmat, xi, zeta, gc = _ret_constants(C, rev)
    dob = do.astype(MXU)

    def body(q_ref, k_ref, v_ref, do_ref, doT_ref, st_ref, d_ref, xi_ref, zeta_ref, gc_ref,
             dq_ref, dk_ref, dv_ref, dn_scr):
        @pl.when(pl.program_id(1) == 0)
        def _():
            dn_scr[...] = jnp.zeros_like(dn_scr)

        for hh in range(SCAN_HPS):
            sl = slice(hh * RET_DV, (hh + 1) * RET_DV)
            qv, kv, vv, dov = q_ref[hh], k_ref[hh], v_ref[:, sl], do_ref[:, sl]
            st, dn = st_ref[hh, 0], dn_scr[hh]
            dm = d_ref[hh]
            sc = _dot_nt(qv, kv) * dm
            dsc = _dot_nt(dov, vv) * dm
            kz = kv * zeta_ref[hh]
            dq_ref[hh] = _dot(dsc, kv) + _dot(dov, st) * xi_ref[hh]
            dk_ref[hh] = _dot(dsc.T, qv) + _dot(vv, dn) * zeta_ref[hh]
            dv_ref[:, sl] = _dot(sc.T, dov) + _dot_nt(kz, dn)
            dn_scr[hh] = dn * gc_ref[hh] + _dot(doT_ref[sl, :], qv * xi_ref[hh])

    cidx = (lambda c: c) if rev else (lambda c: nc - 1 - c)
    hk = pl.BlockSpec((SCAN_HPS, C, RET_DK), lambda h, c: (h, cidx(c), 0))
    vblk = pl.BlockSpec((C, SCAN_HPS * RET_DV), lambda h, c: (cidx(c), h))
    tab = pl.BlockSpec((SCAN_HPS, C, RET_DK), lambda h, c: (h, 0, 0))
    hs = jax.ShapeDtypeStruct(qh.shape, F32)
    return pl.pallas_call(
        body, out_shape=[hs, hs, jax.ShapeDtypeStruct((S, RETW), F32)], grid=(RET_HEADS // SCAN_HPS, nc),
        in_specs=[hk, hk, vblk, vblk, pl.BlockSpec((SCAN_HPS * RET_DV, C), lambda h, c: (h, cidx(c))),
                  pl.BlockSpec((SCAN_HPS, 1, RET_DV, RET_DK), lambda h, c: (h, cidx(c), 0, 0)),
                  pl.BlockSpec((SCAN_HPS, C, C), lambda h, c: (h, 0, 0)), tab, tab,
                  pl.BlockSpec((SCAN_HPS, 1, RET_DK), lambda h, c: (h, 0, 0))],
        out_specs=[hk, hk, vblk],
        scratch_shapes=[pltpu.VMEM((SCAN_HPS, RET_DV, RET_DK), F32)],
        name=name, compiler_params=_params("parallel", "arbitrary"),
    )(qh, kh, v, dob, dob.T, states, jnp.asarray(dmat), jnp.asarray(xi), jnp.asarray(zeta), jnp.asarray(gc))


def _make_ret_scan(rev):
    tag = "ret_rev" if rev else "ret_fwd"

    @jax.custom_vjp
    def op(qh, kh, v):
        return fwd(qh, kh, v)[0]

    def fwd(qh, kh, v):
        o, states = _ret_scan_fwd(qh, kh, v, rev, tag)
        return o, (qh, kh, v, states)

    def bwd(saved, do):
        qh, kh, v, states = saved
        return tuple(_ret_scan_bwd(qh, kh, v, states, do, rev, tag + "_bwd"))

    op.defvjp(fwd, bwd)
    return op


def _rope_tables(S, scale):
    half = RET_DK // 2
    inv = ROPE_BASE ** (-np.arange(half, dtype=np.float32) / half)
    ang = np.arange(S, dtype=np.float32)[:, None] * inv[None, :]
    cos, sin = np.cos(ang), np.sin(ang)
    cos_t = np.tile(np.concatenate([cos, cos], axis=1), (1, RET_HEADS)) * scale
    sin_t = np.tile(np.concatenate([-sin, sin], axis=1), (1, RET_HEADS)) * scale
    return cos_t.astype(np.float32), sin_t.astype(np.float32)


def _rope_apply(t, cos_t, sin_t, name):
    S, W = t.shape
    ts = _pick(S, (512, 256, 128))
    half = RET_DK // 2

    def body(t_ref, c_ref, s_ref, o_ref):
        tv = t_ref[...]
        lane = lax.broadcasted_iota(jnp.int32, tv.shape, 1)
        partner = jnp.where(lane % RET_DK < half, pltpu.roll(tv, W - half, 1), pltpu.roll(tv, half, 1))
        o_ref[...] = tv * c_ref[...] + partner * s_ref[...]

    row = pl.BlockSpec((ts, W), lambda i: (i, 0))
    return pl.pallas_call(body, out_shape=jax.ShapeDtypeStruct((S, W), F32), grid=(S // ts,),
                          in_specs=[row, row, row], out_specs=row, name=name,
                          compiler_params=_params("parallel"))(t, jnp.asarray(cos_t), jnp.asarray(sin_t))


def _make_rope(scale, tag):
    def apply(t):
        cos_t, sin_t = _rope_tables(t.shape[0], scale)
        return _rope_apply(t, cos_t, sin_t, tag)

    op = jax.custom_vjp(apply)

    def fwd(t):
        return apply(t), None

    def bwd(_, dout):
        cos_t, sin_t = _rope_tables(dout.shape[0], scale)
        return (_rope_apply(dout, cos_t, -sin_t, tag + "_bwd"),)

    op.defvjp(fwd, bwd)
    return op


def _att_geometry(L):
    tq = _pick(L, (512, 256, 128))
    return tq, L // tq, tq // ATT_HALO


def _att_specs(tq, per):
    main = pl.BlockSpec((1, tq, DIL_HD), lambda b, n: (b, n, 0))
    prev = pl.BlockSpec((1, ATT_HALO, DIL_HD), lambda b, n: (b, jnp.maximum(n * per - 1, 0), 0))
    return main, prev


def _att_valid(n, u, tq, L):
    ii = lax.broadcasted_iota(jnp.int32, (ATT_SB, ATT_WIN), 0)
    jj = lax.broadcasted_iota(jnp.int32, (ATT_SB, ATT_WIN), 1)
    key = n * tq + u * ATT_SB - ATT_HALO + jj
    return (jnp.abs(jj - ATT_HALO - ii) <= ATT_HALO) & (key >= 0) & (key < L)


def _att_fill(buf, prev_ref, main_ref, next_ref, tq):
    buf[pl.ds(0, ATT_HALO), :] = prev_ref[0]
    buf[pl.ds(ATT_HALO, tq), :] = main_ref[0]
    buf[pl.ds(ATT_HALO + tq, ATT_HALO), :] = next_ref[0]


def _att_fwd(q, k, v, bias, dil, name):
    B, L, _ = q.shape
    tq, nt, per = _att_geometry(L)
    last = L // ATT_HALO - 1

    def body(q_ref, kp_ref, k_ref, kn_ref, vp_ref, v_ref, vn_ref, bias_ref, o_ref, lse_ref, kbuf, vbuf):
        n = pl.program_id(1)
        _att_fill(kbuf, kp_ref, k_ref, kn_ref, tq)
        _att_fill(vbuf, vp_ref, v_ref, vn_ref, tq)
        for u in range(tq // ATT_SB):
            rows = pl.ds(u * ATT_SB, ATT_SB)
            win = pl.ds(u * ATT_SB, ATT_WIN)
            s = _dot_nt(q_ref[0, rows, :], kbuf[win, :]) + bias_ref[0]
            s = jnp.where(_att_valid(n, u, tq, L), s, -1e30)
            m = jnp.max(s, axis=-1, keepdims=True)
            p = jnp.exp(s - m)
            den = jnp.sum(p, axis=-1, keepdims=True)
            o_ref[0, rows, :] = _dot(p, vbuf[win, :]) / den
            lse_ref[0, rows, :] = jnp.broadcast_to(m + jnp.log(den), (ATT_SB, DIL_HD))

    main, prev = _att_specs(tq, per)
    nxt = pl.BlockSpec((1, ATT_HALO, DIL_HD), lambda b, n: (b, jnp.minimum((n + 1) * per, last), 0))
    sd = jax.ShapeDtypeStruct((B, L, DIL_HD), F32)
    return pl.pallas_call(
        body, out_shape=[sd, sd], grid=(B, nt),
        in_specs=[main, prev, main, nxt, prev, main, nxt,
                  pl.BlockSpec((1, ATT_SB, ATT_WIN), lambda b, n: (b // dil, 0, 0))],
        out_specs=[main, main],
        scratch_shapes=[pltpu.VMEM((tq + 2 * ATT_HALO, DIL_HD), q.dtype), pltpu.VMEM((tq + 2 * ATT_HALO, DIL_HD), q.dtype)],
        name=name, compiler_params=_params("parallel", "arbitrary"),
    )(q, k, k, k, v, v, v, bias)


def _att_bwd(q, k, v, bias, o, lse, do, dlse, dil, name):
    B, L, _ = q.shape
    tq, nt, per = _att_geometry(L)
    last = L // ATT_HALO - 1

    def body(q_ref, kp_ref, k_ref, kn_ref, vp_ref, v_ref, vn_ref, bias_ref, o_ref, lse_ref, do_ref, dlse_ref,
             dq_ref, dk_ref, dkp_ref, dkn_ref, dv_ref, dvp_ref, dvn_ref, dbias_ref, kbuf, vbuf, dkbuf, dvbuf):
        b, n = pl.program_id(0), pl.program_id(1)

        @pl.when((b % dil == 0) & (n == 0))
        def _():
            dbias_ref[...] = jnp.zeros_like(dbias_ref)

        _att_fill(kbuf, kp_ref, k_ref, kn_ref, tq)
        _att_fill(vbuf, vp_ref, v_ref, vn_ref, tq)
        dkbuf[...] = jnp.zeros_like(dkbuf)
        dvbuf[...] = jnp.zeros_like(dvbuf)
        for u in range(tq // ATT_SB):
            rows = pl.ds(u * ATT_SB, ATT_SB)
            win = pl.ds(u * ATT_SB, ATT_WIN)
            qu, kw, vw = q_ref[0, rows, :], kbuf[win, :], vbuf[win, :]
            dou = do_ref[0, rows, :]
            s = _dot_nt(qu, kw) + bias_ref[0]
            lse_u = jnp.max(lse_ref[0, rows, :], axis=-1, keepdims=True)
            p = jnp.where(_att_valid(n, u, tq, L), jnp.exp(s - lse_u), 0.0)
            corr = jnp.sum(dlse_ref[0, rows, :] - dou * o_ref[0, rows, :], axis=-1, keepdims=True)
            ds = p * (_dot_nt(dou, vw) + corr)
            dq_ref[0, rows, :] = _dot(ds, kw)
            dkbuf[win, :] += _dot(ds.T, qu)
            dvbuf[win, :] += _dot(p.T, dou)
            dbias_ref[0] += ds
        for full, lo, hi in ((dkbuf, dkp_ref, dkn_ref), (dvbuf, dvp_ref, dvn_ref)):
            lo[0, 0] = full[pl.ds(0, ATT_HALO), :]
            hi[0, 0] = full[pl.ds(ATT_HALO + tq, ATT_HALO), :]
        dk_ref[0] = dkbuf[pl.ds(ATT_HALO, tq), :]
        dv_ref[0] = dvbuf[pl.ds(ATT_HALO, tq), :]

    main, prev = _att_specs(tq, per)
    nxt = pl.BlockSpec((1, ATT_HALO, DIL_HD), lambda b, n: (b, jnp.minimum((n + 1) * per, last), 0))
    halo = pl.BlockSpec((1, 1, ATT_HALO, DIL_HD), lambda b, n: (b, n, 0, 0))
    bias_spec = pl.BlockSpec((1, ATT_SB, ATT_WIN), lambda b, n: (b // dil, 0, 0))
    sd = jax.ShapeDtypeStruct((B, L, DIL_HD), F32)
    hd = jax.ShapeDtypeStruct((B, nt, ATT_HALO, DIL_HD), F32)
    width = tq + 2 * ATT_HALO
    dq, dk, dkp, dkn, dv, dvp, dvn, dbias = pl.pallas_call(
        body, out_shape=[sd, sd, hd, hd, sd, hd, hd, jax.ShapeDtypeStruct(bias.shape, F32)], grid=(B, nt),
        in_specs=[main, prev, main, nxt, prev, main, nxt, bias_spec, main, main, main, main],
        out_specs=[main, main, halo, halo, main, halo, halo, bias_spec],
        scratch_shapes=[pltpu.VMEM((width, DIL_HD), q.dtype), pltpu.VMEM((width, DIL_HD), q.dtype),
                        pltpu.VMEM((width, DIL_HD), F32), pltpu.VMEM((width, DIL_HD), F32)],
        name=name, compiler_params=_params("arbitrary", "arbitrary"),
    )(q, k, k, k, v, v, v, bias, o, lse, do, dlse)

    def fold(mainv, lo, hi):
        t = mainv.reshape(B, nt, tq, DIL_HD)
        if nt > 1:
            t = t.at[:, :-1, tq - ATT_HALO:, :].add(lo[:, 1:])
            t = t.at[:, 1:, :ATT_HALO, :].add(hi[:, :-1])
        return t.reshape(B, L, DIL_HD)

    return dq, fold(dk, dkp, dkn), fold(dv, dvp, dvn), dbias


def _make_attention(dil):
    tag = "att_d%d" % dil

    @jax.custom_vjp
    def op(q, k, v, bias):
        return fwd(q, k, v, bias)[0]

    def fwd(q, k, v, bias):
        qb, kb, vb = q.astype(MXU), k.astype(MXU), v.astype(MXU)
        o, lse = _att_fwd(qb, kb, vb, bias, dil, tag)
        return (o, lse), (qb, kb, vb, bias, o, lse)

    def bwd(saved, cts):
        qb, kb, vb, bias, o, lse = saved
        do, dlse = cts
        return tuple(_att_bwd(qb, kb, vb, bias, o, lse, do, dlse, dil, tag + "_bwd"))

    op.defvjp(fwd, bwd)
    return op


def _merge_weights(l0, l1, l2):
    m = jnp.maximum(jnp.maximum(l0, l1), l2)
    e0, e1, e2 = jnp.exp(l0 - m), jnp.exp(l1 - m), jnp.exp(l2 - m)
    inv = 1.0 / (e0 + e1 + e2)
    return e0 * inv, e1 * inv, e2 * inv


def _merge_call(body, n_in, n_out, shape, name):
    R, W = shape
    ts = _pick(R, (1024, 512, 256, 128))
    row = pl.BlockSpec((ts, W), lambda i: (i, 0))
    sd = jax.ShapeDtypeStruct(shape, F32)
    return pl.pallas_call(body, out_shape=[sd] * n_out, grid=(R // ts,), in_specs=[row] * n_in,
                          out_specs=[row] * n_out, name=name, compiler_params=_params("parallel"))


@jax.custom_vjp
def dil_merge(o0, o1, o2, l0, l1, l2):
    return _dil_merge_fwd(o0, o1, o2, l0, l1, l2)[0]


def _dil_merge_fwd(*args):
    def body(o0, o1, o2, l0, l1, l2, out):
        w0, w1, w2 = _merge_weights(l0[...], l1[...], l2[...])
        out[...] = w0 * o0[...] + w1 * o1[...] + w2 * o2[...]

    return _merge_call(body, 6, 1, args[0].shape, "dil_merge")(*args)[0], args


def _dil_merge_bwd(args, dout):
    def body(o0, o1, o2, l0, l1, l2, d, do0, do1, do2, dl0, dl1, dl2):
        ws = _merge_weights(l0[...], l1[...], l2[...])
        dv = d[...]
        dws = [dv * o[...] for o in (o0, o1, o2)]
        mean = ws[0] * dws[0] + ws[1] * dws[1] + ws[2] * dws[2]
        for w, dw, do_ref, dl_ref in zip(ws, dws, (do0, do1, do2), (dl0, dl1, dl2)):
            do_ref[...] = w * dv
            dl_ref[...] = w * (dw - mean)

    return tuple(_merge_call(body, 7, 6, args[0].shape, "dil_merge_bwd")(*args, dout))


dil_merge.defvjp(_dil_merge_fwd, _dil_merge_bwd)


def _t5_bucket(rel):
    nb = REL_BUCKETS // 2
    max_exact = nb // 2
    sign_off = np.where(rel > 0, nb, 0)
    n = np.abs(rel)
    nf = np.maximum(n, 1).astype(np.float32)
    large = max_exact + (np.log(nf / np.float32(max_exact)) / np.float32(math.log(REL_MAX_DIST / max_exact))
                         * np.float32(nb - max_exact)).astype(np.int32)
    large = np.minimum(large, nb - 1)
    return sign_off + np.where(n < max_exact, n, large)


def _loss_grad(xf, target):
    S, D = xf.shape
    ts = _pick(S, (512, 256, 128))

    def body(x_ref, t_ref, dy_ref, part_ref):
        @pl.when(pl.program_id(0) == 0)
        def _():
            part_ref[...] = jnp.zeros_like(part_ref)

        err = x_ref[...] - t_ref[...]
        dy_ref[...] = err * (1.0 / D)
        part_ref[...] += jnp.sum(err * err, axis=0, keepdims=True)

    row = pl.BlockSpec((ts, D), lambda i: (i, 0))
    return pl.pallas_call(body, out_shape=[jax.ShapeDtypeStruct((S, D), F32), jax.ShapeDtypeStruct((1, D), F32)],
                          grid=(S // ts,), in_specs=[row, row], out_specs=[row, pl.BlockSpec((1, D), lambda i: (0, 0))],
                          name="loss_grad", compiler_params=_params("arbitrary"))(xf, target)


def _adamw_math(g, w, m, v):
    m = ADAM_B1 * m + (1.0 - ADAM_B1) * g
    v = ADAM_B2 * v + (1.0 - ADAM_B2) * (g * g)
    m_hat = m / (1.0 - ADAM_B1 ** ADAM_STEP)
    v_hat = v / (1.0 - ADAM_B2 ** ADAM_STEP)
    delta = -ADAM_LR * (m_hat / (jnp.sqrt(v_hat) + ADAM_EPS) + ADAM_WD * w)
    return delta, m, v


def _adamw(g, w, m, v, name):
    Lw, R, C = w.shape
    tr = _pick(R, (256, 128, 64, 32, 16, 8))

    def body(g_ref, w_ref, m_ref, v_ref, d_ref, nm_ref, nv_ref):
        d_ref[0], nm_ref[0], nv_ref[0] = _adamw_math(g_ref[0], w_ref[0], m_ref[0], v_ref[0])

    blk = pl.BlockSpec((1, tr, C), lambda l, i: (l, i, 0))
    sd = jax.ShapeDtypeStruct(w.shape, F32)
    return pl.pallas_call(body, out_shape=[sd] * 3, grid=(Lw, R // tr), in_specs=[blk] * 4, out_specs=[blk] * 3,
                          name=name, compiler_params=_params("parallel", "parallel"))(g, w, m, v)


def _mesh_pos():
    return lax.axis_index("x"), lax.axis_index("y"), lax.axis_index("c")


def _slot(x, y, c):
    return 4 * x + 2 * y + c


class _Comm:
    def __init__(self, arrays):
        self.arrays = list(arrays)

    def sem_shapes(self):
        na = len(self.arrays)
        return [pltpu.SemaphoreType.DMA((7 * na,)), pltpu.SemaphoreType.DMA((7 * na,)), pltpu.SemaphoreType.DMA((na,))]

    def mid(self, ins, outs, sems):
        pass

    def call(self, name):
        na = len(self.arrays)

        def body(*refs):
            ins, outs, sems = refs[:na], refs[na:2 * na], refs[2 * na:]
            self.start(ins, outs, sems)
            self.mid(ins, outs, sems)
            self.finish(ins, outs, sems)

        anyspec = pl.BlockSpec(memory_space=pl.ANY)
        return pl.pallas_call(body, out_shape=self.out_shapes(), in_specs=[anyspec] * na, out_specs=[anyspec] * na,
                              scratch_shapes=self.sem_shapes(), name=name)(*self.arrays)


class _Gather(_Comm):
    def out_shapes(self):
        return [jax.ShapeDtypeStruct((N_DEV,) + b.shape, b.dtype) for b in self.arrays]

    def _copies(self, ins, outs, sems):
        send_sems, recv_sems, local_sems = sems
        x, y, c = _mesh_pos()
        me, sibling = (x, y, c), (x, y, 1 - c)
        chips = [(1 - x, y), (x, 1 - y), (1 - x, 1 - y)]
        per = []
        for a in range(len(self.arrays)):
            def copy(k, block, to, src=None, a=a):
                dst = outs[a].at[_slot(*block)]
                return pltpu.make_async_remote_copy(
                    src_ref=dst if src is None else src, dst_ref=dst,
                    send_sem=send_sems.at[7 * a + k], recv_sem=recv_sems.at[7 * a + k],
                    device_id=to, device_id_type=MESH_ID)

            per.append(dict(
                mine=pltpu.make_async_copy(ins[a], outs[a].at[_slot(*me)], local_sems.at[a]),
                first=[copy(0, me, sibling, src=ins[a])] + [copy(1 + j, me, (*ch, c), src=ins[a]) for j, ch in enumerate(chips)],
                passed=[copy(4 + j, (*ch, c), sibling) for j, ch in enumerate(chips)],
                over_ici=[copy(1 + j, (*ch, c), me) for j, ch in enumerate(chips)],
                from_sibling=[copy(0, sibling, me)] + [copy(4 + j, (*ch, 1 - c), me) for j, ch in enumerate(chips)]))
        return per

    def start(self, ins, outs, sems):
        for p in self._copies(ins, outs, sems):
            p["mine"].start()
            for cp in p["first"]:
                cp.start()

    def mid(self, ins, outs, sems):
        for p in self._copies(ins, outs, sems):
            for arrived, onward in zip(p["over_ici"], p["passed"]):
                arrived.wait_recv()
                onward.start()

    def finish(self, ins, outs, sems):
        for p in self._copies(ins, outs, sems):
            for cp in p["from_sibling"]:
                cp.wait_recv()
            for cp in p["first"] + p["passed"]:
                cp.wait_send()
            p["mine"].wait()


class _Exchange(_Comm):
    def out_shapes(self):
        return [jax.ShapeDtypeStruct(f.shape, f.dtype) for f in self.arrays]

    def _copies(self, ins, outs, sems):
        send_sems, recv_sems, local_sems = sems
        x, y, c = _mesh_pos()
        my_slot = _slot(x, y, c)
        local, remote = [], []
        for a in range(len(self.arrays)):
            local.append(pltpu.make_async_copy(ins[a].at[my_slot], outs[a].at[my_slot], local_sems.at[a]))
            for k in range(1, N_DEV):
                px = 1 - x if k & 4 else x
                py = 1 - y if k & 2 else y
                pc = 1 - c if k & 1 else c
                remote.append(pltpu.make_async_remote_copy(
                    src_ref=ins[a].at[_slot(px, py, pc)], dst_ref=outs[a].at[my_slot],
                    send_sem=send_sems.at[7 * a + k - 1], recv_sem=recv_sems.at[7 * a + k - 1],
                    device_id=(px, py, pc), device_id_type=MESH_ID))
        return local, remote

    def start(self, ins, outs, sems):
        local, remote = self._copies(ins, outs, sems)
        for cp in local + remote:
            cp.start()

    def finish(self, ins, outs, sems):
        local, remote = self._copies(ins, outs, sems)
        for cp in remote + local:
            cp.wait()


def _sum_slots(parts, name):
    _, R, C = parts.shape
    tr = _pick(R, (256, 128, 64, 32, 16, 8))

    def body(p_ref, o_ref):
        g = p_ref[0].astype(F32)
        for s in range(1, N_DEV):
            g = g + p_ref[s].astype(F32)
        o_ref[...] = g

    return pl.pallas_call(body, out_shape=jax.ShapeDtypeStruct((R, C), F32), grid=(R // tr,),
                          in_specs=[pl.BlockSpec((N_DEV, tr, C), lambda i: (0, i, 0))],
                          out_specs=pl.BlockSpec((tr, C), lambda i: (i, 0)), name=name,
                          compiler_params=_params("parallel"))(parts)


_hg_fwd_op, _hg_rev_op = _make_hg_scan(False), _make_hg_scan(True)
_ret_fwd_op, _ret_rev_op = _make_ret_scan(False), _make_ret_scan(True)
_rope_q, _rope_k = _make_rope(1.0, "rope_q"), _make_rope(RET_DK ** -0.5, "rope_k")
_hg_post = _make_gnorm(2, True, HG_D, False, 1.0, "hg_post")
_ret_post = _make_gnorm(2, True, RET_DV, True, 1.0, "ret_post")
_qk_norms = _make_head_norms((DIL_HD ** -0.5, 1.0) * len(DIL_GROUPS), "dil_qk_norms")
_att_ops = {dil: _make_attention(dil) for _, dil in DIL_GROUPS}


def _to_heads(t, d):
    S, W = t.shape
    return t.reshape(S, W // d, d).transpose(1, 0, 2)


def _dilated_mixer(parts, rel_bias, q_gain, k_gain):
    S = parts[0].shape[0]
    qg, kg = jnp.tile(q_gain, DIL_SLOTS), jnp.tile(k_gain, DIL_SLOTS)
    ii = np.arange(ATT_SB)[:, None]
    jj = np.arange(ATT_WIN)[None, :]
    n_groups = len(DIL_GROUPS)
    normed = _qk_norms(*[parts[3 * g + t] for g in range(n_groups) for t in (0, 1)], *([qg, kg] * n_groups))
    outs, lses = [], []
    for g, (window, dil) in enumerate(DIL_GROUPS):
        assert window // (2 * dil) == ATT_HALO
        L = S // dil

        def to_res(t):
            return t.reshape(L, dil, DIL_SLOTS, DIL_HD).transpose(2, 1, 0, 3).reshape(DIL_SLOTS * dil, L, DIL_HD)

        def from_res(t):
            return t.reshape(DIL_SLOTS, dil, L, DIL_HD).transpose(0, 2, 1, 3).reshape(DIL_SLOTS * S, DIL_HD)

        onehot = (_t5_bucket((jj - ATT_HALO - ii) * dil)[:, :, None] == np.arange(REL_BUCKETS)).astype(np.float32)
        bias = jnp.einsum("ijb,bh->hij", onehot, rel_bias[:, g * DIL_SLOTS:(g + 1) * DIL_SLOTS],
                          precision=lax.Precision.HIGHEST)
        o, lse = _att_ops[dil](to_res(normed[2 * g]), to_res(normed[2 * g + 1]), to_res(parts[3 * g + 2]), bias)
        outs.append(from_res(o))
        lses.append(from_res(lse))
    merged = dil_merge(*outs, *lses)
    return merged.reshape(DIL_SLOTS, S, DIL_HD).transpose(1, 0, 2).reshape(S, DILW)


def _mixers(h, p):
    offs = np.cumsum(IN_SPLITS)[:-1].tolist()
    parts = jnp.split(h, offs, axis=-1)
    q, v = parts[0], parts[1]
    y_a = _hg_post(_hg_fwd_op(q, v, parts[2], p["lb_fwd"]), _hg_rev_op(q, v, parts[3], p["lb_bwd"]), p["hg_norm"], parts[4])
    qh = _to_heads(_rope_q(parts[5]), RET_DK)
    kh = _to_heads(_rope_k(parts[6]), RET_DK)
    y_b = _ret_post(_ret_fwd_op(qh, kh, parts[7]), _ret_rev_op(qh, kh, parts[7]), p["ret_norm"], parts[8])
    y_c = _dilated_mixer(parts[9:], p["rel_bias"], p["q_norm"], p["k_norm"])
    return jnp.concatenate([y_a, y_b, y_c], axis=-1)


def _col_full(blocks):
    return blocks.transpose(1, 0, 2).reshape(blocks.shape[1], -1)


def _col_blocks(full):
    K = full.shape[0]
    return full.reshape(K, N_DEV, -1).transpose(1, 0, 2)


def _row_full(blocks):
    return blocks.reshape(-1, blocks.shape[2])


def _row_blocks(full):
    return full.reshape(N_DEV, -1, full.shape[1])


SMALL_NAMES = ("norm_mix", "norm_mlp", "hg_lb_fwd", "hg_lb_bwd", "hg_norm", "ret_norm", "q_norm", "k_norm", "rel_bias")


def _forward(x, w_in, shards, small):
    depth = len(shards)
    lb_f = jnp.cumsum(jax.nn.softmax(small["hg_lb_fwd"], axis=0), axis=0)
    lb_b = jnp.cumsum(jax.nn.softmax(small["hg_lb_bwd"], axis=0), axis=0)
    for l in range(depth):
        p = {k: small[k][l] for k in ("norm_mix", "norm_mlp", "hg_norm", "ret_norm", "q_norm", "k_norm")}
        p["lb_fwd"], p["lb_bwd"] = lb_f[l] - lb_f[0], lb_b[l] - lb_b[0]
        p["rel_bias"] = small["rel_bias"]
        sh = shards[l]
        h, (g_up, g_out) = norm_matmul(x, p["norm_mix"], w_in, (sh["up"], sh["out"]))
        x, _ = out_proj(_mixers(h, p), _row_full(g_out), x, ())
        halves = ()
        if l < depth - 1:
            nxt = shards[l + 1]["in"]
            halves = (nxt[:nxt.shape[0] // 2], nxt[nxt.shape[0] // 2:])
        x, g_in = mlp(x, p["norm_mlp"], _col_full(g_up), sh["down"], halves)
        if halves:
            w_in = jnp.concatenate([_col_full(t) for t in g_in], axis=0)
    return x


def kernel(x, w_in, w_out, w_up, w_down, norm_mix, norm_mlp, hg_lb_fwd, hg_lb_bwd, hg_norm, ret_norm, q_norm, k_norm, rel_bias, loss_target, m_w_in, m_w_out, m_w_up, m_w_down, m_norm_mix, m_norm_mlp, m_hg_lb_fwd, m_hg_lb_bwd, m_hg_norm, m_ret_norm, m_q_norm, m_k_norm, m_rel_bias, v_w_in, v_w_out, v_w_up, v_w_down, v_norm_mix, v_norm_mlp, v_hg_lb_fwd, v_hg_lb_bwd, v_hg_norm, v_ret_norm, v_q_norm, v_k_norm, v_rel_bias):
    depth = w_in.shape[0]
    big = (w_in, w_out, w_up, w_down)
    big_m = (m_w_in, m_w_out, m_w_up, m_w_down)
    big_v = (v_w_in, v_w_out, v_w_up, v_w_down)
    small = dict(zip(SMALL_NAMES, (norm_mix, norm_mlp, hg_lb_fwd, hg_lb_bwd, hg_norm, ret_norm, q_norm, k_norm, rel_bias)))
    small_m = (m_norm_mix, m_norm_mlp, m_hg_lb_fwd, m_hg_lb_bwd, m_hg_norm, m_ret_norm, m_q_norm, m_k_norm, m_rel_bias)
    small_v = (v_norm_mix, v_norm_mlp, v_hg_lb_fwd, v_hg_lb_bwd, v_hg_norm, v_ret_norm, v_q_norm, v_k_norm, v_rel_bias)

    (gathered0,) = _Gather([w_in[0].astype(WIRE)]).call("gather_w_in0")
    names = ("in", "out", "up", "down")
    shards = [{n: w[l] for n, w in zip(names, big) if l or n != "in"} for l in range(depth)]

    xf, vjp = jax.vjp(_forward, x[0], _col_full(gathered0), shards, small)
    dy, part = _loss_grad(xf, loss_target[0])
    loss = lax.psum(0.5 / xf.shape[1] * jnp.sum(part), ("x", "y", "c"))
    dx, dw_in0, dshards, dsmall = vjp(dy)

    (landed0,) = _Exchange([_col_blocks(dw_in0)]).call("exchange_w_in0")
    dshards[0]["in"] = _sum_slots(landed0, "sum_w_in0")
    big_out = []
    for i, n in enumerate(names):
        g = jnp.stack([dshards[l][n] for l in range(depth)])
        big_out.append((g,) + tuple(_adamw(g, big[i], big_m[i], big_v[i], "adamw_" + n)))

    flat = jnp.concatenate([dsmall[n].reshape(-1) for n in SMALL_NAMES])
    n_small = flat.shape[0]
    rows = -(-n_small // 1024) * 8
    pad = lambda t: jnp.pad(t, (0, rows * 128 - n_small)).reshape(1, rows, 128)
    (small_parts,) = _Gather([pad(flat)[0]]).call("gather_small_grads")
    cat = lambda ts: pad(jnp.concatenate([t.reshape(-1) for t in ts]))
    g_small = _sum_slots(small_parts, "sum_small_grads")[None]
    small_out = (g_small,) + tuple(_adamw(g_small, cat([small[n] for n in SMALL_NAMES]), cat(small_m), cat(small_v), "adamw_small"))

    def unpack(t):
        t = t.reshape(-1)
        out, off = [], 0
        for n in SMALL_NAMES:
            size = small[n].size
            out.append(t[off:off + size].reshape(small[n].shape))
            off += size
        return out

    res = [loss, dx[None]]
    for kind in range(4):
        res += [o[kind] for o in big_out] + unpack(small_out[kind])
    return tuple(res)
```

```python
import functools
import math

import numpy as np
import jax
import jax.numpy as jnp
from jax import lax
from jax.experimental import pallas as pl
from jax.experimental.pallas import tpu as pltpu

F32 = jnp.float32
MXU = jnp.bfloat16
WIRE = jnp.bfloat16
EPS = 1e-6
N_DEV = 8
VMEM_LIMIT = 48 * 1024 * 1024

HG_HEADS, HG_D = 6, 128
RET_HEADS, RET_DK, RET_DV = 6, 64, 128
DIL_SLOTS, DIL_HD = 4, 128
DIL_GROUPS = ((128, 1), (512, 4), (2048, 16))
HGW = HG_HEADS * HG_D
RETW = RET_HEADS * RET_DV
DILW = DIL_SLOTS * DIL_HD
IN_SPLITS = (HGW, HGW, HGW, HGW, HGW, RET_HEADS * RET_DK, RET_HEADS * RET_DK, RETW, RETW) + (DILW,) * 9
REL_BUCKETS, REL_MAX_DIST = 32, 1024
ROPE_BASE = 10000.0
ADAM_LR, ADAM_B1, ADAM_B2, ADAM_EPS, ADAM_WD, ADAM_STEP = 0.001, 0.9, 0.999, 1e-08, 0.01, 10

SCAN_C = 128
SCAN_HPS = 6
HG_SUB = 16
EXP_CLAMP = 60.0
ATT_SB, ATT_HALO = 128, 64
ATT_WIN = ATT_SB + 2 * ATT_HALO
MESH_ID = pl.DeviceIdType.MESH


def _pick(n, prefs):
    for p in prefs:
        if n % p == 0:
            return p
    return n


def _params(*sem):
    return pltpu.CompilerParams(dimension_semantics=sem, vmem_limit_bytes=VMEM_LIMIT)


def _dot(a, b):
    return lax.dot_general(a.astype(MXU), b.astype(MXU), (((1,), (0,)), ((), ())), preferred_element_type=F32)


def _dot_nt(a, b):
    return lax.dot_general(a.astype(MXU), b.astype(MXU), (((1,), (1,)), ((), ())), preferred_element_type=F32)


def _dot_tn(a, b):
    return lax.dot_general(a.astype(MXU), b.astype(MXU), (((0,), (0,)), ((), ())), preferred_element_type=F32)


def _dot01(sel, x):
    if MXU == F32:
        return _dot(sel, x)
    hi = x.astype(MXU)
    r1 = x - hi.astype(F32)
    mid = r1.astype(MXU)
    lo = (r1 - mid.astype(F32)).astype(MXU)
    return _dot(sel, hi) + _dot(sel, mid) + _dot(sel, lo)


def _mm(a, b, *, nt=False, ta=False, out_dtype=F32, res=None, u_in=None, emit_act=False, rider=None, name):
    M, K = a.shape[::-1] if ta else a.shape
    N = b.shape[0] if nt else b.shape[1]
    tm = _pick(M, (1024, 512, 256, 128))
    tn = _pick(N, (1024, 768, 512, 384, 256, 128))
    tk = _pick(K, (2048, 1536, 1024, 512, 256, 128))
    ni, nj, nk = M // tm, N // tn, K // tk
    n_ride = len(rider.arrays) if rider is not None else 0

    def body(*refs):
        it = iter(refs)
        a_ref, b_ref = next(it), next(it)
        res_ref = next(it) if res is not None else None
        u_ref = next(it) if u_in is not None else None
        ride_in = [next(it) for _ in range(n_ride)]
        o_ref = next(it)
        act_ref = next(it) if emit_act else None
        ride_out = [next(it) for _ in range(n_ride)]
        acc_ref = next(it)
        sems = list(it)
        i, j, k = pl.program_id(0), pl.program_id(1), pl.program_id(2)

        step = (i * nj + j) * nk + k
        if rider is not None:
            pl.when(step == 0)(functools.partial(rider.start, ride_in, ride_out, sems))
            pl.when(step == (3 * ni // 4) * nj * nk)(functools.partial(rider.mid, ride_in, ride_out, sems))

        @pl.when(k == 0)
        def _():
            acc_ref[...] = jnp.zeros_like(acc_ref)

        acc_ref[...] += (_dot_tn if ta else _dot_nt if nt else _dot)(a_ref[...], b_ref[...])

        @pl.when(k == nk - 1)
        def _():
            r = acc_ref[...]
            if res_ref is not None:
                r = r + res_ref[...]
            if u_ref is not None:
                r = r * (2.0 * jnp.maximum(u_ref[...], 0.0))
            o_ref[...] = r.astype(o_ref.dtype)
            if act_ref is not None:
                t = jnp.maximum(r, 0.0)
                act_ref[...] = (t * t).astype(act_ref.dtype)

        if rider is not None:
            pl.when(step == ni * nj * nk - 1)(functools.partial(rider.finish, ride_in, ride_out, sems))

    mn = pl.BlockSpec((tm, tn), lambda i, j, k: (i, j))
    anyspec = pl.BlockSpec(memory_space=pl.ANY)
    in_specs = [pl.BlockSpec((tk, tm), lambda i, j, k: (k, i)) if ta else pl.BlockSpec((tm, tk), lambda i, j, k: (i, k)),
                pl.BlockSpec((tn, tk), lambda i, j, k: (j, k)) if nt else pl.BlockSpec((tk, tn), lambda i, j, k: (k, j))]
    args = [a, b]
    for extra in (res, u_in):
        if extra is not None:
            in_specs.append(mn)
            args.append(extra)
    out_shape = [jax.ShapeDtypeStruct((M, N), out_dtype)]
    out_specs = [mn]
    if emit_act:
        out_shape.append(jax.ShapeDtypeStruct((M, N), MXU))
        out_specs.append(mn)
    scratch = [pltpu.VMEM((tm, tn), F32)]
    if rider is not None:
        in_specs += [anyspec] * n_ride
        args += list(rider.arrays)
        out_shape += rider.out_shapes()
        out_specs += [anyspec] * n_ride
        scratch += rider.sem_shapes()
    sem = ("arbitrary",) * 3 if rider is not None else ("parallel", "parallel", "arbitrary")
    out = pl.pallas_call(
        body, out_shape=out_shape, grid=(ni, nj, nk), in_specs=in_specs, out_specs=out_specs,
        scratch_shapes=scratch, name=name, compiler_params=_params(*sem))(*args)
    return out if (emit_act or rider is not None) else out[0]


def _gnorm_stats(x, center):
    if center:
        x = x - jnp.mean(x, axis=-1, keepdims=True)
    r = lax.rsqrt(jnp.mean(x * x, axis=-1, keepdims=True) + EPS)
    return x * r, r


def _silu_parts(gt):
    sg = jax.nn.sigmoid(gt)
    return gt * sg, sg * (1.0 + gt * (1.0 - sg))


def _gnorm_fwd(xs, gain, gate, *, group, center, scale, out_dtype, name):
    S, W = xs[0].shape
    ts = _pick(S, (512, 256, 128))
    nx = len(xs)

    def body(*refs):
        x_refs, g_ref = refs[:nx], refs[nx]
        gate_ref = refs[nx + 1] if gate is not None else None
        o_ref = refs[-1]
        for gi in range(W // group):
            sl = slice(gi * group, (gi + 1) * group)
            x = x_refs[0][:, sl]
            for xr in x_refs[1:]:
                x = x + xr[:, sl]
            n, _ = _gnorm_stats(x, center)
            y = n * (g_ref[:, sl] * scale)
            if gate_ref is not None:
                y = y * _silu_parts(gate_ref[:, sl])[0]
            o_ref[:, sl] = y.astype(o_ref.dtype)

    row = pl.BlockSpec((ts, W), lambda i: (i, 0))
    vec = pl.BlockSpec((1, W), lambda i: (0, 0))
    args = list(xs) + [gain] + ([gate] if gate is not None else [])
    in_specs = [row] * nx + [vec] + ([row] if gate is not None else [])
    return pl.pallas_call(body, out_shape=jax.ShapeDtypeStruct((S, W), out_dtype), grid=(S // ts,),
                          in_specs=in_specs, out_specs=row, name=name, compiler_params=_params("parallel"))(*args)


def _gnorm_bwd(dy, xs, gain, gate, *, group, center, scale, name):
    S, W = xs[0].shape
    ts = _pick(S, (512, 256, 128))
    nx = len(xs)

    def body(*refs):
        dy_ref = refs[0]
        x_refs, g_ref = refs[1:1 + nx], refs[1 + nx]
        gate_ref = refs[2 + nx] if gate is not None else None
        outs = refs[(3 + nx if gate is not None else 2 + nx):]
        dx_ref, dg_ref = outs[0], outs[1]
        dgate_ref = outs[2] if gate is not None else None

        @pl.when(pl.program_id(0) == 0)
        def _():
            dg_ref[...] = jnp.zeros_like(dg_ref)

        for gi in range(W // group):
            sl = slice(gi * group, (gi + 1) * group)
            x = x_refs[0][:, sl]
            for xr in x_refs[1:]:
                x = x + xr[:, sl]
            n, r = _gnorm_stats(x, center)
            dyv = dy_ref[:, sl].astype(F32)
            g = g_ref[:, sl] * scale
            if gate_ref is not None:
                act, dact = _silu_parts(gate_ref[:, sl])
                dgate_ref[:, sl] = dyv * n * g * dact
                dyv = dyv * act
            dg_ref[:, sl] += jnp.sum(dyv * n, axis=0, keepdims=True) * scale
            dn = dyv * g
            t = dn - n * jnp.mean(dn * n, axis=-1, keepdims=True)
            if center:
                t = t - jnp.mean(dn, axis=-1, keepdims=True)
            dx_ref[:, sl] = r * t

    row = pl.BlockSpec((ts, W), lambda i: (i, 0))
    vec = pl.BlockSpec((1, W), lambda i: (0, 0))
    args = [dy] + list(xs) + [gain] + ([gate] if gate is not None else [])
    in_specs = [row] * (1 + nx) + [vec] + ([row] if gate is not None else [])
    out_shape = [jax.ShapeDtypeStruct((S, W), F32), jax.ShapeDtypeStruct((1, W), F32)]
    out_specs = [row, vec]
    if gate is not None:
        out_shape.append(jax.ShapeDtypeStruct((S, W), F32))
        out_specs.append(row)
    out = pl.pallas_call(body, out_shape=out_shape, grid=(S // ts,), in_specs=in_specs, out_specs=out_specs,
                         name=name, compiler_params=_params("arbitrary"))(*args)
    return out[0], out[1], (out[2] if gate is not None else None)


def _head_norms(xs, gains, scales, dys, name):
    n = len(xs)
    S, W = xs[0].shape
    ts = _pick(S, (256, 128))
    bwd = dys is not None

    def body(*refs):
        x_refs, g_refs = refs[:n], refs[n:2 * n]
        dy_refs = refs[2 * n:3 * n] if bwd else ()
        outs = refs[(3 * n if bwd else 2 * n):]
        if bwd:
            @pl.when(pl.program_id(0) == 0)
            def _():
                for dg_ref in outs[n:]:
                    dg_ref[...] = jnp.zeros_like(dg_ref)

        for a in range(n):
            for gi in range(W // DIL_HD):
                sl = slice(gi * DIL_HD, (gi + 1) * DIL_HD)
                nrm, r = _gnorm_stats(x_refs[a][:, sl], False)
                g = g_refs[a][:, sl] * scales[a]
                if not bwd:
                    outs[a][:, sl] = nrm * g
                    continue
                dyv = dy_refs[a][:, sl]
                outs[n + a][:, sl] += jnp.sum(dyv * nrm, axis=0, keepdims=True) * scales[a]
                dn = dyv * g
                outs[a][:, sl] = r * (dn - nrm * jnp.mean(dn * nrm, axis=-1, keepdims=True))

    row = pl.BlockSpec((ts, W), lambda i: (i, 0))
    vec = pl.BlockSpec((1, W), lambda i: (0, 0))
    sd = jax.ShapeDtypeStruct((S, W), F32)
    out_shape = [sd] * n + ([jax.ShapeDtypeStruct((1, W), F32)] * n if bwd else [])
    return pl.pallas_call(
        body, out_shape=out_shape, grid=(S // ts,), in_specs=[row] * n + [vec] * n + ([row] * n if bwd else []),
        out_specs=[row] * n + ([vec] * n if bwd else []), name=name,
        compiler_params=_params("arbitrary" if bwd else "parallel"))(*xs, *gains, *(dys or ()))


def _make_head_norms(scales, tag):
    n = len(scales)

    def apply(*args):
        return tuple(_head_norms(args[:n], [g[None, :] for g in args[n:]], scales, None, tag))

    op = jax.custom_vjp(apply)

    def fwd(*args):
        return apply(*args), args

    def bwd(args, dys):
        out = _head_norms(args[:n], [g[None, :] for g in args[n:]], scales, list(dys), tag + "_bwd")
        return tuple(out[:n]) + tuple(d[0] for d in out[n:])

    op.defvjp(fwd, bwd)
    return op


def _make_gnorm(nx, has_gate, group, center, scale, tag):
    kw = dict(group=group, center=center, scale=scale)

    @jax.custom_vjp
    def op(*args):
        return fwd(*args)[0]

    def fwd(*args):
        xs, gain = args[:nx], args[nx]
        gate = args[nx + 1] if has_gate else None
        y = _gnorm_fwd(xs, gain[None, :], gate, out_dtype=F32, name=tag + "_fwd", **kw)
        return y, args

    def bwd(args, dy):
        xs, gain = args[:nx], args[nx]
        gate = args[nx + 1] if has_gate else None
        dx, dg, dgate = _gnorm_bwd(dy, xs, gain[None, :], gate, name=tag + "_bwd", **kw)
        return (dx,) * nx + (dg[0],) + ((dgate,) if has_gate else ())

    op.defvjp(fwd, bwd)
    return op


def _ride(cls, arrays):
    return cls(list(arrays)) if len(arrays) else None


def _mm_ride(*args, rider, name, **kw):
    if rider is None:
        return _mm(*args, name=name, **kw), ()
    out = _mm(*args, rider=rider, name=name + "_ride", **kw)
    n = len(rider.arrays)
    main = out[:-n]
    return (main[0] if len(main) == 1 else tuple(main)), tuple(out[-n:])


def _landed_sums(landed, tag):
    return tuple(_sum_slots(l, "%s_sum%d" % (tag, i)) for i, l in enumerate(landed))


def _rms(x, g, name):
    return _gnorm_fwd([x], g[None, :], None, group=x.shape[1], center=False, scale=1.0, out_dtype=MXU, name=name)


def _rms_bwd(dxn, x, g, name):
    dx, dg, _ = _gnorm_bwd(dxn, [x], g[None, :], None, group=x.shape[1], center=False, scale=1.0, name=name)
    return dx, dg[0]


def _wire(shards):
    return [t.astype(WIRE) for t in shards]


@jax.custom_vjp
def norm_matmul(x, g, w, shards):
    return _norm_matmul_fwd(x, g, w, shards)[0]


def _norm_matmul_fwd(x, g, w, shards):
    xn = _rms(x, g, "rms_in")
    h, gathered = _mm_ride(xn, w, rider=_ride(_Gather, _wire(shards)), name="mm_in")
    return (h, gathered), (x, g, w, xn)


def _norm_matmul_bwd(saved, cts):
    x, g, w, xn = saved
    dh, d_gathered = cts
    dxn, l0 = _mm_ride(dh, w, nt=True, rider=_ride(_Exchange, d_gathered[:1]), name="mm_in_dx")
    dw, l1 = _mm_ride(xn, dh, ta=True, out_dtype=w.dtype, rider=_ride(_Exchange, d_gathered[1:]), name="mm_in_dw")
    dx, dg = _rms_bwd(dxn, x, g, "rms_in_bwd")
    return dx, dg, dw, _landed_sums(l0 + l1, "in")


norm_matmul.defvjp(_norm_matmul_fwd, _norm_matmul_bwd)


@jax.custom_vjp
def out_proj(y, w, x, shards):
    return _out_proj_fwd(y, w, x, shards)[0]


def _out_proj_fwd(y, w, x, shards):
    yb = y.astype(MXU)
    out, gathered = _mm_ride(yb, w, res=x, rider=_ride(_Gather, _wire(shards)), name="mm_out")
    return (out, gathered), (yb, w)


def _out_proj_bwd(saved, cts):
    yb, w = saved
    dout, d_gathered = cts
    db = dout.astype(MXU)
    dy, l0 = _mm_ride(db, w, nt=True, rider=_ride(_Exchange, d_gathered[:1]), name="mm_out_dy")
    dw, l1 = _mm_ride(yb, db, ta=True, out_dtype=w.dtype, rider=_ride(_Exchange, d_gathered[1:]), name="mm_out_dw")
    return dy, dw, dout, _landed_sums(l0 + l1, "out")


out_proj.defvjp(_out_proj_fwd, _out_proj_bwd)


@jax.custom_vjp
def mlp(x, g, w_up, s_down, shards):
    return _mlp_fwd(x, g, w_up, s_down, shards)[0]


def _mlp_fwd(x, g, w_up, s_down, shards):
    hm = _rms(x, g, "rms_mlp")
    (u, act), (g_down,) = _mm_ride(hm, w_up, emit_act=True, rider=_Gather(_wire([s_down])), name="mm_up")
    w_down = _row_full(g_down)
    out, gathered = _mm_ride(act, w_down, res=x, rider=_ride(_Gather, _wire(shards)), name="mm_down")
    return (out, gathered), (x, g, w_up, w_down, hm, u, act)


def _mlp_bwd(saved, cts):
    x, g, w_up, w_down, hm, u, act = saved
    dout, d_gathered = cts
    db = dout.astype(MXU)
    du, l0 = _mm_ride(db, w_down, nt=True, u_in=u, out_dtype=MXU, rider=_ride(_Exchange, d_gathered[:1]), name="mm_down_da")
    dw_down, l1 = _mm_ride(act, db, ta=True, out_dtype=w_down.dtype, rider=_ride(_Exchange, d_gathered[1:]), name="mm_down_dw")
    dhm, l_down = _mm_ride(du, w_up, nt=True, rider=_Exchange([_row_blocks(dw_down)]), name="mm_up_dx")
    dw_up = _mm(hm, du, ta=True, out_dtype=w_up.dtype, name="mm_up_dw")
    dx, dg = _rms_bwd(dhm, x, g, "rms_mlp_bwd")
    return dout + dx, dg, dw_up, _landed_sums(l_down, "down")[0], _landed_sums(l0 + l1, "mlp")


mlp.defvjp(_mlp_fwd, _mlp_bwd)


def _order(C, rev):
    i = np.arange(C)
    return (C - 1 - i) if rev else i


def _hg_constants(C, rev):
    p = _order(C, rev)
    pi, pj = p[:, None], p[None, :]
    tri = (pj <= pi).astype(np.float32)
    masks = [((pi // HG_SUB) == (pj // HG_SUB)) & (pj <= pi)]
    h = HG_SUB
    halves = []
    while h < C:
        masks.append(((pi // (2 * h)) == (pj // (2 * h))) & ((pi // h) % 2 == 1) & ((pj // h) % 2 == 0))
        halves.append(h)
        h *= 2
    return tri, np.stack(masks).astype(np.float32), halves


def _hg_tables(b_scr, C, h, rev):
    nb = C // h
    g_rows, e_rows = [], []
    for rb in range(nb):
        if rev:
            e = b_scr[pl.ds(rb * h, 1), :]
            g = b_scr[pl.ds((rb + 1) * h, 1), :] if rb < nb - 1 else None
        else:
            e = b_scr[pl.ds((rb + 1) * h - 1, 1), :]
            g = b_scr[pl.ds(rb * h - 1, 1), :] if rb >= 1 else None
        e_rows.append(jnp.broadcast_to(e, (h, HG_D)))
        g_rows.append(jnp.zeros((h, HG_D), F32) if g is None else jnp.broadcast_to(g, (h, HG_D)))
    return jnp.concatenate(g_rows, axis=0), jnp.concatenate(e_rows, axis=0)


def _hg_gates(z, lb):
    sg = jax.nn.sigmoid(z)
    f = lb + (1.0 - lb) * sg
    return sg, f, 1.0 - f, jnp.log(f)


def _hg_exponents(b, b_scr, C, halves, rev):
    g0, _ = _hg_tables(b_scr, C, HG_SUB, rev)
    p0 = b - g0
    out = [(p0, jnp.minimum(-p0, EXP_CLAMP))]
    for h in halves:
        g, e = _hg_tables(b_scr, C, h, rev)
        out.append((jnp.minimum(b - g, 0.0), jnp.minimum(e - b, 0.0)))
    return out


def _hg_scan_fwd(q, v, z, lb, rev, name):
    S = q.shape[0]
    C = SCAN_C
    nc = S // C
    tri, masks, halves = _hg_constants(C, rev)
    nlev = masks.shape[0]
    end_row = 0 if rev else C - 1

    def body(q_ref, v_ref, z_ref, lb_ref, tri_ref, mask_ref, o_ref, st_ref, s_scr, b_scr):
        @pl.when(pl.program_id(1) == 0)
        def _():
            s_scr[...] = jnp.zeros_like(s_scr)

        for hh in range(SCAN_HPS):
            sl = slice(hh * HG_D, (hh + 1) * HG_D)
            bh = b_scr.at[hh]
            _, f, k, lf = _hg_gates(z_ref[:, sl], lb_ref[:, sl])
            b = _dot01(tri_ref[...], lf)
            bh[...] = b
            qv, vv = q_ref[:, sl], v_ref[:, sl]
            st = s_scr[hh]
            st_ref[hh, 0] = st
            a = jnp.zeros((C, C), F32)
            for lv, (eq, ek) in enumerate(_hg_exponents(b, bh, C, halves, rev)):
                a = a + mask_ref[lv] * _dot_nt(qv * jnp.exp(eq), k * jnp.exp(ek))
            bend = bh[pl.ds(end_row, 1), :]
            o_ref[:, sl] = _dot(a, vv) + _dot_nt(qv * jnp.exp(b), st)
            s_scr[hh] = st * jnp.exp(bend) + _dot_tn(vv, k * jnp.exp(bend - b))

    cidx = (lambda c: nc - 1 - c) if rev else (lambda c: c)
    wid = SCAN_HPS * HG_D
    blk = pl.BlockSpec((C, wid), lambda h, c: (cidx(c), h))
    o, states = pl.pallas_call(
        body,
        out_shape=[jax.ShapeDtypeStruct((S, HGW), F32), jax.ShapeDtypeStruct((HG_HEADS, nc, HG_D, HG_D), F32)],
        grid=(HG_HEADS // SCAN_HPS, nc),
        in_specs=[blk, blk, blk, pl.BlockSpec((1, wid), lambda h, c: (0, h)),
                  pl.BlockSpec((C, C), lambda h, c: (0, 0)), pl.BlockSpec((nlev, C, C), lambda h, c: (0, 0, 0))],
        out_specs=[blk, pl.BlockSpec((SCAN_HPS, 1, HG_D, HG_D), lambda h, c: (h, cidx(c), 0, 0))],
        scratch_shapes=[pltpu.VMEM((SCAN_HPS, HG_D, HG_D), F32), pltpu.VMEM((SCAN_HPS, C, HG_D), F32)],
        name=name, compiler_params=_params("parallel", "arbitrary"),
    )(q, v, z, lb, jnp.asarray(tri, MXU), jnp.asarray(masks))
    return o, states


def _hg_scan_bwd(q, v, z, lb, states, do, rev, name):
    S = q.shape[0]
    C = SCAN_C
    nc = S // C
    tri, masks, halves = _hg_constants(C, rev)
    nlev = masks.shape[0]
    end_row = 0 if rev else C - 1
    masks_t = np.ascontiguousarray(np.transpose(masks, (0, 2, 1)))

    def body(q_ref, v_ref, z_ref, lb_ref, do_ref, st_ref, tri_ref, triT_ref, mask_ref, maskT_ref,
             dq_ref, dv_ref, dz_ref, dlb_ref, dn_scr, b_scr):
        @pl.when(pl.program_id(1) == 0)
        def _():
            dn_scr[...] = jnp.zeros_like(dn_scr)
            dlb_ref[...] = jnp.zeros_like(dlb_ref)

        for hh in range(SCAN_HPS):
            sl = slice(hh * HG_D, (hh + 1) * HG_D)
            bh = b_scr.at[hh]
            lb_v = lb_ref[:, sl]
            sg, f, k, lf = _hg_gates(z_ref[:, sl], lb_v)
            b = _dot01(tri_ref[...], lf)
            bh[...] = b
            qv, vv, dov = q_ref[:, sl], v_ref[:, sl], do_ref[:, sl]
            st, dn = st_ref[hh, 0], dn_scr[hh]
            da = _dot_nt(dov, vv)
            da_t = da.T
            a = jnp.zeros((C, C), F32)
            dq = jnp.zeros((C, HG_D), F32)
            dk = jnp.zeros((C, HG_D), F32)
            for lv, (eq, ek) in enumerate(_hg_exponents(b, bh, C, halves, rev)):
                xq, xk = jnp.exp(eq), jnp.exp(ek)
                qs, ks = qv * xq, k * xk
                a = a + mask_ref[lv] * _dot_nt(qs, ks)
                dq = dq + _dot(mask_ref[lv] * da, ks) * xq
                dk = dk + _dot(maskT_ref[lv] * da_t, qs) * xk
            bend = bh[pl.ds(end_row, 1), :]
            xb, xe, xend = jnp.exp(b), jnp.exp(bend - b), jnp.exp(bend)
            dq = dq + _dot(dov, st) * xb
            dk_state = _dot(vv, dn) * xe
            dk = dk + dk_state
            dv_ref[:, sl] = _dot(a.T, dov) + _dot_nt(k * xe, dn)
            dn_scr[hh] = dn * xend + _dot_tn(dov, qv * xb)
            extra = jnp.sum(k * dk_state, axis=0, keepdims=True) + xend * jnp.sum(st * dn, axis=0, keepdims=True)
            rows = lax.broadcasted_iota(jnp.int32, (C, HG_D), 0)
            db = qv * dq - k * dk + jnp.where(rows == end_row, extra, 0.0)
            df = _dot01(triT_ref[...], db) / f - dk
            dq_ref[:, sl] = dq
            dz_ref[:, sl] = df * (1.0 - lb_v) * sg * (1.0 - sg)
            dlb_ref[:, sl] += jnp.sum(df * (1.0 - sg), axis=0, keepdims=True)

    cidx = (lambda c: c) if rev else (lambda c: nc - 1 - c)
    wid = SCAN_HPS * HG_D
    blk = pl.BlockSpec((C, wid), lambda h, c: (cidx(c), h))
    vec = pl.BlockSpec((1, wid), lambda h, c: (0, h))
    cc = pl.BlockSpec((C, C), lambda h, c: (0, 0))
    lcc = pl.BlockSpec((nlev, C, C), lambda h, c: (0, 0, 0))
    sd = jax.ShapeDtypeStruct((S, HGW), F32)
    return pl.pallas_call(
        body, out_shape=[sd, sd, sd, jax.ShapeDtypeStruct((1, HGW), F32)], grid=(HG_HEADS // SCAN_HPS, nc),
        in_specs=[blk, blk, blk, vec, blk,
                  pl.BlockSpec((SCAN_HPS, 1, HG_D, HG_D), lambda h, c: (h, cidx(c), 0, 0)), cc, cc, lcc, lcc],
        out_specs=[blk, blk, blk, vec],
        scratch_shapes=[pltpu.VMEM((SCAN_HPS, HG_D, HG_D), F32), pltpu.VMEM((SCAN_HPS, C, HG_D), F32)],
        name=name, compiler_params=_params("parallel", "arbitrary"),
    )(q, v, z, lb, do, states, jnp.asarray(tri, MXU), jnp.asarray(tri.T, MXU),
      jnp.asarray(masks), jnp.asarray(masks_t))


def _make_hg_scan(rev):
    tag = "hg_rev" if rev else "hg_fwd"

    @jax.custom_vjp
    def op(q, v, z, lb):
        return fwd(q, v, z, lb)[0]

    def fwd(q, v, z, lb):
        o, states = _hg_scan_fwd(q, v, z, lb[None, :], rev, tag)
        return o, (q, v, z, lb, states)

    def bwd(saved, do):
        q, v, z, lb, states = saved
        dq, dv, dz, dlb = _hg_scan_bwd(q, v, z, lb[None, :], states, do, rev, tag + "_bwd")
        return dq, dv, dz, dlb[0]

    op.defvjp(fwd, bwd)
    return op


def _ret_constants(C, rev):
    hidx = np.arange(RET_HEADS, dtype=np.float64)
    lg = np.log1p(-np.exp2(-5.0 - hidx))
    if rev:
        lg = lg[::-1]
    p = _order(C, rev).astype(np.float64)
    rel = p[:, None] - p[None, :]
    dmat = np.where(rel >= 0, np.exp(lg[:, None, None] * np.maximum(rel, 0.0)), 0.0)
    xi = np.exp(lg[:, None] * (p[None, :] + 1.0))
    zeta = np.exp(lg[:, None] * (C - 1.0 - p[None, :]))
    gc = np.exp(lg * C)
    bc = lambda t: np.ascontiguousarray(np.broadcast_to(t[:, :, None], (RET_HEADS, C, RET_DK))).astype(np.float32)
    gcb = np.ascontiguousarray(np.broadcast_to(gc[:, None, None], (RET_HEADS, 1, RET_DK))).astype(np.float32)
    return dmat.astype(np.float32), bc(xi), bc(zeta), gcb


def _ret_scan_fwd(qh, kh, v, rev, name):
    S = v.shape[0]
    C = SCAN_C
    nc = S // C
    dmat, xi, zeta, gc = _ret_constants(C, rev)

    def body(q_ref, k_ref, v_ref, d_ref, xi_ref, zeta_ref, gc_ref, o_ref, st_ref, s_scr):
        @pl.when(pl.program_id(1) == 0)
        def _():
            s_scr[...] = jnp.zeros_like(s_scr)

        for hh in range(SCAN_HPS):
            sl = slice(hh * RET_DV, (hh + 1) * RET_DV)
            qv, kv = q_ref[hh], k_ref[hh]
            st = s_scr[hh]
            st_ref[hh, 0] = st
            sc = _dot_nt(qv, kv) * d_ref[hh]
            o_ref[:, sl] = _dot(sc, v_ref[:, sl]) + _dot_nt(qv * xi_ref[hh], st)
            s_scr[hh] = st * gc_ref[hh] + _dot_tn(v_ref[:, sl], kv * zeta_ref[hh])

    cidx = (lambda c: nc - 1 - c) if rev else (lambda c: c)
    hk = pl.BlockSpec((SCAN_HPS, C, RET_DK), lambda h, c: (h, cidx(c), 0))
    vblk = pl.BlockSpec((C, SCAN_HPS * RET_DV), lambda h, c: (cidx(c), h))
    tab = pl.BlockSpec((SCAN_HPS, C, RET_DK), lambda h, c: (h, 0, 0))
    return pl.pallas_call(
        body,
        out_shape=[jax.ShapeDtypeStruct((S, RETW), F32), jax.ShapeDtypeStruct((RET_HEADS, nc, RET_DV, RET_DK), F32)],
        grid=(RET_HEADS // SCAN_HPS, nc),
        in_specs=[hk, hk, vblk, pl.BlockSpec((SCAN_HPS, C, C), lambda h, c: (h, 0, 0)), tab, tab,
                  pl.BlockSpec((SCAN_HPS, 1, RET_DK), lambda h, c: (h, 0, 0))],
        out_specs=[vblk, pl.BlockSpec((SCAN_HPS, 1, RET_DV, RET_DK), lambda h, c: (h, cidx(c), 0, 0))],
        scratch_shapes=[pltpu.VMEM((SCAN_HPS, RET_DV, RET_DK), F32)],
        name=name, compiler_params=_params("parallel", "arbitrary"),
    )(qh, kh, v, jnp.asarray(dmat), jnp.asarray(xi), jnp.asarray(zeta), jnp.asarray(gc))


def _ret_scan_bwd(qh, kh, v, states, do, rev, name):
    S = v.shape[0]
    C = SCAN_C
    nc = S // C
    dmat, xi, zeta, gc = _ret_constants(C, rev)

    def body(q_ref, k_ref, v_ref, do_ref, st_ref, d_ref, xi_ref, zeta_ref, gc_ref,
             dq_ref, dk_ref, dv_ref, dn_scr):
        @pl.when(pl.program_id(1) == 0)
        def _():
            dn_scr[...] = jnp.zeros_like(dn_scr)

        for hh in range(SCAN_HPS):
            sl = slice(hh * RET_DV, (hh + 1) * RET_DV)
            qv, kv, vv, dov = q_ref[hh], k_ref[hh], v_ref[:, sl], do_ref[:, sl]
            st, dn = st_ref[hh, 0], dn_scr[hh]
            dm = d_ref[hh]
            sc = _dot_nt(qv, kv) * dm
            dsc = _dot_nt(dov, vv) * dm
            kz = kv * zeta_ref[hh]
            dq_ref[hh] = _dot(dsc, kv) + _dot(dov, st) * xi_ref[hh]
            dk_ref[hh] = _dot(dsc.T, qv) + _dot(vv, dn) * zeta_ref[hh]
            dv_ref[:, sl] = _dot(sc.T, dov) + _dot_nt(kz, dn)
            dn_scr[hh] = dn * gc_ref[hh] + _dot_tn(dov, qv * xi_ref[hh])

    cidx = (lambda c: c) if rev else (lambda c: nc - 1 - c)
    hk = pl.BlockSpec((SCAN_HPS, C, RET_DK), lambda h, c: (h, cidx(c), 0))
    vblk = pl.BlockSpec((C, SCAN_HPS * RET_DV), lambda h, c: (cidx(c), h))
    tab = pl.BlockSpec((SCAN_HPS, C, RET_DK), lambda h, c: (h, 0, 0))
    hs = jax.ShapeDtypeStruct(qh.shape, F32)
    return pl.pallas_call(
        body, out_shape=[hs, hs, jax.ShapeDtypeStruct((S, RETW), F32)], grid=(RET_HEADS // SCAN_HPS, nc),
        in_specs=[hk, hk, vblk, vblk,
                  pl.BlockSpec((SCAN_HPS, 1, RET_DV, RET_DK), lambda h, c: (h, cidx(c), 0, 0)),
                  pl.BlockSpec((SCAN_HPS, C, C), lambda h, c: (h, 0, 0)), tab, tab,
                  pl.BlockSpec((SCAN_HPS, 1, RET_DK), lambda h, c: (h, 0, 0))],
        out_specs=[hk, hk, vblk],
        scratch_shapes=[pltpu.VMEM((SCAN_HPS, RET_DV, RET_DK), F32)],
        name=name, compiler_params=_params("parallel", "arbitrary"),
    )(qh, kh, v, do, states, jnp.asarray(dmat), jnp.asarray(xi), jnp.asarray(zeta), jnp.asarray(gc))


def _make_ret_scan(rev):
    tag = "ret_rev" if rev else "ret_fwd"

    @jax.custom_vjp
    def op(qh, kh, v):
        return fwd(qh, kh, v)[0]

    def fwd(qh, kh, v):
        o, states = _ret_scan_fwd(qh, kh, v, rev, tag)
        return o, (qh, kh, v, states)

    def bwd(saved, do):
        qh, kh, v, states = saved
        return tuple(_ret_scan_bwd(qh, kh, v, states, do, rev, tag + "_bwd"))

    op.defvjp(fwd, bwd)
    return op


def _rope_tables(S, scale):
    half = RET_DK // 2
    inv = ROPE_BASE ** (-np.arange(half, dtype=np.float32) / half)
    ang = np.arange(S, dtype=np.float32)[:, None] * inv[None, :]
    cos, sin = np.cos(ang), np.sin(ang)
    cos_t = np.tile(np.concatenate([cos, cos], axis=1), (1, RET_HEADS)) * scale
    sin_t = np.tile(np.concatenate([-sin, sin], axis=1), (1, RET_HEADS)) * scale
    return cos_t.astype(np.float32), sin_t.astype(np.float32)


def _rope_apply(t, cos_t, sin_t, name):
    S, W = t.shape
    ts = _pick(S, (512, 256, 128))
    half = RET_DK // 2

    def body(t_ref, c_ref, s_ref, o_ref):
        tv = t_ref[...]
        lane = lax.broadcasted_iota(jnp.int32, tv.shape, 1)
        partner = jnp.where(lane % RET_DK < half, pltpu.roll(tv, W - half, 1), pltpu.roll(tv, half, 1))
        o_ref[...] = tv * c_ref[...] + partner * s_ref[...]

    row = pl.BlockSpec((ts, W), lambda i: (i, 0))
    return pl.pallas_call(body, out_shape=jax.ShapeDtypeStruct((S, W), F32), grid=(S // ts,),
                          in_specs=[row, row, row], out_specs=row, name=name,
                          compiler_params=_params("parallel"))(t, jnp.asarray(cos_t), jnp.asarray(sin_t))


def _make_rope(scale, tag):
    def apply(t):
        cos_t, sin_t = _rope_tables(t.shape[0], scale)
        return _rope_apply(t, cos_t, sin_t, tag)

    op = jax.custom_vjp(apply)

    def fwd(t):
        return apply(t), None

    def bwd(_, dout):
        cos_t, sin_t = _rope_tables(dout.shape[0], scale)
        return (_rope_apply(dout, cos_t, -sin_t, tag + "_bwd"),)

    op.defvjp(fwd, bwd)
    return op


def _att_geometry(L):
    tq = _pick(L, (512, 256, 128))
    return tq, L // tq, tq // ATT_HALO


def _att_specs(tq, per):
    main = pl.BlockSpec((1, tq, DIL_HD), lambda b, n: (b, n, 0))
    prev = pl.BlockSpec((1, ATT_HALO, DIL_HD), lambda b, n: (b, jnp.maximum(n * per - 1, 0), 0))
    return main, prev


def _att_valid(n, u, tq, L):
    ii = lax.broadcasted_iota(jnp.int32, (ATT_SB, ATT_WIN), 0)
    jj = lax.broadcasted_iota(jnp.int32, (ATT_SB, ATT_WIN), 1)
    key = n * tq + u * ATT_SB - ATT_HALO + jj
    return (jnp.abs(jj - ATT_HALO - ii) <= ATT_HALO) & (key >= 0) & (key < L)


def _att_fill(buf, prev_ref, main_ref, next_ref, tq):
    buf[pl.ds(0, ATT_HALO), :] = prev_ref[0]
    buf[pl.ds(ATT_HALO, tq), :] = main_ref[0]
    buf[pl.ds(ATT_HALO + tq, ATT_HALO), :] = next_ref[0]


def _att_fwd(q, k, v, bias, dil, name):
    B, L, _ = q.shape
    tq, nt, per = _att_geometry(L)
    last = L // ATT_HALO - 1

    def body(q_ref, kp_ref, k_ref, kn_ref, vp_ref, v_ref, vn_ref, bias_ref, o_ref, lse_ref, kbuf, vbuf):
        n = pl.program_id(1)
        _att_fill(kbuf, kp_ref, k_ref, kn_ref, tq)
        _att_fill(vbuf, vp_ref, v_ref, vn_ref, tq)
        for u in range(tq // ATT_SB):
            rows = pl.ds(u * ATT_SB, ATT_SB)
            win = pl.ds(u * ATT_SB, ATT_WIN)
            s = _dot_nt(q_ref[0, rows, :], kbuf[win, :]) + bias_ref[0]
            s = jnp.where(_att_valid(n, u, tq, L), s, -1e30)
            m = jnp.max(s, axis=-1, keepdims=True)
            p = jnp.exp(s - m)
            den = jnp.sum(p, axis=-1, keepdims=True)
            o_ref[0, rows, :] = _dot(p, vbuf[win, :]) / den
            lse_ref[0, rows, :] = jnp.broadcast_to(m + jnp.log(den), (ATT_SB, DIL_HD))

    main, prev = _att_specs(tq, per)
    nxt = pl.BlockSpec((1, ATT_HALO, DIL_HD), lambda b, n: (b, jnp.minimum((n + 1) * per, last), 0))
    sd = jax.ShapeDtypeStruct((B, L, DIL_HD), F32)
    return pl.pallas_call(
        body, out_shape=[sd, sd], grid=(B, nt),
        in_specs=[main, prev, main, nxt, prev, main, nxt,
                  pl.BlockSpec((1, ATT_SB, ATT_WIN), lambda b, n: (b // dil, 0, 0))],
        out_specs=[main, main],
        scratch_shapes=[pltpu.VMEM((tq + 2 * ATT_HALO, DIL_HD), q.dtype), pltpu.VMEM((tq + 2 * ATT_HALO, DIL_HD), q.dtype)],
        name=name, compiler_params=_params("parallel", "arbitrary"),
    )(q, k, k, k, v, v, v, bias)


def _att_bwd(q, k, v, bias, o, lse, do, dlse, dil, name):
    B, L, _ = q.shape
    tq, nt, per = _att_geometry(L)
    last = L // ATT_HALO - 1

    def body(q_ref, kp_ref, k_ref, kn_ref, vp_ref, v_ref, vn_ref, bias_ref, o_ref, lse_ref, do_ref, dlse_ref,
             dq_ref, dk_ref, dkp_ref, dkn_ref, dv_ref, dvp_ref, dvn_ref, dbias_ref, kbuf, vbuf, dkbuf, dvbuf):
        b, n = pl.program_id(0), pl.program_id(1)

        @pl.when((b % dil == 0) & (n == 0))
        def _():
            dbias_ref[...] = jnp.zeros_like(dbias_ref)

        _att_fill(kbuf, kp_ref, k_ref, kn_ref, tq)
        _att_fill(vbuf, vp_ref, v_ref, vn_ref, tq)
        dkbuf[...] = jnp.zeros_like(dkbuf)
        dvbuf[...] = jnp.zeros_like(dvbuf)
        for u in range(tq // ATT_SB):
            rows = pl.ds(u * ATT_SB, ATT_SB)
            win = pl.ds(u * ATT_SB, ATT_WIN)
            qu, kw, vw = q_ref[0, rows, :], kbuf[win, :], vbuf[win, :]
            dou = do_ref[0, rows, :]
            s = _dot_nt(qu, kw) + bias_ref[0]
            lse_u = jnp.max(lse_ref[0, rows, :], axis=-1, keepdims=True)
            p = jnp.where(_att_valid(n, u, tq, L), jnp.exp(s - lse_u), 0.0)
            corr = jnp.sum(dlse_ref[0, rows, :] - dou * o_ref[0, rows, :], axis=-1, keepdims=True)
            ds = p * (_dot_nt(dou, vw) + corr)
            dq_ref[0, rows, :] = _dot(ds, kw)
            dkbuf[win, :] += _dot(ds.T, qu)
            dvbuf[win, :] += _dot(p.T, dou)
            dbias_ref[0] += ds
        for full, lo, hi in ((dkbuf, dkp_ref, dkn_ref), (dvbuf, dvp_ref, dvn_ref)):
            lo[0, 0] = full[pl.ds(0, ATT_HALO), :]
            hi[0, 0] = full[pl.ds(ATT_HALO + tq, ATT_HALO), :]
        dk_ref[0] = dkbuf[pl.ds(ATT_HALO, tq), :]
        dv_ref[0] = dvbuf[pl.ds(ATT_HALO, tq), :]

    main, prev = _att_specs(tq, per)
    nxt = pl.BlockSpec((1, ATT_HALO, DIL_HD), lambda b, n: (b, jnp.minimum((n + 1) * per, last), 0))
    halo = pl.BlockSpec((1, 1, ATT_HALO, DIL_HD), lambda b, n: (b, n, 0, 0))
    bias_spec = pl.BlockSpec((1, ATT_SB, ATT_WIN), lambda b, n: (b // dil, 0, 0))
    sd = jax.ShapeDtypeStruct((B, L, DIL_HD), F32)
    hd = jax.ShapeDtypeStruct((B, nt, ATT_HALO, DIL_HD), F32)
    width = tq + 2 * ATT_HALO
    dq, dk, dkp, dkn, dv, dvp, dvn, dbias = pl.pallas_call(
        body, out_shape=[sd, sd, hd, hd, sd, hd, hd, jax.ShapeDtypeStruct(bias.shape, F32)], grid=(B, nt),
        in_specs=[main, prev, main, nxt, prev, main, nxt, bias_spec, main, main, main, main],
        out_specs=[main, main, halo, halo, main, halo, halo, bias_spec],
        scratch_shapes=[pltpu.VMEM((width, DIL_HD), q.dtype), pltpu.VMEM((width, DIL_HD), q.dtype),
                        pltpu.VMEM((width, DIL_HD), F32), pltpu.VMEM((width, DIL_HD), F32)],
        name=name, compiler_params=_params("arbitrary", "arbitrary"),
    )(q, k, k, k, v, v, v, bias, o, lse, do, dlse)

    def fold(mainv, lo, hi):
        t = mainv.reshape(B, nt, tq, DIL_HD)
        if nt > 1:
            t = t.at[:, :-1, tq - ATT_HALO:, :].add(lo[:, 1:])
            t = t.at[:, 1:, :ATT_HALO, :].add(hi[:, :-1])
        return t.reshape(B, L, DIL_HD)

    return dq, fold(dk, dkp, dkn), fold(dv, dvp, dvn), dbias


def _make_attention(dil):
    tag = "att_d%d" % dil

    @jax.custom_vjp
    def op(q, k, v, bias):
        return fwd(q, k, v, bias)[0]

    def fwd(q, k, v, bias):
        qb, kb, vb = q.astype(MXU), k.astype(MXU), v.astype(MXU)
        o, lse = _att_fwd(qb, kb, vb, bias, dil, tag)
        return (o, lse), (qb, kb, vb, bias, o, lse)

    def bwd(saved, cts):
        qb, kb, vb, bias, o, lse = saved
        do, dlse = cts
        return tuple(_att_bwd(qb, kb, vb, bias, o, lse, do, dlse, dil, tag + "_bwd"))

    op.defvjp(fwd, bwd)
    return op


def _merge_weights(l0, l1, l2):
    m = jnp.maximum(jnp.maximum(l0, l1), l2)
    e0, e1, e2 = jnp.exp(l0 - m), jnp.exp(l1 - m), jnp.exp(l2 - m)
    inv = 1.0 / (e0 + e1 + e2)
    return e0 * inv, e1 * inv, e2 * inv


def _merge_call(body, n_in, n_out, shape, name):
    R, W = shape
    ts = _pick(R, (1024, 512, 256, 128))
    row = pl.BlockSpec((ts, W), lambda i: (i, 0))
    sd = jax.ShapeDtypeStruct(shape, F32)
    return pl.pallas_call(body, out_shape=[sd] * n_out, grid=(R // ts,), in_specs=[row] * n_in,
                          out_specs=[row] * n_out, name=name, compiler_params=_params("parallel"))


@jax.custom_vjp
def dil_merge(o0, o1, o2, l0, l1, l2):
    return _dil_merge_fwd(o0, o1, o2, l0, l1, l2)[0]


def _dil_merge_fwd(*args):
    def body(o0, o1, o2, l0, l1, l2, out):
        w0, w1, w2 = _merge_weights(l0[...], l1[...], l2[...])
        out[...] = w0 * o0[...] + w1 * o1[...] + w2 * o2[...]

    return _merge_call(body, 6, 1, args[0].shape, "dil_merge")(*args)[0], args


def _dil_merge_bwd(args, dout):
    def body(o0, o1, o2, l0, l1, l2, d, do0, do1, do2, dl0, dl1, dl2):
        ws = _merge_weights(l0[...], l1[...], l2[...])
        dv = d[...]
        dws = [dv * o[...] for o in (o0, o1, o2)]
        mean = ws[0] * dws[0] + ws[1] * dws[1] + ws[2] * dws[2]
        for w, dw, do_ref, dl_ref in zip(ws, dws, (do0, do1, do2), (dl0, dl1, dl2)):
            do_ref[...] = w * dv
            dl_ref[...] = w * (dw - mean)

    return tuple(_merge_call(body, 7, 6, args[0].shape, "dil_merge_bwd")(*args, dout))


dil_merge.defvjp(_dil_merge_fwd, _dil_merge_bwd)


def _t5_bucket(rel):
    nb = REL_BUCKETS // 2
    max_exact = nb // 2
    sign_off = np.where(rel > 0, nb, 0)
    n = np.abs(rel)
    nf = np.maximum(n, 1).astype(np.float32)
    large = max_exact + (np.log(nf / np.float32(max_exact)) / np.float32(math.log(REL_MAX_DIST / max_exact))
                         * np.float32(nb - max_exact)).astype(np.int32)
    large = np.minimum(large, nb - 1)
    return sign_off + np.where(n < max_exact, n, large)


def _loss_grad(xf, target):
    S, D = xf.shape
    ts = _pick(S, (512, 256, 128))

    def body(x_ref, t_ref, dy_ref, part_ref):
        @pl.when(pl.program_id(0) == 0)
        def _():
            part_ref[...] = jnp.zeros_like(part_ref)

        err = x_ref[...] - t_ref[...]
        dy_ref[...] = err * (1.0 / D)
        part_ref[...] += jnp.sum(err * err, axis=0, keepdims=True)

    row = pl.BlockSpec((ts, D), lambda i: (i, 0))
    return pl.pallas_call(body, out_shape=[jax.ShapeDtypeStruct((S, D), F32), jax.ShapeDtypeStruct((1, D), F32)],
                          grid=(S // ts,), in_specs=[row, row], out_specs=[row, pl.BlockSpec((1, D), lambda i: (0, 0))],
                          name="loss_grad", compiler_params=_params("arbitrary"))(xf, target)


def _adamw_math(g, w, m, v):
    m = ADAM_B1 * m + (1.0 - ADAM_B1) * g
    v = ADAM_B2 * v + (1.0 - ADAM_B2) * (g * g)
    m_hat = m / (1.0 - ADAM_B1 ** ADAM_STEP)
    v_hat = v / (1.0 - ADAM_B2 ** ADAM_STEP)
    delta = -ADAM_LR * (m_hat / (jnp.sqrt(v_hat) + ADAM_EPS) + ADAM_WD * w)
    return delta, m, v


def _adamw(g, w, m, v, name):
    Lw, R, C = w.shape
    tr = _pick(R, (256, 128, 64, 32, 16, 8))

    def body(g_ref, w_ref, m_ref, v_ref, d_ref, nm_ref, nv_ref):
        d_ref[0], nm_ref[0], nv_ref[0] = _adamw_math(g_ref[0], w_ref[0], m_ref[0], v_ref[0])

    blk = pl.BlockSpec((1, tr, C), lambda l, i: (l, i, 0))
    sd = jax.ShapeDtypeStruct(w.shape, F32)
    return pl.pallas_call(body, out_shape=[sd] * 3, grid=(Lw, R // tr), in_specs=[blk] * 4, out_specs=[blk] * 3,
                          name=name, compiler_params=_params("parallel", "parallel"))(g, w, m, v)


def _mesh_pos():
    return lax.axis_index("x"), lax.axis_index("y"), lax.axis_index("c")


def _slot(x, y, c):
    return 4 * x + 2 * y + c


class _Comm:
    def __init__(self, arrays):
        self.arrays = list(arrays)

    def sem_shapes(self):
        na = len(self.arrays)
        return [pltpu.SemaphoreType.DMA((7 * na,)), pltpu.SemaphoreType.DMA((7 * na,)), pltpu.SemaphoreType.DMA((na,))]

    def mid(self, ins, outs, sems):
        pass

    def call(self, name):
        na = len(self.arrays)

        def body(*refs):
            ins, outs, sems = refs[:na], refs[na:2 * na], refs[2 * na:]
            self.start(ins, outs, sems)
            self.mid(ins, outs, sems)
            self.finish(ins, outs, sems)

        anyspec = pl.BlockSpec(memory_space=pl.ANY)
        return pl.pallas_call(body, out_shape=self.out_shapes(), in_specs=[anyspec] * na, out_specs=[anyspec] * na,
                              scratch_shapes=self.sem_shapes(), name=name)(*self.arrays)


class _Gather(_Comm):
    def out_shapes(self):
        return [jax.ShapeDtypeStruct((N_DEV,) + b.shape, b.dtype) for b in self.arrays]

    def _copies(self, ins, outs, sems):
        send_sems, recv_sems, local_sems = sems
        x, y, c = _mesh_pos()
        me, sibling = (x, y, c), (x, y, 1 - c)
        chips = [(1 - x, y), (x, 1 - y), (1 - x, 1 - y)]
        per = []
        for a in range(len(self.arrays)):
            def copy(k, block, to, src=None, a=a):
                dst = outs[a].at[_slot(*block)]
                return pltpu.make_async_remote_copy(
                    src_ref=dst if src is None else src, dst_ref=dst,
                    send_sem=send_sems.at[7 * a + k], recv_sem=recv_sems.at[7 * a + k],
                    device_id=to, device_id_type=MESH_ID)

            per.append(dict(
                mine=pltpu.make_async_copy(ins[a], outs[a].at[_slot(*me)], local_sems.at[a]),
                first=[copy(0, me, sibling, src=ins[a])] + [copy(1 + j, me, (*ch, c), src=ins[a]) for j, ch in enumerate(chips)],
                passed=[copy(4 + j, (*ch, c), sibling) for j, ch in enumerate(chips)],
                over_ici=[copy(1 + j, (*ch, c), me) for j, ch in enumerate(chips)],
                from_sibling=[copy(0, sibling, me)] + [copy(4 + j, (*ch, 1 - c), me) for j, ch in enumerate(chips)]))
        return per

    def start(self, ins, outs, sems):
        for p in self._copies(ins, outs, sems):
            p["mine"].start()
            for cp in p["first"]:
                cp.start()

    def mid(self, ins, outs, sems):
        for p in self._copies(ins, outs, sems):
            for arrived, onward in zip(p["over_ici"], p["passed"]):
                arrived.wait_recv()
                onward.start()

    def finish(self, ins, outs, sems):
        for p in self._copies(ins, outs, sems):
            for cp in p["from_sibling"]:
                cp.wait_recv()
            for cp in p["first"] + p["passed"]:
                cp.wait_send()
            p["mine"].wait()


class _Exchange(_Comm):
    def out_shapes(self):
        return [jax.ShapeDtypeStruct(f.shape, f.dtype) for f in self.arrays]

    def _copies(self, ins, outs, sems):
        send_sems, recv_sems, local_sems = sems
        x, y, c = _mesh_pos()
        my_slot = _slot(x, y, c)
        local, remote = [], []
        for a in range(len(self.arrays)):
            local.append(pltpu.make_async_copy(ins[a].at[my_slot], outs[a].at[my_slot], local_sems.at[a]))
            for k in range(1, N_DEV):
                px = 1 - x if k & 4 else x
                py = 1 - y if k & 2 else y
                pc = 1 - c if k & 1 else c
                remote.append(pltpu.make_async_remote_copy(
                    src_ref=ins[a].at[_slot(px, py, pc)], dst_ref=outs[a].at[my_slot],
                    send_sem=send_sems.at[7 * a + k - 1], recv_sem=recv_sems.at[7 * a + k - 1],
                    device_id=(px, py, pc), device_id_type=MESH_ID))
        return local, remote

    def start(self, ins, outs, sems):
        local, remote = self._copies(ins, outs, sems)
        for cp in local + remote:
            cp.start()

    def finish(self, ins, outs, sems):
        local, remote = self._copies(ins, outs, sems)
        for cp in remote + local:
            cp.wait()


def _sum_slots(parts, name):
    _, R, C = parts.shape
    tr = _pick(R, (256, 128, 64, 32, 16, 8))

    def body(p_ref, o_ref):
        g = p_ref[0].astype(F32)
        for s in range(1, N_DEV):
            g = g + p_ref[s].astype(F32)
        o_ref[...] = g

    return pl.pallas_call(body, out_shape=jax.ShapeDtypeStruct((R, C), F32), grid=(R // tr,),
                          in_specs=[pl.BlockSpec((N_DEV, tr, C), lambda i: (0, i, 0))],
                          out_specs=pl.BlockSpec((tr, C), lambda i: (i, 0)), name=name,
                          compiler_params=_params("parallel"))(parts)


_hg_fwd_op, _hg_rev_op = _make_hg_scan(False), _make_hg_scan(True)
_ret_fwd_op, _ret_rev_op = _make_ret_scan(False), _make_ret_scan(True)
_rope_q, _rope_k = _make_rope(1.0, "rope_q"), _make_rope(RET_DK ** -0.5, "rope_k")
_hg_post = _make_gnorm(2, True, HG_D, False, 1.0, "hg_post")
_ret_post = _make_gnorm(2, True, RET_DV, True, 1.0, "ret_post")
_qk_norms = _make_head_norms((DIL_HD ** -0.5, 1.0) * len(DIL_GROUPS), "dil_qk_norms")
_att_ops = {dil: _make_attention(dil) for _, dil in DIL_GROUPS}


def _to_heads(t, d):
    S, W = t.shape
    return t.reshape(S, W // d, d).transpose(1, 0, 2)


def _dilated_mixer(parts, rel_bias, q_gain, k_gain):
    S = parts[0].shape[0]
    qg, kg = jnp.tile(q_gain, DIL_SLOTS), jnp.tile(k_gain, DIL_SLOTS)
    ii = np.arange(ATT_SB)[:, None]
    jj = np.arange(ATT_WIN)[None, :]
    n_groups = len(DIL_GROUPS)
    normed = _qk_norms(*[parts[3 * g + t] for g in range(n_groups) for t in (0, 1)], *([qg, kg] * n_groups))
    outs, lses = [], []
    for g, (window, dil) in enumerate(DIL_GROUPS):
        assert window // (2 * dil) == ATT_HALO
        L = S // dil

        def to_res(t):
            return t.reshape(L, dil, DIL_SLOTS, DIL_HD).transpose(2, 1, 0, 3).reshape(DIL_SLOTS * dil, L, DIL_HD)

        def from_res(t):
            return t.reshape(DIL_SLOTS, dil, L, DIL_HD).transpose(0, 2, 1, 3).reshape(DIL_SLOTS * S, DIL_HD)

        onehot = (_t5_bucket((jj - ATT_HALO - ii) * dil)[:, :, None] == np.arange(REL_BUCKETS)).astype(np.float32)
        bias = jnp.einsum("ijb,bh->hij", onehot, rel_bias[:, g * DIL_SLOTS:(g + 1) * DIL_SLOTS],
                          precision=lax.Precision.HIGHEST)
        o, lse = _att_ops[dil](to_res(normed[2 * g]), to_res(normed[2 * g + 1]), to_res(parts[3 * g + 2]), bias)
        outs.append(from_res(o))
        lses.append(from_res(lse))
    merged = dil_merge(*outs, *lses)
    return merged.reshape(DIL_SLOTS, S, DIL_HD).transpose(1, 0, 2).reshape(S, DILW)


def _mixers(h, p):
    offs = np.cumsum(IN_SPLITS)[:-1].tolist()
    parts = jnp.split(h, offs, axis=-1)
    q, v = parts[0], parts[1]
    y_a = _hg_post(_hg_fwd_op(q, v, parts[2], p["lb_fwd"]), _hg_rev_op(q, v, parts[3], p["lb_bwd"]), p["hg_norm"], parts[4])
    qh = _to_heads(_rope_q(parts[5]), RET_DK)
    kh = _to_heads(_rope_k(parts[6]), RET_DK)
    y_b = _ret_post(_ret_fwd_op(qh, kh, parts[7]), _ret_rev_op(qh, kh, parts[7]), p["ret_norm"], parts[8])
    y_c = _dilated_mixer(parts[9:], p["rel_bias"], p["q_norm"], p["k_norm"])
    return jnp.concatenate([y_a, y_b, y_c], axis=-1)


def _col_full(blocks):
    return blocks.transpose(1, 0, 2).reshape(blocks.shape[1], -1)


def _col_blocks(full):
    K = full.shape[0]
    return full.reshape(K, N_DEV, -1).transpose(1, 0, 2)


def _row_full(blocks):
    return blocks.reshape(-1, blocks.shape[2])


def _row_blocks(full):
    return full.reshape(N_DEV, -1, full.shape[1])


SMALL_NAMES = ("norm_mix", "norm_mlp", "hg_lb_fwd", "hg_lb_bwd", "hg_norm", "ret_norm", "q_norm", "k_norm", "rel_bias")


def _forward(x, w_in, shards, small):
    depth = len(shards)
    lb_f = jnp.cumsum(jax.nn.softmax(small["hg_lb_fwd"], axis=0), axis=0)
    lb_b = jnp.cumsum(jax.nn.softmax(small["hg_lb_bwd"], axis=0), axis=0)
    for l in range(depth):
        p = {k: small[k][l] for k in ("norm_mix", "norm_mlp", "hg_norm", "ret_norm", "q_norm", "k_norm")}
        p["lb_fwd"], p["lb_bwd"] = lb_f[l] - lb_f[0], lb_b[l] - lb_b[0]
        p["rel_bias"] = small["rel_bias"]
        sh = shards[l]
        h, (g_up, g_out) = norm_matmul(x, p["norm_mix"], w_in, (sh["up"], sh["out"]))
        x, _ = out_proj(_mixers(h, p), _row_full(g_out), x, ())
        halves = ()
        if l < depth - 1:
            nxt = shards[l + 1]["in"]
            halves = (nxt[:nxt.shape[0] // 2], nxt[nxt.shape[0] // 2:])
        x, g_in = mlp(x, p["norm_mlp"], _col_full(g_up), sh["down"], halves)
        if halves:
            w_in = jnp.concatenate([_col_full(t) for t in g_in], axis=0)
    return x


def kernel(x, w_in, w_out, w_up, w_down, norm_mix, norm_mlp, hg_lb_fwd, hg_lb_bwd, hg_norm, ret_norm, q_norm, k_norm, rel_bias, loss_target, m_w_in, m_w_out, m_w_up, m_w_down, m_norm_mix, m_norm_mlp, m_hg_lb_fwd, m_hg_lb_bwd, m_hg_norm, m_ret_norm, m_q_norm, m_k_norm, m_rel_bias, v_w_in, v_w_out, v_w_up, v_w_down, v_norm_mix, v_norm_mlp, v_hg_lb_fwd, v_hg_lb_bwd, v_hg_norm, v_ret_norm, v_q_norm, v_k_norm, v_rel_bias):
    depth = w_in.shape[0]
    big = (w_in, w_out, w_up, w_down)
    big_m = (m_w_in, m_w_out, m_w_up, m_w_down)
    big_v = (v_w_in, v_w_out, v_w_up, v_w_down)
    small = dict(zip(SMALL_NAMES, (norm_mix, norm_mlp, hg_lb_fwd, hg_lb_bwd, hg_norm, ret_norm, q_norm, k_norm, rel_bias)))
    small_m = (m_norm_mix, m_norm_mlp, m_hg_lb_fwd, m_hg_lb_bwd, m_hg_norm, m_ret_norm, m_q_norm, m_k_norm, m_rel_bias)
    small_v = (v_norm_mix, v_norm_mlp, v_hg_lb_fwd, v_hg_lb_bwd, v_hg_norm, v_ret_norm, v_q_norm, v_k_norm, v_rel_bias)

    (gathered0,) = _Gather([w_in[0].astype(WIRE)]).call("gather_w_in0")
    names = ("in", "out", "up", "down")
    shards = [{n: w[l] for n, w in zip(names, big) if l or n != "in"} for l in range(depth)]

    xf, vjp = jax.vjp(_forward, x[0], _col_full(gathered0), shards, small)
    dy, part = _loss_grad(xf, loss_target[0])
    loss = lax.psum(0.5 / xf.shape[1] * jnp.sum(part), ("x", "y", "c"))
    dx, dw_in0, dshards, dsmall = vjp(dy)

    (landed0,) = _Exchange([_col_blocks(dw_in0)]).call("exchange_w_in0")
    dshards[0]["in"] = _sum_slots(landed0, "sum_w_in0")
    big_out = []
    for i, n in enumerate(names):
        g = jnp.stack([dshards[l][n] for l in range(depth)])
        big_out.append((g,) + tuple(_adamw(g, big[i], big_m[i], big_v[i], "adamw_" + n)))

    flat = jnp.concatenate([dsmall[n].reshape(-1) for n in SMALL_NAMES])
    n_small = flat.shape[0]
    rows = -(-n_small // 1024) * 8
    pad = lambda t: jnp.pad(t, (0, rows * 128 - n_small)).reshape(1, rows, 128)
    (small_parts,) = _Gather([pad(flat)[0]]).call("gather_small_grads")
    cat = lambda ts: pad(jnp.concatenate([t.reshape(-1) for t in ts]))
    g_small = _sum_slots(small_parts, "sum_small_grads")[None]
    small_out = (g_small,) + tuple(_adamw(g_small, cat([small[n] for n in SMALL_NAMES]), cat(small_m), cat(small_v), "adamw_small"))

    def unpack(t):
        t = t.reshape(-1)
        out, off = [], 0
        for n in SMALL_NAMES:
            size = small[n].size
            out.append(t[off:off + size].reshape(small[n].shape))
            off += size
        return out

    res = [loss, dx[None]]
    for kind in range(4):
        res += [o[kind] for o in big_out] + unpack(small_out[kind])
    return tuple(res)
```

```python
import functools
import math

import numpy as np
import jax
import jax.numpy as jnp
from jax import lax
from jax.experimental import pallas as pl
from jax.experimental.pallas import tpu as pltpu

F32 = jnp.float32
MXU = jnp.bfloat16
WIRE = jnp.bfloat16
EPS = 1e-6
N_DEV = 8
VMEM_LIMIT = 48 * 1024 * 1024

HG_HEADS, HG_D = 6, 128
RET_HEADS, RET_DK, RET_DV = 6, 64, 128
DIL_SLOTS, DIL_HD = 4, 128
DIL_GROUPS = ((128, 1), (512, 4), (2048, 16))
HGW = HG_HEADS * HG_D
RETW = RET_HEADS * RET_DV
DILW = DIL_SLOTS * DIL_HD
IN_SPLITS = (HGW, HGW, HGW, HGW, HGW, RET_HEADS * RET_DK, RET_HEADS * RET_DK, RETW, RETW) + (DILW,) * 9
REL_BUCKETS, REL_MAX_DIST = 32, 1024
ROPE_BASE = 10000.0
ADAM_LR, ADAM_B1, ADAM_B2, ADAM_EPS, ADAM_WD, ADAM_STEP = 0.001, 0.9, 0.999, 1e-08, 0.01, 10

SCAN_C = 128
SCAN_HPS = 6
HG_SUB = 16
EXP_CLAMP = 60.0
ATT_SB, ATT_HALO = 128, 64
ATT_WIN = ATT_SB + 2 * ATT_HALO
MESH_ID = pl.DeviceIdType.MESH


def _pick(n, prefs):
    for p in prefs:
        if n % p == 0:
            return p
    return n


def _params(*sem):
    return pltpu.CompilerParams(dimension_semantics=sem, vmem_limit_bytes=VMEM_LIMIT)


def _dot(a, b):
    return lax.dot_general(a.astype(MXU), b.astype(MXU), (((1,), (0,)), ((), ())), preferred_element_type=F32)


def _dot_nt(a, b):
    return lax.dot_general(a.astype(MXU), b.astype(MXU), (((1,), (1,)), ((), ())), preferred_element_type=F32)


def _dot_tn(a, b):
    return lax.dot_general(a.astype(MXU), b.astype(MXU), (((0,), (0,)), ((), ())), preferred_element_type=F32)


def _dot01(sel, x):
    if MXU == F32:
        return _dot(sel, x)
    hi = x.astype(MXU)
    r1 = x - hi.astype(F32)
    mid = r1.astype(MXU)
    lo = (r1 - mid.astype(F32)).astype(MXU)
    return _dot(sel, hi) + _dot(sel, mid) + _dot(sel, lo)


def _mm(a, b, *, nt=False, ta=False, out_dtype=F32, res=None, u_in=None, emit_act=False, rider=None, name):
    M, K = a.shape[::-1] if ta else a.shape
    N = b.shape[0] if nt else b.shape[1]
    tm = _pick(M, (1024, 512, 256, 128))
    tn = _pick(N, (1024, 768, 512, 384, 256, 128))
    tk = _pick(K, (2048, 1536, 1024, 512, 256, 128))
    ni, nj, nk = M // tm, N // tn, K // tk
    n_ride = len(rider.arrays) if rider is not None else 0

    def body(*refs):
        it = iter(refs)
        a_ref, b_ref = next(it), next(it)
        res_ref = next(it) if res is not None else None
        u_ref = next(it) if u_in is not None else None
        ride_in = [next(it) for _ in range(n_ride)]
        o_ref = next(it)
        act_ref = next(it) if emit_act else None
        ride_out = [next(it) for _ in range(n_ride)]
        acc_ref = next(it)
        sems = list(it)
        i, j, k = pl.program_id(0), pl.program_id(1), pl.program_id(2)

        step = (i * nj + j) * nk + k
        if rider is not None:
            pl.when(step == 0)(functools.partial(rider.start, ride_in, ride_out, sems))
            pl.when(step == (3 * ni // 4) * nj * nk)(functools.partial(rider.mid, ride_in, ride_out, sems))

        @pl.when(k == 0)
        def _():
            acc_ref[...] = jnp.zeros_like(acc_ref)

        acc_ref[...] += (_dot_tn if ta else _dot_nt if nt else _dot)(a_ref[...], b_ref[...])

        @pl.when(k == nk - 1)
        def _():
            r = acc_ref[...]
            if res_ref is not None:
                r = r + res_ref[...]
            if u_ref is not None:
                r = r * (2.0 * jnp.maximum(u_ref[...], 0.0))
            o_ref[...] = r.astype(o_ref.dtype)
            if act_ref is not None:
                t = jnp.maximum(r, 0.0)
                act_ref[...] = (t * t).astype(act_ref.dtype)

        if rider is not None:
            pl.when(step == ni * nj * nk - 1)(functools.partial(rider.finish, ride_in, ride_out, sems))

    mn = pl.BlockSpec((tm, tn), lambda i, j, k: (i, j))
    anyspec = pl.BlockSpec(memory_space=pl.ANY)
    in_specs = [pl.BlockSpec((tk, tm), lambda i, j, k: (k, i)) if ta else pl.BlockSpec((tm, tk), lambda i, j, k: (i, k)),
                pl.BlockSpec((tn, tk), lambda i, j, k: (j, k)) if nt else pl.BlockSpec((tk, tn), lambda i, j, k: (k, j))]
    args = [a, b]
    for extra in (res, u_in):
        if extra is not None:
            in_specs.append(mn)
            args.append(extra)
    out_shape = [jax.ShapeDtypeStruct((M, N), out_dtype)]
    out_specs = [mn]
    if emit_act:
        out_shape.append(jax.ShapeDtypeStruct((M, N), MXU))
        out_specs.append(mn)
    scratch = [pltpu.VMEM((tm, tn), F32)]
    if rider is not None:
        in_specs += [anyspec] * n_ride
        args += list(rider.arrays)
        out_shape += rider.out_shapes()
        out_specs += [anyspec] * n_ride
        scratch += rider.sem_shapes()
    sem = ("arbitrary",) * 3 if rider is not None else ("parallel", "parallel", "arbitrary")
    out = pl.pallas_call(
        body, out_shape=out_shape, grid=(ni, nj, nk), in_specs=in_specs, out_specs=out_specs,
        scratch_shapes=scratch, name=name, compiler_params=_params(*sem))(*args)
    return out if (emit_act or rider is not None) else out[0]


def _gnorm_stats(x, center):
    if center:
        x = x - jnp.mean(x, axis=-1, keepdims=True)
    r = lax.rsqrt(jnp.mean(x * x, axis=-1, keepdims=True) + EPS)
    return x * r, r


def _silu_parts(gt):
    sg = jax.nn.sigmoid(gt)
    return gt * sg, sg * (1.0 + gt * (1.0 - sg))


def _gnorm_fwd(xs, gain, gate, *, group, center, scale, out_dtype, name):
    S, W = xs[0].shape
    ts = _pick(S, (512, 256, 128))
    nx = len(xs)

    def body(*refs):
        x_refs, g_ref = refs[:nx], refs[nx]
        gate_ref = refs[nx + 1] if gate is not None else None
        o_ref = refs[-1]
        for gi in range(W // group):
            sl = slice(gi * group, (gi + 1) * group)
            x = x_refs[0][:, sl]
            for xr in x_refs[1:]:
                x = x + xr[:, sl]
            n, _ = _gnorm_stats(x, center)
            y = n * (g_ref[:, sl] * scale)
            if gate_ref is not None:
                y = y * _silu_parts(gate_ref[:, sl])[0]
            o_ref[:, sl] = y.astype(o_ref.dtype)

    row = pl.BlockSpec((ts, W), lambda i: (i, 0))
    vec = pl.BlockSpec((1, W), lambda i: (0, 0))
    args = list(xs) + [gain] + ([gate] if gate is not None else [])
    in_specs = [row] * nx + [vec] + ([row] if gate is not None else [])
    return pl.pallas_call(body, out_shape=jax.ShapeDtypeStruct((S, W), out_dtype), grid=(S // ts,),
                          in_specs=in_specs, out_specs=row, name=name, compiler_params=_params("parallel"))(*args)


def _gnorm_bwd(dy, xs, gain, gate, *, group, center, scale, name):
    S, W = xs[0].shape
    ts = _pick(S, (512, 256, 128))
    nx = len(xs)

    def body(*refs):
        dy_ref = refs[0]
        x_refs, g_ref = refs[1:1 + nx], refs[1 + nx]
        gate_ref = refs[2 + nx] if gate is not None else None
        outs = refs[(3 + nx if gate is not None else 2 + nx):]
        dx_ref, dg_ref = outs[0], outs[1]
        dgate_ref = outs[2] if gate is not None else None

        @pl.when(pl.program_id(0) == 0)
        def _():
            dg_ref[...] = jnp.zeros_like(dg_ref)

        for gi in range(W // group):
            sl = slice(gi * group, (gi + 1) * group)
            x = x_refs[0][:, sl]
            for xr in x_refs[1:]:
                x = x + xr[:, sl]
            n, r = _gnorm_stats(x, center)
            dyv = dy_ref[:, sl].astype(F32)
            g = g_ref[:, sl] * scale
            if gate_ref is not None:
                act, dact = _silu_parts(gate_ref[:, sl])
                dgate_ref[:, sl] = dyv * n * g * dact
                dyv = dyv * act
            dg_ref[:, sl] += jnp.sum(dyv * n, axis=0, keepdims=True) * scale
            dn = dyv * g
            t = dn - n * jnp.mean(dn * n, axis=-1, keepdims=True)
            if center:
                t = t - jnp.mean(dn, axis=-1, keepdims=True)
            dx_ref[:, sl] = r * t

    row = pl.BlockSpec((ts, W), lambda i: (i, 0))
    vec = pl.BlockSpec((1, W), lambda i: (0, 0))
    args = [dy] + list(xs) + [gain] + ([gate] if gate is not None else [])
    in_specs = [row] * (1 + nx) + [vec] + ([row] if gate is not None else [])
    out_shape = [jax.ShapeDtypeStruct((S, W), F32), jax.ShapeDtypeStruct((1, W), F32)]
    out_specs = [row, vec]
    if gate is not None:
        out_shape.append(jax.ShapeDtypeStruct((S, W), F32))
        out_specs.append(row)
    out = pl.pallas_call(body, out_shape=out_shape, grid=(S // ts,), in_specs=in_specs, out_specs=out_specs,
                         name=name, compiler_params=_params("arbitrary"))(*args)
    return out[0], out[1], (out[2] if gate is not None else None)


def _head_norms(xs, gains, scales, dys, name):
    n = len(xs)
    S, W = xs[0].shape
    ts = _pick(S, (256, 128))
    bwd = dys is not None

    def body(*refs):
        x_refs, g_refs = refs[:n], refs[n:2 * n]
        dy_refs = refs[2 * n:3 * n] if bwd else ()
        outs = refs[(3 * n if bwd else 2 * n):]
        if bwd:
            @pl.when(pl.program_id(0) == 0)
            def _():
                for dg_ref in outs[n:]:
                    dg_ref[...] = jnp.zeros_like(dg_ref)

        for a in range(n):
            for gi in range(W // DIL_HD):
                sl = slice(gi * DIL_HD, (gi + 1) * DIL_HD)
                nrm, r = _gnorm_stats(x_refs[a][:, sl], False)
                g = g_refs[a][:, sl] * scales[a]
                if not bwd:
                    outs[a][:, sl] = nrm * g
                    continue
                dyv = dy_refs[a][:, sl]
                outs[n + a][:, sl] += jnp.sum(dyv * nrm, axis=0, keepdims=True) * scales[a]
                dn = dyv * g
                outs[a][:, sl] = r * (dn - nrm * jnp.mean(dn * nrm, axis=-1, keepdims=True))

    row = pl.BlockSpec((ts, W), lambda i: (i, 0))
    vec = pl.BlockSpec((1, W), lambda i: (0, 0))
    sd = jax.ShapeDtypeStruct((S, W), F32)
    out_shape = [sd] * n + ([jax.ShapeDtypeStruct((1, W), F32)] * n if bwd else [])
    return pl.pallas_call(
        body, out_shape=out_shape, grid=(S // ts,), in_specs=[row] * n + [vec] * n + ([row] * n if bwd else []),
        out_specs=[row] * n + ([vec] * n if bwd else []), name=name,
        compiler_params=_params("arbitrary" if bwd else "parallel"))(*xs, *gains, *(dys or ()))


def _make_head_norms(scales, tag):
    n = len(scales)

    def apply(*args):
        return tuple(_head_norms(args[:n], [g[None, :] for g in args[n:]], scales, None, tag))

    op = jax.custom_vjp(apply)

    def fwd(*args):
        return apply(*args), args

    def bwd(args, dys):
        out = _head_norms(args[:n], [g[None, :] for g in args[n:]], scales, list(dys), tag + "_bwd")
        return tuple(out[:n]) + tuple(d[0] for d in out[n:])

    op.defvjp(fwd, bwd)
    return op


def _make_gnorm(nx, has_gate, group, center, scale, tag):
    kw = dict(group=group, center=center, scale=scale)

    @jax.custom_vjp
    def op(*args):
        return fwd(*args)[0]

    def fwd(*args):
        xs, gain = args[:nx], args[nx]
        gate = args[nx + 1] if has_gate else None
        y = _gnorm_fwd(xs, gain[None, :], gate, out_dtype=F32, name=tag + "_fwd", **kw)
        return y, args

    def bwd(args, dy):
        xs, gain = args[:nx], args[nx]
        gate = args[nx + 1] if has_gate else None
        dx, dg, dgate = _gnorm_bwd(dy, xs, gain[None, :], gate, name=tag + "_bwd", **kw)
        return (dx,) * nx + (dg[0],) + ((dgate,) if has_gate else ())

    op.defvjp(fwd, bwd)
    return op


def _ride(cls, arrays):
    return cls(list(arrays)) if len(arrays) else None


def _mm_ride(*args, rider, name, **kw):
    if rider is None:
        return _mm(*args, name=name, **kw), ()
    out = _mm(*args, rider=rider, name=name + "_ride", **kw)
    n = len(rider.arrays)
    main = out[:-n]
    return (main[0] if len(main) == 1 else tuple(main)), tuple(out[-n:])


def _landed_sums(landed, tag):
    return tuple(_sum_slots(l, "%s_sum%d" % (tag, i)) for i, l in enumerate(landed))


def _rms(x, g, name):
    return _gnorm_fwd([x], g[None, :], None, group=x.shape[1], center=False, scale=1.0, out_dtype=MXU, name=name)


def _rms_bwd(dxn, x, g, name):
    dx, dg, _ = _gnorm_bwd(dxn, [x], g[None, :], None, group=x.shape[1], center=False, scale=1.0, name=name)
    return dx, dg[0]


def _wire(shards):
    return [t.astype(WIRE) for t in shards]


@jax.custom_vjp
def norm_matmul(x, g, w, shards):
    return _norm_matmul_fwd(x, g, w, shards)[0]


def _norm_matmul_fwd(x, g, w, shards):
    xn = _rms(x, g, "rms_in")
    h, gathered = _mm_ride(xn, w, rider=_ride(_Gather, _wire(shards)), name="mm_in")
    return (h, gathered), (x, g, w, xn)


def _norm_matmul_bwd(saved, cts):
    x, g, w, xn = saved
    dh, d_gathered = cts
    dxn, l0 = _mm_ride(dh, w, nt=True, rider=_ride(_Exchange, d_gathered[:1]), name="mm_in_dx")
    dw, l1 = _mm_ride(xn, dh, ta=True, out_dtype=w.dtype, rider=_ride(_Exchange, d_gathered[1:]), name="mm_in_dw")
    dx, dg = _rms_bwd(dxn, x, g, "rms_in_bwd")
    return dx, dg, dw, _landed_sums(l0 + l1, "in")


norm_matmul.defvjp(_norm_matmul_fwd, _norm_matmul_bwd)


@jax.custom_vjp
def out_proj(y, w, x, shards):
    return _out_proj_fwd(y, w, x, shards)[0]


def _out_proj_fwd(y, w, x, shards):
    yb = y.astype(MXU)
    out, gathered = _mm_ride(yb, w, res=x, rider=_ride(_Gather, _wire(shards)), name="mm_out")
    return (out, gathered), (yb, w)


def _out_proj_bwd(saved, cts):
    yb, w = saved
    dout, d_gathered = cts
    db = dout.astype(MXU)
    dy, l0 = _mm_ride(db, w, nt=True, rider=_ride(_Exchange, d_gathered[:1]), name="mm_out_dy")
    dw, l1 = _mm_ride(yb, db, ta=True, out_dtype=w.dtype, rider=_ride(_Exchange, d_gathered[1:]), name="mm_out_dw")
    return dy, dw, dout, _landed_sums(l0 + l1, "out")


out_proj.defvjp(_out_proj_fwd, _out_proj_bwd)


@jax.custom_vjp
def mlp(x, g, w_up, s_down, shards):
    return _mlp_fwd(x, g, w_up, s_down, shards)[0]


def _mlp_fwd(x, g, w_up, s_down, shards):
    hm = _rms(x, g, "rms_mlp")
    (u, act), (g_down,) = _mm_ride(hm, w_up, emit_act=True, rider=_Gather(_wire([s_down])), name="mm_up")
    w_down = _row_full(g_down)
    out, gathered = _mm_ride(act, w_down, res=x, rider=_ride(_Gather, _wire(shards)), name="mm_down")
    return (out, gathered), (x, g, w_up, w_down, hm, u, act)


def _mlp_bwd(saved, cts):
    x, g, w_up, w_down, hm, u, act = saved
    dout, d_gathered = cts
    db = dout.astype(MXU)
    du, l0 = _mm_ride(db, w_down, nt=True, u_in=u, out_dtype=MXU, rider=_ride(_Exchange, d_gathered[:1]), name="mm_down_da")
    dw_down, l1 = _mm_ride(act, db, ta=True, out_dtype=w_down.dtype, rider=_ride(_Exchange, d_gathered[1:]), name="mm_down_dw")
    blocks = _row_blocks(dw_down)
    half = blocks.shape[1] // 2
    dhm, l_lo = _mm_ride(du, w_up, nt=True, rider=_Exchange([blocks[:, :half]]), name="mm_up_dx")
    dw_up, l_hi = _mm_ride(hm, du, ta=True, out_dtype=w_up.dtype, rider=_Exchange([blocks[:, half:]]), name="mm_up_dw")
    dx, dg = _rms_bwd(dhm, x, g, "rms_mlp_bwd")
    ds_down = jnp.concatenate(_landed_sums(l_lo + l_hi, "down"), axis=0)
    return dout + dx, dg, dw_up, ds_down, _landed_sums(l0 + l1, "mlp")


mlp.defvjp(_mlp_fwd, _mlp_bwd)


def _order(C, rev):
    i = np.arange(C)
    return (C - 1 - i) if rev else i


def _hg_constants(C, rev):
    p = _order(C, rev)
    pi, pj = p[:, None], p[None, :]
    tri = (pj <= pi).astype(np.float32)
    masks = [((pi // HG_SUB) == (pj // HG_SUB)) & (pj <= pi)]
    h = HG_SUB
    halves = []
    while h < C:
        masks.append(((pi // (2 * h)) == (pj // (2 * h))) & ((pi // h) % 2 == 1) & ((pj // h) % 2 == 0))
        halves.append(h)
        h *= 2
    return tri, np.stack(masks).astype(np.float32), halves


def _hg_tables(b_scr, C, h, rev):
    nb = C // h
    g_rows, e_rows = [], []
    for rb in range(nb):
        if rev:
            e = b_scr[pl.ds(rb * h, 1), :]
            g = b_scr[pl.ds((rb + 1) * h, 1), :] if rb < nb - 1 else None
        else:
            e = b_scr[pl.ds((rb + 1) * h - 1, 1), :]
            g = b_scr[pl.ds(rb * h - 1, 1), :] if rb >= 1 else None
        e_rows.append(jnp.broadcast_to(e, (h, HG_D)))
        g_rows.append(jnp.zeros((h, HG_D), F32) if g is None else jnp.broadcast_to(g, (h, HG_D)))
    return jnp.concatenate(g_rows, axis=0), jnp.concatenate(e_rows, axis=0)


def _hg_gates(z, lb):
    sg = jax.nn.sigmoid(z)
    f = lb + (1.0 - lb) * sg
    return sg, f, 1.0 - f, jnp.log(f)


def _hg_exponents(b, b_scr, C, halves, rev):
    g0, _ = _hg_tables(b_scr, C, HG_SUB, rev)
    p0 = b - g0
    out = [(p0, jnp.minimum(-p0, EXP_CLAMP))]
    for h in halves:
        g, e = _hg_tables(b_scr, C, h, rev)
        out.append((jnp.minimum(b - g, 0.0), jnp.minimum(e - b, 0.0)))
    return out


def _hg_scan_fwd(q, v, z, lb, rev, name):
    S = q.shape[0]
    C = SCAN_C
    nc = S // C
    tri, masks, halves = _hg_constants(C, rev)
    nlev = masks.shape[0]
    end_row = 0 if rev else C - 1

    def body(q_ref, v_ref, z_ref, lb_ref, tri_ref, mask_ref, o_ref, st_ref, s_scr, b_scr):
        @pl.when(pl.program_id(1) == 0)
        def _():
            s_scr[...] = jnp.zeros_like(s_scr)

        for hh in range(SCAN_HPS):
            sl = slice(hh * HG_D, (hh + 1) * HG_D)
            bh = b_scr.at[hh]
            _, f, k, lf = _hg_gates(z_ref[:, sl], lb_ref[:, sl])
            b = _dot01(tri_ref[...], lf)
            bh[...] = b
            qv, vv = q_ref[:, sl], v_ref[:, sl]
            st = s_scr[hh]
            st_ref[hh, 0] = st
            a = jnp.zeros((C, C), F32)
            for lv, (eq, ek) in enumerate(_hg_exponents(b, bh, C, halves, rev)):
                a = a + mask_ref[lv] * _dot_nt(qv * jnp.exp(eq), k * jnp.exp(ek))
            bend = bh[pl.ds(end_row, 1), :]
            o_ref[:, sl] = _dot(a, vv) + _dot_nt(qv * jnp.exp(b), st)
            s_scr[hh] = st * jnp.exp(bend) + _dot_tn(vv, k * jnp.exp(bend - b))

    cidx = (lambda c: nc - 1 - c) if rev else (lambda c: c)
    wid = SCAN_HPS * HG_D
    blk = pl.BlockSpec((C, wid), lambda h, c: (cidx(c), h))
    o, states = pl.pallas_call(
        body,
        out_shape=[jax.ShapeDtypeStruct((S, HGW), F32), jax.ShapeDtypeStruct((HG_HEADS, nc, HG_D, HG_D), F32)],
        grid=(HG_HEADS // SCAN_HPS, nc),
        in_specs=[blk, blk, blk, pl.BlockSpec((1, wid), lambda h, c: (0, h)),
                  pl.BlockSpec((C, C), lambda h, c: (0, 0)), pl.BlockSpec((nlev, C, C), lambda h, c: (0, 0, 0))],
        out_specs=[blk, pl.BlockSpec((SCAN_HPS, 1, HG_D, HG_D), lambda h, c: (h, cidx(c), 0, 0))],
        scratch_shapes=[pltpu.VMEM((SCAN_HPS, HG_D, HG_D), F32), pltpu.VMEM((SCAN_HPS, C, HG_D), F32)],
        name=name, compiler_params=_params("parallel", "arbitrary"),
    )(q, v, z, lb, jnp.asarray(tri, MXU), jnp.asarray(masks))
    return o, states


def _hg_scan_bwd(q, v, z, lb, states, do, rev, name):
    S = q.shape[0]
    C = SCAN_C
    nc = S // C
    tri, masks, halves = _hg_constants(C, rev)
    nlev = masks.shape[0]
    end_row = 0 if rev else C - 1
    masks_t = np.ascontiguousarray(np.transpose(masks, (0, 2, 1)))

    def body(q_ref, v_ref, z_ref, lb_ref, do_ref, st_ref, tri_ref, triT_ref, mask_ref, maskT_ref,
             dq_ref, dv_ref, dz_ref, dlb_ref, dn_scr, b_scr):
        @pl.when(pl.program_id(1) == 0)
        def _():
            dn_scr[...] = jnp.zeros_like(dn_scr)
            dlb_ref[...] = jnp.zeros_like(dlb_ref)

        for hh in range(SCAN_HPS):
            sl = slice(hh * HG_D, (hh + 1) * HG_D)
            bh = b_scr.at[hh]
            lb_v = lb_ref[:, sl]
            sg, f, k, lf = _hg_gates(z_ref[:, sl], lb_v)
            b = _dot01(tri_ref[...], lf)
            bh[...] = b
            qv, vv, dov = q_ref[:, sl], v_ref[:, sl], do_ref[:, sl]
            st, dn = st_ref[hh, 0], dn_scr[hh]
            da = _dot_nt(dov, vv)
            da_t = da.T
            a = jnp.zeros((C, C), F32)
            dq = jnp.zeros((C, HG_D), F32)
            dk = jnp.zeros((C, HG_D), F32)
            for lv, (eq, ek) in enumerate(_hg_exponents(b, bh, C, halves, rev)):
                xq, xk = jnp.exp(eq), jnp.exp(ek)
                qs, ks = qv * xq, k * xk
                a = a + mask_ref[lv] * _dot_nt(qs, ks)
                dq = dq + _dot(mask_ref[lv] * da, ks) * xq
                dk = dk + _dot(maskT_ref[lv] * da_t, qs) * xk
            bend = bh[pl.ds(end_row, 1), :]
            xb, xe, xend = jnp.exp(b), jnp.exp(bend - b), jnp.exp(bend)
            dq = dq + _dot(dov, st) * xb
            dk_state = _dot(vv, dn) * xe
            dk = dk + dk_state
            dv_ref[:, sl] = _dot(a.T, dov) + _dot_nt(k * xe, dn)
            dn_scr[hh] = dn * xend + _dot_tn(dov, qv * xb)
            extra = jnp.sum(k * dk_state, axis=0, keepdims=True) + xend * jnp.sum(st * dn, axis=0, keepdims=True)
            rows = lax.broadcasted_iota(jnp.int32, (C, HG_D), 0)
            db = qv * dq - k * dk + jnp.where(rows == end_row, extra, 0.0)
            df = _dot01(triT_ref[...], db) / f - dk
            dq_ref[:, sl] = dq
            dz_ref[:, sl] = df * (1.0 - lb_v) * sg * (1.0 - sg)
            dlb_ref[:, sl] += jnp.sum(df * (1.0 - sg), axis=0, keepdims=True)

    cidx = (lambda c: c) if rev else (lambda c: nc - 1 - c)
    wid = SCAN_HPS * HG_D
    blk = pl.BlockSpec((C, wid), lambda h, c: (cidx(c), h))
    vec = pl.BlockSpec((1, wid), lambda h, c: (0, h))
    cc = pl.BlockSpec((C, C), lambda h, c: (0, 0))
    lcc = pl.BlockSpec((nlev, C, C), lambda h, c: (0, 0, 0))
    sd = jax.ShapeDtypeStruct((S, HGW), F32)
    return pl.pallas_call(
        body, out_shape=[sd, sd, sd, jax.ShapeDtypeStruct((1, HGW), F32)], grid=(HG_HEADS // SCAN_HPS, nc),
        in_specs=[blk, blk, blk, vec, blk,
                  pl.BlockSpec((SCAN_HPS, 1, HG_D, HG_D), lambda h, c: (h, cidx(c), 0, 0)), cc, cc, lcc, lcc],
        out_specs=[blk, blk, blk, vec],
        scratch_shapes=[pltpu.VMEM((SCAN_HPS, HG_D, HG_D), F32), pltpu.VMEM((SCAN_HPS, C, HG_D), F32)],
        name=name, compiler_params=_params("parallel", "arbitrary"),
    )(q, v, z, lb, do, states, jnp.asarray(tri, MXU), jnp.asarray(tri.T, MXU),
      jnp.asarray(masks), jnp.asarray(masks_t))


def _make_hg_scan(rev):
    tag = "hg_rev" if rev else "hg_fwd"

    @jax.custom_vjp
    def op(q, v, z, lb):
        return fwd(q, v, z, lb)[0]

    def fwd(q, v, z, lb):
        o, states = _hg_scan_fwd(q, v, z, lb[None, :], rev, tag)
        return o, (q, v, z, lb, states)

    def bwd(saved, do):
        q, v, z, lb, states = saved
        dq, dv, dz, dlb = _hg_scan_bwd(q, v, z, lb[None, :], states, do, rev, tag + "_bwd")
        return dq, dv, dz, dlb[0]

    op.defvjp(fwd, bwd)
    return op


def _ret_constants(C, rev):
    hidx = np.arange(RET_HEADS, dtype=np.float64)
    lg = np.log1p(-np.exp2(-5.0 - hidx))
    if rev:
        lg = lg[::-1]
    p = _order(C, rev).astype(np.float64)
    rel = p[:, None] - p[None, :]
    dmat = np.where(rel >= 0, np.exp(lg[:, None, None] * np.maximum(rel, 0.0)), 0.0)
    xi = np.exp(lg[:, None] * (p[None, :] + 1.0))
    zeta = np.exp(lg[:, None] * (C - 1.0 - p[None, :]))
    gc = np.exp(lg * C)
    bc = lambda t: np.ascontiguousarray(np.broadcast_to(t[:, :, None], (RET_HEADS, C, RET_DK))).astype(np.float32)
    gcb = np.ascontiguousarray(np.broadcast_to(gc[:, None, None], (RET_HEADS, 1, RET_DK))).astype(np.float32)
    return dmat.astype(np.float32), bc(xi), bc(zeta), gcb


def _ret_scan_fwd(qh, kh, v, rev, name):
    S = v.shape[0]
    C = SCAN_C
    nc = S // C
    dmat, xi, zeta, gc = _ret_constants(C, rev)

    def body(q_ref, k_ref, v_ref, d_ref, xi_ref, zeta_ref, gc_ref, o_ref, st_ref, s_scr):
        @pl.when(pl.program_id(1) == 0)
        def _():
            s_scr[...] = jnp.zeros_like(s_scr)

        for hh in range(SCAN_HPS):
            sl = slice(hh * RET_DV, (hh + 1) * RET_DV)
            qv, kv = q_ref[hh], k_ref[hh]
            st = s_scr[hh]
            st_ref[hh, 0] = st
            sc = _dot_nt(qv, kv) * d_ref[hh]
            o_ref[:, sl] = _dot(sc, v_ref[:, sl]) + _dot_nt(qv * xi_ref[hh], st)
            s_scr[hh] = st * gc_ref[hh] + _dot_tn(v_ref[:, sl], kv * zeta_ref[hh])

    cidx = (lambda c: nc - 1 - c) if rev else (lambda c: c)
    hk = pl.BlockSpec((SCAN_HPS, C, RET_DK), lambda h, c: (h, cidx(c), 0))
    vblk = pl.BlockSpec((C, SCAN_HPS * RET_DV), lambda h, c: (cidx(c), h))
    tab = pl.BlockSpec((SCAN_HPS, C, RET_DK), lambda h, c: (h, 0, 0))
    return pl.pallas_call(
        body,
        out_shape=[jax.ShapeDtypeStruct((S, RETW), F32), jax.ShapeDtypeStruct((RET_HEADS, nc, RET_DV, RET_DK), F32)],
        grid=(RET_HEADS // SCAN_HPS, nc),
        in_specs=[hk, hk, vblk, pl.BlockSpec((SCAN_HPS, C, C), lambda h, c: (h, 0, 0)), tab, tab,
                  pl.BlockSpec((SCAN_HPS, 1, RET_DK), lambda h, c: (h, 0, 0))],
        out_specs=[vblk, pl.BlockSpec((SCAN_HPS, 1, RET_DV, RET_DK), lambda h, c: (h, cidx(c), 0, 0))],
        scratch_shapes=[pltpu.VMEM((SCAN_HPS, RET_DV, RET_DK), F32)],
        name=name, compiler_params=_params("parallel", "arbitrary"),
    )(qh, kh, v, jnp.asarray(dmat), jnp.asarray(xi), jnp.asarray(zeta), jnp.asarray(gc))


def _ret_scan_bwd(qh, kh, v, states, do, rev, name):
    S = v.shape[0]
    C = SCAN_C
    nc = S // C
    dmat, xi, zeta, gc = _ret_constants(C, rev)

    def body(q_ref, k_ref, v_ref, do_ref, st_ref, d_ref, xi_ref, zeta_ref, gc_ref,
             dq_ref, dk_ref, dv_ref, dn_scr):
        @pl.when(pl.program_id(1) == 0)
        def _():
            dn_scr[...] = jnp.zeros_like(dn_scr)

        for hh in range(SCAN_HPS):
            sl = slice(hh * RET_DV, (hh + 1) * RET_DV)
            qv, kv, vv, dov = q_ref[hh], k_ref[hh], v_ref[:, sl], do_ref[:, sl]
            st, dn = st_ref[hh, 0], dn_scr[hh]
            dm = d_ref[hh]
            sc = _dot_nt(qv, kv) * dm
            dsc = _dot_nt(dov, vv) * dm
            kz = kv * zeta_ref[hh]
            dq_ref[hh] = _dot(dsc, kv) + _dot(dov, st) * xi_ref[hh]
            dk_ref[hh] = _dot(dsc.T, qv) + _dot(vv, dn) * zeta_ref[hh]
            dv_ref[:, sl] = _dot(sc.T, dov) + _dot_nt(kz, dn)
            dn_scr[hh] = dn * gc_ref[hh] + _dot_tn(dov, qv * xi_ref[hh])

    cidx = (lambda c: c) if rev else (lambda c: nc - 1 - c)
    hk = pl.BlockSpec((SCAN_HPS, C, RET_DK), lambda h, c: (h, cidx(c), 0))
    vblk = pl.BlockSpec((C, SCAN_HPS * RET_DV), lambda h, c: (cidx(c), h))
    tab = pl.BlockSpec((SCAN_HPS, C, RET_DK), lambda h, c: (h, 0, 0))
    hs = jax.ShapeDtypeStruct(qh.shape, F32)
    return pl.pallas_call(
        body, out_shape=[hs, hs, jax.ShapeDtypeStruct((S, RETW), F32)], grid=(RET_HEADS // SCAN_HPS, nc),
        in_specs=[hk, hk, vblk, vblk,
                  pl.BlockSpec((SCAN_HPS, 1, RET_DV, RET_DK), lambda h, c: (h, cidx(c), 0, 0)),
                  pl.BlockSpec((SCAN_HPS, C, C), lambda h, c: (h, 0, 0)), tab, tab,
                  pl.BlockSpec((SCAN_HPS, 1, RET_DK), lambda h, c: (h, 0, 0))],
        out_specs=[hk, hk, vblk],
        scratch_shapes=[pltpu.VMEM((SCAN_HPS, RET_DV, RET_DK), F32)],
        name=name, compiler_params=_params("parallel", "arbitrary"),
    )(qh, kh, v, do, states, jnp.asarray(dmat), jnp.asarray(xi), jnp.asarray(zeta), jnp.asarray(gc))


def _make_ret_scan(rev):
    tag = "ret_rev" if rev else "ret_fwd"

    @jax.custom_vjp
    def op(qh, kh, v):
        return fwd(qh, kh, v)[0]

    def fwd(qh, kh, v):
        o, states = _ret_scan_fwd(qh, kh, v, rev, tag)
        return o, (qh, kh, v, states)

    def bwd(saved, do):
        qh, kh, v, states = saved
        return tuple(_ret_scan_bwd(qh, kh, v, states, do, rev, tag + "_bwd"))

    op.defvjp(fwd, bwd)
    return op


def _rope_tables(S, scale):
    half = RET_DK // 2
    inv = ROPE_BASE ** (-np.arange(half, dtype=np.float32) / half)
    ang = np.arange(S, dtype=np.float32)[:, None] * inv[None, :]
    cos, sin = np.cos(ang), np.sin(ang)
    cos_t = np.tile(np.concatenate([cos, cos], axis=1), (1, RET_HEADS)) * scale
    sin_t = np.tile(np.concatenate([-sin, sin], axis=1), (1, RET_HEADS)) * scale
    return cos_t.astype(np.float32), sin_t.astype(np.float32)


def _rope_apply(t, cos_t, sin_t, name):
    S, W = t.shape
    ts = _pick(S, (512, 256, 128))
    half = RET_DK // 2

    def body(t_ref, c_ref, s_ref, o_ref):
        tv = t_ref[...]
        lane = lax.broadcasted_iota(jnp.int32, tv.shape, 1)
        partner = jnp.where(lane % RET_DK < half, pltpu.roll(tv, W - half, 1), pltpu.roll(tv, half, 1))
        o_ref[...] = tv * c_ref[...] + partner * s_ref[...]

    row = pl.BlockSpec((ts, W), lambda i: (i, 0))
    return pl.pallas_call(body, out_shape=jax.ShapeDtypeStruct((S, W), F32), grid=(S // ts,),
                          in_specs=[row, row, row], out_specs=row, name=name,
                          compiler_params=_params("parallel"))(t, jnp.asarray(cos_t), jnp.asarray(sin_t))


def _make_rope(scale, tag):
    def apply(t):
        cos_t, sin_t = _rope_tables(t.shape[0], scale)
        return _rope_apply(t, cos_t, sin_t, tag)

    op = jax.custom_vjp(apply)

    def fwd(t):
        return apply(t), None

    def bwd(_, dout):
        cos_t, sin_t = _rope_tables(dout.shape[0], scale)
        return (_rope_apply(dout, cos_t, -sin_t, tag + "_bwd"),)

    op.defvjp(fwd, bwd)
    return op


def _att_geometry(L):
    tq = _pick(L, (512, 256, 128))
    return tq, L // tq, tq // ATT_HALO


def _att_specs(tq, per):
    main = pl.BlockSpec((1, tq, DIL_HD), lambda b, n: (b, n, 0))
    prev = pl.BlockSpec((1, ATT_HALO, DIL_HD), lambda b, n: (b, jnp.maximum(n * per - 1, 0), 0))
    return main, prev


def _att_valid(n, u, tq, L):
    ii = lax.broadcasted_iota(jnp.int32, (ATT_SB, ATT_WIN), 0)
    jj = lax.broadcasted_iota(jnp.int32, (ATT_SB, ATT_WIN), 1)
    key = n * tq + u * ATT_SB - ATT_HALO + jj
    return (jnp.abs(jj - ATT_HALO - ii) <= ATT_HALO) & (key >= 0) & (key < L)


def _att_fill(buf, prev_ref, main_ref, next_ref, tq):
    buf[pl.ds(0, ATT_HALO), :] = prev_ref[0]
    buf[pl.ds(ATT_HALO, tq), :] = main_ref[0]
    buf[pl.ds(ATT_HALO + tq, ATT_HALO), :] = next_ref[0]


def _att_fwd(q, k, v, bias, dil, name):
    B, L, _ = q.shape
    tq, nt, per = _att_geometry(L)
    last = L // ATT_HALO - 1

    def body(q_ref, kp_ref, k_ref, kn_ref, vp_ref, v_ref, vn_ref, bias_ref, o_ref, lse_ref, kbuf, vbuf):
        n = pl.program_id(1)
        _att_fill(kbuf, kp_ref, k_ref, kn_ref, tq)
        _att_fill(vbuf, vp_ref, v_ref, vn_ref, tq)
        for u in range(tq // ATT_SB):
            rows = pl.ds(u * ATT_SB, ATT_SB)
            win = pl.ds(u * ATT_SB, ATT_WIN)
            s = _dot_nt(q_ref[0, rows, :], kbuf[win, :]) + bias_ref[0]
            s = jnp.where(_att_valid(n, u, tq, L), s, -1e30)
            m = jnp.max(s, axis=-1, keepdims=True)
            p = jnp.exp(s - m)
            den = jnp.sum(p, axis=-1, keepdims=True)
            o_ref[0, rows, :] = _dot(p, vbuf[win, :]) / den
            lse_ref[0, rows, :] = jnp.broadcast_to(m + jnp.log(den), (ATT_SB, DIL_HD))

    main, prev = _att_specs(tq, per)
    nxt = pl.BlockSpec((1, ATT_HALO, DIL_HD), lambda b, n: (b, jnp.minimum((n + 1) * per, last), 0))
    sd = jax.ShapeDtypeStruct((B, L, DIL_HD), F32)
    return pl.pallas_call(
        body, out_shape=[sd, sd], grid=(B, nt),
        in_specs=[main, prev, main, nxt, prev, main, nxt,
                  pl.BlockSpec((1, ATT_SB, ATT_WIN), lambda b, n: (b // dil, 0, 0))],
        out_specs=[main, main],
        scratch_shapes=[pltpu.VMEM((tq + 2 * ATT_HALO, DIL_HD), q.dtype), pltpu.VMEM((tq + 2 * ATT_HALO, DIL_HD), q.dtype)],
        name=name, compiler_params=_params("parallel", "arbitrary"),
    )(q, k, k, k, v, v, v, bias)


def _att_bwd(q, k, v, bias, o, lse, do, dlse, dil, name):
    B, L, _ = q.shape
    tq, nt, per = _att_geometry(L)
    last = L // ATT_HALO - 1

    def body(q_ref, kp_ref, k_ref, kn_ref, vp_ref, v_ref, vn_ref, bias_ref, o_ref, lse_ref, do_ref, dlse_ref,
             dq_ref, dk_ref, dkp_ref, dkn_ref, dv_ref, dvp_ref, dvn_ref, dbias_ref, kbuf, vbuf, dkbuf, dvbuf):
        b, n = pl.program_id(0), pl.program_id(1)

        @pl.when((b % dil == 0) & (n == 0))
        def _():
            dbias_ref[...] = jnp.zeros_like(dbias_ref)

        _att_fill(kbuf, kp_ref, k_ref, kn_ref, tq)
        _att_fill(vbuf, vp_ref, v_ref, vn_ref, tq)
        dkbuf[...] = jnp.zeros_like(dkbuf)
        dvbuf[...] = jnp.zeros_like(dvbuf)
        for u in range(tq // ATT_SB):
            rows = pl.ds(u * ATT_SB, ATT_SB)
            win = pl.ds(u * ATT_SB, ATT_WIN)
            qu, kw, vw = q_ref[0, rows, :], kbuf[win, :], vbuf[win, :]
            dou = do_ref[0, rows, :]
            s = _dot_nt(qu, kw) + bias_ref[0]
            lse_u = jnp.max(lse_ref[0, rows, :], axis=-1, keepdims=True)
            p = jnp.where(_att_valid(n, u, tq, L), jnp.exp(s - lse_u), 0.0)
            corr = jnp.sum(dlse_ref[0, rows, :] - dou * o_ref[0, rows, :], axis=-1, keepdims=True)
            ds = p * (_dot_nt(dou, vw) + corr)
            dq_ref[0, rows, :] = _dot(ds, kw)
            dkbuf[win, :] += _dot(ds.T, qu)
            dvbuf[win, :] += _dot(p.T, dou)
            dbias_ref[0] += ds
        for full, lo, hi in ((dkbuf, dkp_ref, dkn_ref), (dvbuf, dvp_ref, dvn_ref)):
            lo[0, 0] = full[pl.ds(0, ATT_HALO), :]
            hi[0, 0] = full[pl.ds(ATT_HALO + tq, ATT_HALO), :]
        dk_ref[0] = dkbuf[pl.ds(ATT_HALO, tq), :]
        dv_ref[0] = dvbuf[pl.ds(ATT_HALO, tq), :]

    main, prev = _att_specs(tq, per)
    nxt = pl.BlockSpec((1, ATT_HALO, DIL_HD), lambda b, n: (b, jnp.minimum((n + 1) * per, last), 0))
    halo = pl.BlockSpec((1, 1, ATT_HALO, DIL_HD), lambda b, n: (b, n, 0, 0))
    bias_spec = pl.BlockSpec((1, ATT_SB, ATT_WIN), lambda b, n: (b // dil, 0, 0))
    sd = jax.ShapeDtypeStruct((B, L, DIL_HD), F32)
    hd = jax.ShapeDtypeStruct((B, nt, ATT_HALO, DIL_HD), F32)
    width = tq + 2 * ATT_HALO
    dq, dk, dkp, dkn, dv, dvp, dvn, dbias = pl.pallas_call(
        body, out_shape=[sd, sd, hd, hd, sd, hd, hd, jax.ShapeDtypeStruct(bias.shape, F32)], grid=(B, nt),
        in_specs=[main, prev, main, nxt, prev, main, nxt, bias_spec, main, main, main, main],
        out_specs=[main, main, halo, halo, main, halo, halo, bias_spec],
        scratch_shapes=[pltpu.VMEM((width, DIL_HD), q.dtype), pltpu.VMEM((width, DIL_HD), q.dtype),
                        pltpu.VMEM((width, DIL_HD), F32), pltpu.VMEM((width, DIL_HD), F32)],
        name=name, compiler_params=_params("arbitrary", "arbitrary"),
    )(q, k, k, k, v, v, v, bias, o, lse, do, dlse)

    def fold(mainv, lo, hi):
        t = mainv.reshape(B, nt, tq, DIL_HD)
        if nt > 1:
            t = t.at[:, :-1, tq - ATT_HALO:, :].add(lo[:, 1:])
            t = t.at[:, 1:, :ATT_HALO, :].add(hi[:, :-1])
        return t.reshape(B, L, DIL_HD)

    return dq, fold(dk, dkp, dkn), fold(dv, dvp, dvn), dbias


def _make_attention(dil):
    tag = "att_d%d" % dil

    @jax.custom_vjp
    def op(q, k, v, bias):
        return fwd(q, k, v, bias)[0]

    def fwd(q, k, v, bias):
        qb, kb, vb = q.astype(MXU), k.astype(MXU), v.astype(MXU)
        o, lse = _att_fwd(qb, kb, vb, bias, dil, tag)
        return (o, lse), (qb, kb, vb, bias, o, lse)

    def bwd(saved, cts):
        qb, kb, vb, bias, o, lse = saved
        do, dlse = cts
        return tuple(_att_bwd(qb, kb, vb, bias, o, lse, do, dlse, dil, tag + "_bwd"))

    op.defvjp(fwd, bwd)
    return op


def _merge_weights(l0, l1, l2):
    m = jnp.maximum(jnp.maximum(l0, l1), l2)
    e0, e1, e2 = jnp.exp(l0 - m), jnp.exp(l1 - m), jnp.exp(l2 - m)
    inv = 1.0 / (e0 + e1 + e2)
    return e0 * inv, e1 * inv, e2 * inv


def _merge_call(body, n_in, n_out, shape, name):
    R, W = shape
    ts = _pick(R, (1024, 512, 256, 128))
    row = pl.BlockSpec((ts, W), lambda i: (i, 0))
    sd = jax.ShapeDtypeStruct(shape, F32)
    return pl.pallas_call(body, out_shape=[sd] * n_out, grid=(R // ts,), in_specs=[row] * n_in,
                          out_specs=[row] * n_out, name=name, compiler_params=_params("parallel"))


@jax.custom_vjp
def dil_merge(o0, o1, o2, l0, l1, l2):
    return _dil_merge_fwd(o0, o1, o2, l0, l1, l2)[0]


def _dil_merge_fwd(*args):
    def body(o0, o1, o2, l0, l1, l2, out):
        w0, w1, w2 = _merge_weights(l0[...], l1[...], l2[...])
        out[...] = w0 * o0[...] + w1 * o1[...] + w2 * o2[...]

    return _merge_call(body, 6, 1, args[0].shape, "dil_merge")(*args)[0], args


def _dil_merge_bwd(args, dout):
    def body(o0, o1, o2, l0, l1, l2, d, do0, do1, do2, dl0, dl1, dl2):
        ws = _merge_weights(l0[...], l1[...], l2[...])
        dv = d[...]
        dws = [dv * o[...] for o in (o0, o1, o2)]
        mean = ws[0] * dws[0] + ws[1] * dws[1] + ws[2] * dws[2]
        for w, dw, do_ref, dl_ref in zip(ws, dws, (do0, do1, do2), (dl0, dl1, dl2)):
            do_ref[...] = w * dv
            dl_ref[...] = w * (dw - mean)

    return tuple(_merge_call(body, 7, 6, args[0].shape, "dil_merge_bwd")(*args, dout))


dil_merge.defvjp(_dil_merge_fwd, _dil_merge_bwd)


def _t5_bucket(rel):
    nb = REL_BUCKETS // 2
    max_exact = nb // 2
    sign_off = np.where(rel > 0, nb, 0)
    n = np.abs(rel)
    nf = np.maximum(n, 1).astype(np.float32)
    large = max_exact + (np.log(nf / np.float32(max_exact)) / np.float32(math.log(REL_MAX_DIST / max_exact))
                         * np.float32(nb - max_exact)).astype(np.int32)
    large = np.minimum(large, nb - 1)
    return sign_off + np.where(n < max_exact, n, large)


def _loss_grad(xf, target):
    S, D = xf.shape
    ts = _pick(S, (512, 256, 128))

    def body(x_ref, t_ref, dy_ref, part_ref):
        @pl.when(pl.program_id(0) == 0)
        def _():
            part_ref[...] = jnp.zeros_like(part_ref)

        err = x_ref[...] - t_ref[...]
        dy_ref[...] = err * (1.0 / D)
        part_ref[...] += jnp.sum(err * err, axis=0, keepdims=True)

    row = pl.BlockSpec((ts, D), lambda i: (i, 0))
    return pl.pallas_call(body, out_shape=[jax.ShapeDtypeStruct((S, D), F32), jax.ShapeDtypeStruct((1, D), F32)],
                          grid=(S // ts,), in_specs=[row, row], out_specs=[row, pl.BlockSpec((1, D), lambda i: (0, 0))],
                          name="loss_grad", compiler_params=_params("arbitrary"))(xf, target)


def _adamw_math(g, w, m, v):
    m = ADAM_B1 * m + (1.0 - ADAM_B1) * g
    v = ADAM_B2 * v + (1.0 - ADAM_B2) * (g * g)
    m_hat = m / (1.0 - ADAM_B1 ** ADAM_STEP)
    v_hat = v / (1.0 - ADAM_B2 ** ADAM_STEP)
    delta = -ADAM_LR * (m_hat / (jnp.sqrt(v_hat) + ADAM_EPS) + ADAM_WD * w)
    return delta, m, v


def _adamw(g, w, m, v, name):
    Lw, R, C = w.shape
    tr = _pick(R, (256, 128, 64, 32, 16, 8))

    def body(g_ref, w_ref, m_ref, v_ref, d_ref, nm_ref, nv_ref):
        d_ref[0], nm_ref[0], nv_ref[0] = _adamw_math(g_ref[0], w_ref[0], m_ref[0], v_ref[0])

    blk = pl.BlockSpec((1, tr, C), lambda l, i: (l, i, 0))
    sd = jax.ShapeDtypeStruct(w.shape, F32)
    return pl.pallas_call(body, out_shape=[sd] * 3, grid=(Lw, R // tr), in_specs=[blk] * 4, out_specs=[blk] * 3,
                          name=name, compiler_params=_params("parallel", "parallel"))(g, w, m, v)


def _mesh_pos():
    return lax.axis_index("x"), lax.axis_index("y"), lax.axis_index("c")


def _slot(x, y, c):
    return 4 * x + 2 * y + c


class _Comm:
    def __init__(self, arrays):
        self.arrays = list(arrays)

    def sem_shapes(self):
        na = len(self.arrays)
        return [pltpu.SemaphoreType.DMA((7 * na,)), pltpu.SemaphoreType.DMA((7 * na,)), pltpu.SemaphoreType.DMA((na,))]

    def mid(self, ins, outs, sems):
        pass

    def call(self, name):
        na = len(self.arrays)

        def body(*refs):
            ins, outs, sems = refs[:na], refs[na:2 * na], refs[2 * na:]
            self.start(ins, outs, sems)
            self.mid(ins, outs, sems)
            self.finish(ins, outs, sems)

        anyspec = pl.BlockSpec(memory_space=pl.ANY)
        return pl.pallas_call(body, out_shape=self.out_shapes(), in_specs=[anyspec] * na, out_specs=[anyspec] * na,
                              scratch_shapes=self.sem_shapes(), name=name)(*self.arrays)


class _Gather(_Comm):
    def out_shapes(self):
        return [jax.ShapeDtypeStruct((N_DEV,) + b.shape, b.dtype) for b in self.arrays]

    def _copies(self, ins, outs, sems):
        send_sems, recv_sems, local_sems = sems
        x, y, c = _mesh_pos()
        me, sibling = (x, y, c), (x, y, 1 - c)
        chips = [(1 - x, y), (x, 1 - y), (1 - x, 1 - y)]
        per = []
        for a in range(len(self.arrays)):
            def copy(k, block, to, src=None, a=a):
                dst = outs[a].at[_slot(*block)]
                return pltpu.make_async_remote_copy(
                    src_ref=dst if src is None else src, dst_ref=dst,
                    send_sem=send_sems.at[7 * a + k], recv_sem=recv_sems.at[7 * a + k],
                    device_id=to, device_id_type=MESH_ID)

            per.append(dict(
                mine=pltpu.make_async_copy(ins[a], outs[a].at[_slot(*me)], local_sems.at[a]),
                first=[copy(0, me, sibling, src=ins[a])] + [copy(1 + j, me, (*ch, c), src=ins[a]) for j, ch in enumerate(chips)],
                passed=[copy(4 + j, (*ch, c), sibling) for j, ch in enumerate(chips)],
                over_ici=[copy(1 + j, (*ch, c), me) for j, ch in enumerate(chips)],
                from_sibling=[copy(0, sibling, me)] + [copy(4 + j, (*ch, 1 - c), me) for j, ch in enumerate(chips)]))
        return per

    def start(self, ins, outs, sems):
        for p in self._copies(ins, outs, sems):
            p["mine"].start()
            for cp in p["first"]:
                cp.start()

    def mid(self, ins, outs, sems):
        for p in self._copies(ins, outs, sems):
            for arrived, onward in zip(p["over_ici"], p["passed"]):
                arrived.wait_recv()
                onward.start()

    def finish(self, ins, outs, sems):
        for p in self._copies(ins, outs, sems):
            for cp in p["from_sibling"]:
                cp.wait_recv()
            for cp in p["first"] + p["passed"]:
                cp.wait_send()
            p["mine"].wait()


class _Exchange(_Comm):
    def out_shapes(self):
        return [jax.ShapeDtypeStruct(f.shape, f.dtype) for f in self.arrays]

    def _copies(self, ins, outs, sems):
        send_sems, recv_sems, local_sems = sems
        x, y, c = _mesh_pos()
        my_slot = _slot(x, y, c)
        local, remote = [], []
        for a in range(len(self.arrays)):
            local.append(pltpu.make_async_copy(ins[a].at[my_slot], outs[a].at[my_slot], local_sems.at[a]))
            for k in range(1, N_DEV):
                px = 1 - x if k & 4 else x
                py = 1 - y if k & 2 else y
                pc = 1 - c if k & 1 else c
                remote.append(pltpu.make_async_remote_copy(
                    src_ref=ins[a].at[_slot(px, py, pc)], dst_ref=outs[a].at[my_slot],
                    send_sem=send_sems.at[7 * a + k - 1], recv_sem=recv_sems.at[7 * a + k - 1],
                    device_id=(px, py, pc), device_id_type=MESH_ID))
        return local, remote

    def start(self, ins, outs, sems):
        local, remote = self._copies(ins, outs, sems)
        for cp in local + remote:
            cp.start()

    def finish(self, ins, outs, sems):
        local, remote = self._copies(ins, outs, sems)
        for cp in remote + local:
            cp.wait()


def _sum_slots(parts, name):
    _, R, C = parts.shape
    tr = _pick(R, (256, 128, 64, 32, 16, 8))

    def body(p_ref, o_ref):
        g = p_ref[0].astype(F32)
        for s in range(1, N_DEV):
            g = g + p_ref[s].astype(F32)
        o_ref[...] = g

    return pl.pallas_call(body, out_shape=jax.ShapeDtypeStruct((R, C), F32), grid=(R // tr,),
                          in_specs=[pl.BlockSpec((N_DEV, tr, C), lambda i: (0, i, 0))],
                          out_specs=pl.BlockSpec((tr, C), lambda i: (i, 0)), name=name,
                          compiler_params=_params("parallel"))(parts)


_hg_fwd_op, _hg_rev_op = _make_hg_scan(False), _make_hg_scan(True)
_ret_fwd_op, _ret_rev_op = _make_ret_scan(False), _make_ret_scan(True)
_rope_q, _rope_k = _make_rope(1.0, "rope_q"), _make_rope(RET_DK ** -0.5, "rope_k")
_hg_post = _make_gnorm(2, True, HG_D, False, 1.0, "hg_post")
_ret_post = _make_gnorm(2, True, RET_DV, True, 1.0, "ret_post")
_qk_norms = _make_head_norms((DIL_HD ** -0.5, 1.0) * len(DIL_GROUPS), "dil_qk_norms")
_att_ops = {dil: _make_attention(dil) for _, dil in DIL_GROUPS}


def _to_heads(t, d):
    S, W = t.shape
    return t.reshape(S, W // d, d).transpose(1, 0, 2)


def _dilated_mixer(parts, rel_bias, q_gain, k_gain):
    S = parts[0].shape[0]
    qg, kg = jnp.tile(q_gain, DIL_SLOTS), jnp.tile(k_gain, DIL_SLOTS)
    ii = np.arange(ATT_SB)[:, None]
    jj = np.arange(ATT_WIN)[None, :]
    n_groups = len(DIL_GROUPS)
    normed = _qk_norms(*[parts[3 * g + t] for g in range(n_groups) for t in (0, 1)], *([qg, kg] * n_groups))
    outs, lses = [], []
    for g, (window, dil) in enumerate(DIL_GROUPS):
        assert window // (2 * dil) == ATT_HALO
        L = S // dil

        def to_res(t):
            return t.reshape(L, dil, DIL_SLOTS, DIL_HD).transpose(2, 1, 0, 3).reshape(DIL_SLOTS * dil, L, DIL_HD)

        def from_res(t):
            return t.reshape(DIL_SLOTS, dil, L, DIL_HD).transpose(0, 2, 1, 3).reshape(DIL_SLOTS * S, DIL_HD)

        onehot = (_t5_bucket((jj - ATT_HALO - ii) * dil)[:, :, None] == np.arange(REL_BUCKETS)).astype(np.float32)
        bias = jnp.einsum("ijb,bh->hij", onehot, rel_bias[:, g * DIL_SLOTS:(g + 1) * DIL_SLOTS],
                          precision=lax.Precision.HIGHEST)
        o, lse = _att_ops[dil](to_res(normed[2 * g]), to_res(normed[2 * g + 1]), to_res(parts[3 * g + 2]), bias)
        outs.append(from_res(o))
        lses.append(from_res(lse))
    merged = dil_merge(*outs, *lses)
    return merged.reshape(DIL_SLOTS, S, DIL_HD).transpose(1, 0, 2).reshape(S, DILW)


def _mixers(h, p):
    offs = np.cumsum(IN_SPLITS)[:-1].tolist()
    parts = jnp.split(h, offs, axis=-1)
    q, v = parts[0], parts[1]
    y_a = _hg_post(_hg_fwd_op(q, v, parts[2], p["lb_fwd"]), _hg_rev_op(q, v, parts[3], p["lb_bwd"]), p["hg_norm"], parts[4])
    qh = _to_heads(_rope_q(parts[5]), RET_DK)
    kh = _to_heads(_rope_k(parts[6]), RET_DK)
    y_b = _ret_post(_ret_fwd_op(qh, kh, parts[7]), _ret_rev_op(qh, kh, parts[7]), p["ret_norm"], parts[8])
    y_c = _dilated_mixer(parts[9:], p["rel_bias"], p["q_norm"], p["k_norm"])
    return jnp.concatenate([y_a, y_b, y_c], axis=-1)


def _col_full(blocks):
    return blocks.transpose(1, 0, 2).reshape(blocks.shape[1], -1)


def _col_blocks(full):
    K = full.shape[0]
    return full.reshape(K, N_DEV, -1).transpose(1, 0, 2)


def _row_full(blocks):
    return blocks.reshape(-1, blocks.shape[2])


def _row_blocks(full):
    return full.reshape(N_DEV, -1, full.shape[1])


SMALL_NAMES = ("norm_mix", "norm_mlp", "hg_lb_fwd", "hg_lb_bwd", "hg_norm", "ret_norm", "q_norm", "k_norm", "rel_bias")


def _forward(x, w_in, shards, small):
    depth = len(shards)
    lb_f = jnp.cumsum(jax.nn.softmax(small["hg_lb_fwd"], axis=0), axis=0)
    lb_b = jnp.cumsum(jax.nn.softmax(small["hg_lb_bwd"], axis=0), axis=0)
    for l in range(depth):
        p = {k: small[k][l] for k in ("norm_mix", "norm_mlp", "hg_norm", "ret_norm", "q_norm", "k_norm")}
        p["lb_fwd"], p["lb_bwd"] = lb_f[l] - lb_f[0], lb_b[l] - lb_b[0]
        p["rel_bias"] = small["rel_bias"]
        sh = shards[l]
        h, (g_up, g_out) = norm_matmul(x, p["norm_mix"], w_in, (sh["up"], sh["out"]))
        x, _ = out_proj(_mixers(h, p), _row_full(g_out), x, ())
        halves = ()
        if l < depth - 1:
            nxt = shards[l + 1]["in"]
            halves = (nxt[:nxt.shape[0] // 2], nxt[nxt.shape[0] // 2:])
        x, g_in = mlp(x, p["norm_mlp"], _col_full(g_up), sh["down"], halves)
        if halves:
            w_in = jnp.concatenate([_col_full(t) for t in g_in], axis=0)
    return x


def kernel(x, w_in, w_out, w_up, w_down, norm_mix, norm_mlp, hg_lb_fwd, hg_lb_bwd, hg_norm, ret_norm, q_norm, k_norm, rel_bias, loss_target, m_w_in, m_w_out, m_w_up, m_w_down, m_norm_mix, m_norm_mlp, m_hg_lb_fwd, m_hg_lb_bwd, m_hg_norm, m_ret_norm, m_q_norm, m_k_norm, m_rel_bias, v_w_in, v_w_out, v_w_up, v_w_down, v_norm_mix, v_norm_mlp, v_hg_lb_fwd, v_hg_lb_bwd, v_hg_norm, v_ret_norm, v_q_norm, v_k_norm, v_rel_bias):
    depth = w_in.shape[0]
    big = (w_in, w_out, w_up, w_down)
    big_m = (m_w_in, m_w_out, m_w_up, m_w_down)
    big_v = (v_w_in, v_w_out, v_w_up, v_w_down)
    small = dict(zip(SMALL_NAMES, (norm_mix, norm_mlp, hg_lb_fwd, hg_lb_bwd, hg_norm, ret_norm, q_norm, k_norm, rel_bias)))
    small_m = (m_norm_mix, m_norm_mlp, m_hg_lb_fwd, m_hg_lb_bwd, m_hg_norm, m_ret_norm, m_q_norm, m_k_norm, m_rel_bias)
    small_v = (v_norm_mix, v_norm_mlp, v_hg_lb_fwd, v_hg_lb_bwd, v_hg_norm, v_ret_norm, v_q_norm, v_k_norm, v_rel_bias)

    (gathered0,) = _Gather([w_in[0].astype(WIRE)]).call("gather_w_in0")
    names = ("in", "out", "up", "down")
    shards = [{n: w[l] for n, w in zip(names, big) if l or n != "in"} for l in range(depth)]

    xf, vjp = jax.vjp(_forward, x[0], _col_full(gathered0), shards, small)
    dy, part = _loss_grad(xf, loss_target[0])
    loss = lax.psum(0.5 / xf.shape[1] * jnp.sum(part), ("x", "y", "c"))
    dx, dw_in0, dshards, dsmall = vjp(dy)

    (landed0,) = _Exchange([_col_blocks(dw_in0)]).call("exchange_w_in0")
    dshards[0]["in"] = _sum_slots(landed0, "sum_w_in0")
    big_out = []
    for i, n in enumerate(names):
        g = jnp.stack([dshards[l][n] for l in range(depth)])
        big_out.append((g,) + tuple(_adamw(g, big[i], big_m[i], big_v[i], "adamw_" + n)))

    flat = jnp.concatenate([dsmall[n].reshape(-1) for n in SMALL_NAMES])
    n_small = flat.shape[0]
    rows = -(-n_small // 1024) * 8
    pad = lambda t: jnp.pad(t, (0, rows * 128 - n_small)).reshape(1, rows, 128)
    (small_parts,) = _Gather([pad(flat)[0]]).call("gather_small_grads")
    cat = lambda ts: pad(jnp.concatenate([t.reshape(-1) for t in ts]))
    g_small = _sum_slots(small_parts, "sum_small_grads")[None]
    small_out = (g_small,) + tuple(_adamw(g_small, cat([small[n] for n in SMALL_NAMES]), cat(small_m), cat(small_v), "adamw_small"))

    def unpack(t):
        t = t.reshape(-1)
        out, off = [], 0
        for n in SMALL_NAMES:
            size = small[n].size
            out.append(t[off:off + size].reshape(small[n].shape))
            off += size
        return out

    res = [loss, dx[None]]
    for kind in range(4):
        res += [o[kind] for o in big_out] + unpack(small_out[kind])
    return tuple(res)
```
